```python
import jax
import jax.numpy as jnp
from jax import lax
import numpy as np

D_MODEL = 1024
BATCH = 4
SEQ = 4096
DEPTH = 1
DEC_BATCH = 128
DEC_SEQ = 4
PAST_LEN = 2048
PAGE_SIZE = 128

H_M = 4
DK_M = 128
DV_M = 128
W_M = H_M * DV_M
H_F = 8
DH_F = 64
W_F = H_F * DH_F
D_FF = -(-8 * D_MODEL // (3 * 256)) * 256
PLE_DIM = 256
CHUNK = 128
Q_BLOCK = 128
EPS = 1e-6
F_SCALE = DH_F ** -0.5
K_SCALE_M = DK_M ** -0.5
SPLIT_SIZES = (W_M, W_M, W_M, W_M, H_M, H_M, W_F, W_F, W_F, H_F, D_MODEL, D_MODEL)
D_IN = sum(SPLIT_SIZES)

kernel_name = 'mlstm_fox_gated_hybrid_step'


def rmsnorm(x, g):
    xf = x.astype(jnp.float32)
    y = xf * lax.rsqrt(jnp.mean(xf * xf, axis=-1, keepdims=True) + EPS)
    return (y * g.astype(jnp.float32)).astype(x.dtype)


def project_in(x, g_mix, w_in, b_mi, b_mf, b_ff, g_qn, g_kn):
    bn, t = x.shape[:2]
    z = rmsnorm(x, g_mix) @ w_in
    parts = []
    off = 0
    for s in SPLIT_SIZES:
        parts.append(z[..., off:off + s])
        off += s
    mq, mk, mv, mo, mi, mf, fq, fk, fv, ff, ga, gb = parts
    mq = mq.reshape(bn, t, H_M, DK_M)
    mk = mk.reshape(bn, t, H_M, DK_M) * K_SCALE_M
    mv = mv.reshape(bn, t, H_M, DV_M)
    m_ig = (mi + b_mi).astype(jnp.float32)
    m_lf = jax.nn.log_sigmoid((mf + b_mf).astype(jnp.float32))
    fq = rmsnorm(fq.reshape(bn, t, H_F, DH_F), g_qn)
    fk = rmsnorm(fk.reshape(bn, t, H_F, DH_F), g_kn)
    fv = fv.reshape(bn, t, H_F, DH_F)
    f_lf = jax.nn.log_sigmoid((ff + b_ff).astype(jnp.float32))
    return (mq, mk, mv, mo, m_ig, m_lf), (fq, fk, fv, f_lf), (ga, gb)


def mlstm_chunk(carry, xs):
    c_st, n_st, m_st = carry
    q, k, v, ig, lf = xs
    q, k, v = (a.astype(jnp.float32).transpose(0, 2, 1, 3) for a in (q, k, v))
    ig = ig.transpose(0, 2, 1)
    lf = lf.transpose(0, 2, 1)
    length = q.shape[2]
    b = jnp.cumsum(lf, axis=-1)
    a = b + m_st[..., None]
    causal = jnp.tril(jnp.ones((length, length), dtype=bool))
    dlog = jnp.where(causal, b[..., :, None] - b[..., None, :] + ig[..., None, :], -jnp.inf)
    m_t = jnp.maximum(a, jnp.max(dlog, axis=-1))
    w_intra = jnp.exp(dlog - m_t[..., None])
    w_inter = jnp.exp(a - m_t)
    s = jnp.einsum('bhtd,bhsd->bhts', q, k) * w_intra
    num = w_inter[..., None] * jnp.einsum('bhtd,bhde->bhte', q, c_st) + jnp.einsum('bhts,bhse->bhte', s, v)
    den = w_inter * jnp.einsum('bhtd,bhd->bht', q, n_st) + jnp.sum(s, axis=-1)
    h = num / jnp.maximum(jnp.abs(den), jnp.exp(-m_t))[..., None]
    b_end = b[..., -1]
    a_end = b_end + m_st
    w_log = b_end[..., None] - b + ig
    m_new = jnp.maximum(a_end, jnp.max(w_log, axis=-1))
    w_s = jnp.exp(w_log - m_new[..., None])
    decay = jnp.exp(a_end - m_new)
    c_new = decay[..., None, None] * c_st + jnp.einsum('bhs,bhsd,bhse->bhde', w_s, k, v)
    n_new = decay[..., None] * n_st + jnp.einsum('bhs,bhsd->bhd', w_s, k)
    return (c_new, n_new, m_new), h.transpose(0, 2, 1, 3)


def mlstm_prompt(q, k, v, ig, lf):
    bn, s = q.shape[:2]
    nc = s // CHUNK

    def chunks(a):
        return a.reshape((bn, nc, CHUNK) + a.shape[2:]).swapaxes(0, 1)

    init = (jnp.zeros((bn, H_M, DK_M, DV_M), jnp.float32),
            jnp.zeros((bn, H_M, DK_M), jnp.float32),
            jnp.zeros((bn, H_M), jnp.float32))
    (c_f, n_f, m_f), h = lax.scan(mlstm_chunk, init, (chunks(q), chunks(k), chunks(v), chunks(ig), chunks(lf)))
    h = h.swapaxes(0, 1).reshape(bn, s, H_M, DV_M)
    return h, c_f, n_f, m_f


def fox_prompt(q, k, v, lf):
    bn, s = q.shape[:2]
    nb = s // Q_BLOCK
    f_cum = jnp.cumsum(lf, axis=1)
    f_keys = f_cum.transpose(0, 2, 1)
    kpos = jnp.arange(s)

    def block(args):
        qb, fb, start = args
        logits = jnp.einsum('bthd,bshd->bhts', qb, k).astype(jnp.float32) * F_SCALE
        logits = logits + fb.transpose(0, 2, 1)[..., None] - f_keys[:, :, None, :]
        qpos = start + jnp.arange(Q_BLOCK)
        logits = jnp.where(kpos[None, :] <= qpos[:, None], logits, -jnp.inf)
        p = jax.nn.softmax(logits, axis=-1)
        return jnp.einsum('bhts,bshd->bthd', p.astype(v.dtype), v)

    qs = q.reshape(bn, nb, Q_BLOCK, H_F, DH_F).swapaxes(0, 1)
    fs = f_cum.reshape(bn, nb, Q_BLOCK, H_F).swapaxes(0, 1)
    starts = jnp.arange(nb) * Q_BLOCK
    out = lax.map(block, (qs, fs, starts))
    return out.swapaxes(0, 1).reshape(bn, s, H_F, DH_F)


def fox_sample(q, k, v, lf, cache_k, cache_v, cache_lf, page_table):
    db, t = q.shape[:2]
    kp = cache_k[page_table].reshape(db, -1, H_F, DH_F)
    vp = cache_v[page_table].reshape(db, -1, H_F, DH_F)
    lfp = cache_lf[page_table].reshape(db, -1, H_F).astype(jnp.float32)
    n_past = kp.shape[1]
    past_cum = jnp.cumsum(lfp, axis=1)
    f_past = (past_cum - past_cum[:, -1:]).transpose(0, 2, 1)
    f_new = jnp.cumsum(lf, axis=1).transpose(0, 2, 1)
    lp = jnp.einsum('bthd,bshd->bhts', q, kp).astype(jnp.float32) * F_SCALE
    lp = lp + f_new[..., None] - f_past[:, :, None, :]
    ln = jnp.einsum('bthd,bshd->bhts', q, k).astype(jnp.float32) * F_SCALE
    ln = ln + f_new[..., None] - f_new[:, :, None, :]
    ln = jnp.where(jnp.tril(jnp.ones((t, t), dtype=bool)), ln, -jnp.inf)
    p = jax.nn.softmax(jnp.concatenate([lp, ln], axis=-1), axis=-1).astype(v.dtype)
    return (jnp.einsum('bhts,bshd->bthd', p[..., :n_past], vp)
            + jnp.einsum('bhts,bshd->bthd', p[..., n_past:], v))


def merge_out(x, y_m, m_o, g_mh, y_f, gates, w_pa, w_pb, w_out):
    bn, t = x.shape[:2]
    hm = rmsnorm(y_m.astype(x.dtype), g_mh) * jax.nn.sigmoid(m_o).reshape(bn, t, H_M, DV_M)
    a = hm.reshape(bn, t, W_M) @ w_pa
    b = y_f.reshape(bn, t, W_F) @ w_pb
    ga, gb = gates
    return x + (jax.nn.sigmoid(ga) * a + jax.nn.sigmoid(gb) * b) @ w_out


def ffn_and_ple(x, p, g_ffn, w_gu, w_down, g_ple, w_ple_gate, w_ple_proj):
    gu = rmsnorm(x, g_ffn) @ w_gu
    x = x + (jax.nn.silu(gu[..., :D_FF]) * gu[..., D_FF:]) @ w_down
    gate = jax.nn.sigmoid(rmsnorm(x, g_ple) @ w_ple_gate)
    return x + gate * (p.astype(x.dtype) @ w_ple_proj)


def setup_inputs(seed: int = 0) -> dict:
    key = jax.random.key(seed)
    ks = iter(jax.random.split(key, 48))
    f32 = jnp.float32

    def nrm(shape, scale):
        return jax.random.normal(next(ks), shape, f32) * scale

    n_pages = PAST_LEN // PAGE_SIZE
    n_used = DEC_BATCH * n_pages
    n_pool = n_used + n_used // 4
    page_table = jax.random.permutation(next(ks), n_pool)[:n_used].reshape(DEC_BATCH, n_pages).astype(jnp.int32)
    return {
        'x_prompt': nrm((BATCH, SEQ, D_MODEL), 1.0),
        'x_sample': nrm((DEC_BATCH, DEC_SEQ, D_MODEL), 1.0),
        'p_prompt': nrm((DEPTH, BATCH, SEQ, PLE_DIM), 1.0),
        'p_sample': nrm((DEPTH, DEC_BATCH, DEC_SEQ, PLE_DIM), 1.0),
        'cache_k': nrm((DEPTH, n_pool, PAGE_SIZE, H_F, DH_F), 1.0),
        'cache_v': nrm((DEPTH, n_pool, PAGE_SIZE, H_F, DH_F), 1.0),
        'cache_logf': jax.nn.log_sigmoid(2.5 + nrm((DEPTH, n_pool, PAGE_SIZE, H_F), 1.0)),
        'state_C': nrm((DEPTH, DEC_BATCH, H_M, DK_M, DV_M), 0.1),
        'state_n': nrm((DEPTH, DEC_BATCH, H_M, DK_M), 0.1),
        'state_m': nrm((DEPTH, DEC_BATCH, H_M), 1.0),
        'page_table': page_table,
        'g_mix': 1.0 + nrm((DEPTH, D_MODEL), 0.02),
        'w_in': nrm((DEPTH, D_MODEL, D_IN), D_MODEL ** -0.5),
        'b_mi': nrm((DEPTH, H_M), 0.1),
        'b_mf': jnp.linspace(3.0, 6.0, H_M, dtype=f32)[None, :] + nrm((DEPTH, H_M), 0.1),
        'b_ff': jnp.linspace(1.0, 4.0, H_F, dtype=f32)[None, :] + nrm((DEPTH, H_F), 0.1),
        'g_mh': 1.0 + nrm((DEPTH, H_M, DV_M), 0.02),
        'g_qn': 1.0 + nrm((DEPTH, DH_F), 0.02),
        'g_kn': 1.0 + nrm((DEPTH, DH_F), 0.02),
        'w_pa': nrm((DEPTH, W_M, D_MODEL), W_M ** -0.5),
        'w_pb': nrm((DEPTH, W_F, D_MODEL), W_F ** -0.5),
        'w_out': nrm((DEPTH, D_MODEL, D_MODEL), D_MODEL ** -0.5),
        'g_ffn': 1.0 + nrm((DEPTH, D_MODEL), 0.02),
        'w_gu': nrm((DEPTH, D_MODEL, 2 * D_FF), D_MODEL ** -0.5),
        'w_down': nrm((DEPTH, D_FF, D_MODEL), D_FF ** -0.5),
        'g_ple': 1.0 + nrm((DEPTH, D_MODEL), 0.02),
        'w_ple_gate': nrm((DEPTH, D_MODEL, D_MODEL), D_MODEL ** -0.5),
        'w_ple_proj': nrm((DEPTH, PLE_DIM, D_MODEL), PLE_DIM ** -0.5),
    }


def reference(x_prompt, x_sample, p_prompt, p_sample, cache_k, cache_v, cache_logf, state_C, state_n, state_m,
              page_table, g_mix, w_in, b_mi, b_mf, b_ff, g_mh, g_qn, g_kn, w_pa, w_pb, w_out, g_ffn, w_gu,
              w_down, g_ple, w_ple_gate, w_ple_proj):
    xp = x_prompt
    xs = x_sample
    kp_l, vp_l, lfp_l, cp_l, np_l, mp_l = [], [], [], [], [], []
    ks_l, vs_l, lfs_l, cs_l, ns_l, ms_l = [], [], [], [], [], []
    for l in range(DEPTH):
        mi, fi, gates = project_in(xp, g_mix[l], w_in[l], b_mi[l], b_mf[l], b_ff[l], g_qn[l], g_kn[l])
        mq, mk, mv, mo, mig, mlf = mi
        fq, fk, fv, flf = fi
        hm, c_f, n_f, m_f = mlstm_prompt(mq, mk, mv, mig, mlf)
        yf = fox_prompt(fq, fk, fv, flf)
        xp = merge_out(xp, hm, mo, g_mh[l], yf, gates, w_pa[l], w_pb[l], w_out[l])
        xp = ffn_and_ple(xp, p_prompt[l], g_ffn[l], w_gu[l], w_down[l], g_ple[l], w_ple_gate[l], w_ple_proj[l])
        kp_l.append(fk); vp_l.append(fv); lfp_l.append(flf)
        cp_l.append(c_f); np_l.append(n_f); mp_l.append(m_f)

        mi, fi, gates = project_in(xs, g_mix[l], w_in[l], b_mi[l], b_mf[l], b_ff[l], g_qn[l], g_kn[l])
        mq, mk, mv, mo, mig, mlf = mi
        fq, fk, fv, flf = fi
        carry = (state_C[l].astype(jnp.float32), state_n[l].astype(jnp.float32), state_m[l].astype(jnp.float32))
        (c_s, n_s, m_s), hm = mlstm_chunk(carry, (mq, mk, mv, mig, mlf))
        yf = fox_sample(fq, fk, fv, flf, cache_k[l], cache_v[l], cache_logf[l], page_table)
        xs = merge_out(xs, hm, mo, g_mh[l], yf, gates, w_pa[l], w_pb[l], w_out[l])
        xs = ffn_and_ple(xs, p_sample[l], g_ffn[l], w_gu[l], w_down[l], g_ple[l], w_ple_gate[l], w_ple_proj[l])
        ks_l.append(fk); vs_l.append(fv); lfs_l.append(flf)
        cs_l.append(c_s); ns_l.append(n_s); ms_l.append(m_s)

    return (xp, xs,
            jnp.stack(kp_l), jnp.stack(vp_l), jnp.stack(lfp_l),
            jnp.stack(cp_l), jnp.stack(np_l), jnp.stack(mp_l),
            jnp.stack(ks_l), jnp.stack(vs_l), jnp.stack(lfs_l),
            jnp.stack(cs_l), jnp.stack(ns_l), jnp.stack(ms_l))
```

```python
import functools

import jax
import jax.numpy as jnp
from jax import lax
from jax.experimental import pallas as pl
from jax.experimental.pallas import tpu as pltpu

D_MODEL = 1024
H_M = 4
DK_M = 128
DV_M = 128
W_M = H_M * DV_M
H_F = 8
DH_F = 64
W_F = H_F * DH_F
D_FF = 2816
PLE_DIM = 256
CHUNK = 128
EPS = 1e-6
F_SCALE = DH_F ** -0.5
K_SCALE_M = DK_M ** -0.5
N_GATES = 2 * H_M + H_F
GATE_ROWS = 128
LANES = 128
VMEM_LIMIT = 56 * 1024 * 1024

BF16 = jnp.bfloat16
F32 = jnp.float32
NEG_INF = float("-inf")


def _dot(a, b):
    return jnp.dot(a, b, preferred_element_type=F32)


def _dot_nt(a, b):
    return lax.dot_general(a, b, (((1,), (1,)), ((), ())), preferred_element_type=F32)


def _dot_tn(a, b):
    return lax.dot_general(a, b, (((0,), (0,)), ((), ())), preferred_element_type=F32)


def _log_sigmoid(u):
    return jnp.minimum(u, 0.0) - jnp.log1p(jnp.exp(-jnp.abs(u)))


def _rms_scale(x):
    return lax.rsqrt(jnp.mean(x * x, axis=-1, keepdims=True) + EPS)


def _params(*sem):
    return pltpu.CompilerParams(dimension_semantics=sem, vmem_limit_bytes=VMEM_LIMIT)


def _in_proj_kernel(x_ref, gmix_ref, wn_ref, wt_ref, bias_ref, gain_ref,
                    mq_ref, mk_ref, mv_ref, smo_ref, sga_ref, sgb_ref,
                    fqT_ref, fkT_ref, fvT_ref, fk_ref, gT_ref, g_ref):
    x = x_ref[0]
    xn = (x * _rms_scale(x) * gmix_ref[...]).astype(BF16)

    def zn(lo, hi):
        return _dot(xn, wn_ref[:, lo:hi])

    mq_ref[...] = zn(0, W_M).astype(BF16)
    mk_ref[...] = (zn(W_M, 2 * W_M) * K_SCALE_M).astype(BF16)
    mv_ref[...] = zn(2 * W_M, 3 * W_M).astype(BF16)
    smo_ref[...] = jax.nn.sigmoid(zn(3 * W_M, 4 * W_M)).astype(BF16)
    off = 4 * W_M
    for ref in (sga_ref, sgb_ref):
        for c in range(0, D_MODEL, W_M):
            ref[:, c:c + W_M] = jax.nn.sigmoid(zn(off + c, off + c + W_M)).astype(BF16)
        off += D_MODEL

    def zt(lo, hi):
        return _dot_nt(wt_ref[lo:hi, :], xn)

    def head_norm(z, gain):
        tm = z.shape[-1]
        r = z.reshape(H_F, DH_F, tm)
        r = r * lax.rsqrt(jnp.mean(r * r, axis=1, keepdims=True) + EPS)
        return r.reshape(W_F, tm) * gain

    fqT_ref[0] = head_norm(zt(0, W_F), gain_ref[0:W_F, :]).astype(BF16)
    fk = head_norm(zt(W_F, 2 * W_F), gain_ref[W_F:2 * W_F, :])
    fkT_ref[0] = fk
    fk_ref[...] = fk.T.astype(BF16)
    fvT_ref[0] = zt(2 * W_F, 3 * W_F)

    u = zt(3 * W_F, 3 * W_F + GATE_ROWS) + bias_ref[...]
    row = lax.broadcasted_iota(jnp.int32, u.shape, 0)
    g = jnp.where(row < H_M, u, jnp.where(row < N_GATES, _log_sigmoid(u), 0.0))
    gT_ref[0] = g
    g_ref[...] = g.T


def _in_proj(x, g_mix, wn, wt, bias_col, gain_col, tm):
    b, s, d = x.shape
    nt = s // tm
    t = b * s
    tok = lambda w: pl.BlockSpec((tm, w), lambda bi, i: (bi * nt + i, 0))
    hm = lambda r: pl.BlockSpec((1, r, tm), lambda bi, i: (bi, 0, i))
    full = lambda a: pl.BlockSpec(a.shape, lambda bi, i: (0,) * a.ndim)
    out_shape = (
        jax.ShapeDtypeStruct((t, W_M), BF16),
        jax.ShapeDtypeStruct((t, W_M), BF16),
        jax.ShapeDtypeStruct((t, W_M), BF16),
        jax.ShapeDtypeStruct((t, W_M), BF16),
        jax.ShapeDtypeStruct((t, D_MODEL), BF16),
        jax.ShapeDtypeStruct((t, D_MODEL), BF16),
        jax.ShapeDtypeStruct((b, W_F, s), BF16),
        jax.ShapeDtypeStruct((b, W_F, s), F32),
        jax.ShapeDtypeStruct((b, W_F, s), F32),
        jax.ShapeDtypeStruct((t, W_F), BF16),
        jax.ShapeDtypeStruct((b, GATE_ROWS, s), F32),
        jax.ShapeDtypeStruct((t, GATE_ROWS), F32),
    )
    out_specs = (tok(W_M), tok(W_M), tok(W_M), tok(W_M), tok(D_MODEL), tok(D_MODEL),
                 hm(W_F), hm(W_F), hm(W_F), tok(W_F), hm(GATE_ROWS), tok(GATE_ROWS))
    return pl.pallas_call(
        _in_proj_kernel,
        grid=(b, nt),
        in_specs=[pl.BlockSpec((1, tm, d), lambda bi, i: (bi, i, 0)),
                  full(g_mix), full(wn), full(wt), full(bias_col), full(gain_col)],
        out_specs=out_specs,
        out_shape=out_shape,
        compiler_params=_params("parallel", "parallel"),
        name="in_proj",
    )(x, g_mix, wn, wt, bias_col, gain_col)


def _lane_cumsum(x):
    lane = lax.broadcasted_iota(jnp.int32, x.shape, x.ndim - 1)
    sh = 1
    while sh < x.shape[-1]:
        x = x + jnp.where(lane >= sh, pltpu.roll(x, sh, x.ndim - 1), 0.0)
        sh *= 2
    return x


def _gate_scan_kernel(gT_ref, cT_ref, c_ref):
    s = gT_ref.shape[-1]
    rows = 2 * 8
    row = lax.broadcasted_iota(jnp.int32, (rows, CHUNK), 0)
    carry = jnp.zeros((rows, 1), F32)
    pad = jnp.zeros((GATE_ROWS - rows, CHUNK), F32)
    for c in range(s // CHUNK):
        sl = slice(c * CHUNK, (c + 1) * CHUNK)
        g = gT_ref[0, 0:rows, sl]
        cs = _lane_cumsum(g)
        out = jnp.where(row < H_M, g, jnp.where(row < 2 * H_M, cs, cs + carry))
        carry = carry + cs[:, CHUNK - 1:CHUNK]
        full = jnp.concatenate([out, pad], axis=0)
        cT_ref[0, :, sl] = full
        c_ref[sl, :] = full.T


def _gate_scan(gT):
    b, r, s = gT.shape
    return pl.pallas_call(
        _gate_scan_kernel,
        grid=(b,),
        in_specs=[pl.BlockSpec((1, r, s), lambda bi: (bi, 0, 0))],
        out_specs=(pl.BlockSpec((1, r, s), lambda bi: (bi, 0, 0)),
                   pl.BlockSpec((s, r), lambda bi: (bi, 0))),
        out_shape=(jax.ShapeDtypeStruct((b, r, s), F32), jax.ShapeDtypeStruct((b * s, r), F32)),
        compiler_params=_params("parallel"),
        name="gate_scan",
    )(gT)


def _mlstm_chunk(q, k, v, ig_row, b_row, ig_col, b_col, caug, m_st, n_valid=None):
    length = q.shape[0]
    row = lax.broadcasted_iota(jnp.int32, (length, length), 0)
    col = lax.broadcasted_iota(jnp.int32, (length, length), 1)
    valid = col <= row
    if n_valid is not None:
        valid = valid & (col < n_valid)
    s = _dot_nt(q, k)
    dlog = jnp.where(valid, b_col - b_row + ig_row, NEG_INF)
    a_col = b_col + m_st
    m_t = jnp.maximum(a_col, jnp.max(dlog, axis=1, keepdims=True))
    w_intra = jnp.exp(dlog - m_t)
    w_inter = jnp.exp(a_col - m_t)
    lane = lax.broadcasted_iota(jnp.int32, (length, DV_M), 1)
    v_aug = jnp.concatenate([v, jnp.where(lane == 0, 1.0, 0.0).astype(BF16)], axis=1)
    r1 = _dot((s * w_intra).astype(BF16), v_aug)
    r2 = _dot(q, caug.astype(BF16))
    num = w_inter * r2[:, :DV_M] + r1[:, :DV_M]
    den = w_inter * r2[:, DV_M:DV_M + 1] + r1[:, DV_M:DV_M + 1]
    h = num / jnp.maximum(jnp.abs(den), jnp.exp(-m_t))

    last = length - 1 if n_valid is None else n_valid - 1
    b_end = b_row[:, last:last + 1]
    a_end = b_end + m_st
    wlog = b_end - b_col + ig_col
    if n_valid is not None:
        wlog = jnp.where(lax.broadcasted_iota(jnp.int32, wlog.shape, 0) < n_valid, wlog, NEG_INF)
    m_new = jnp.maximum(a_end, jnp.max(wlog, axis=0, keepdims=True))
    ws = jnp.exp(wlog - m_new)
    decay = jnp.exp(a_end - m_new)
    kw = (k.astype(F32) * ws).astype(BF16)
    caug_new = decay * caug + _dot_tn(kw, v_aug)
    return h, caug_new, m_new


def _mlstm_prompt_kernel(mq_ref, mk_ref, mv_ref, smo_ref, c_ref, cT_ref, gmh_ref,
                         hm_ref, caug_ref, m_ref):
    @pl.when(pl.program_id(1) == 0)
    def _():
        caug_ref[...] = jnp.zeros_like(caug_ref)
        m_ref[...] = jnp.zeros_like(m_ref)

    for h in range(H_M):
        sl = slice(h * DK_M, (h + 1) * DK_M)
        hh, caug, m_new = _mlstm_chunk(
            mq_ref[:, sl], mk_ref[:, sl], mv_ref[:, sl],
            cT_ref[0, h:h + 1, :], cT_ref[0, H_M + h:H_M + h + 1, :],
            c_ref[:, h:h + 1], c_ref[:, H_M + h:H_M + h + 1],
            caug_ref[0, h], m_ref[0, h:h + 1, 0:1])
        hn = hh * _rms_scale(hh) * gmh_ref[:, sl]
        hm_ref[:, sl] = (hn * smo_ref[:, sl].astype(F32)).astype(BF16)
        caug_ref[0, h] = caug
        m_ref[0, h:h + 1, :] = jnp.broadcast_to(m_new, (1, LANES))


def _mlstm_prompt(mq, mk, mv, smo, c_tok, cT, gmh, b, s):
    nc = s // CHUNK
    tok = lambda w: pl.BlockSpec((CHUNK, w), lambda bi, c: (bi * nc + c, 0))
    return pl.pallas_call(
        _mlstm_prompt_kernel,
        grid=(b, nc),
        in_specs=[tok(W_M), tok(W_M), tok(W_M), tok(W_M), tok(GATE_ROWS),
                  pl.BlockSpec((1, GATE_ROWS, CHUNK), lambda bi, c: (bi, 0, c)),
                  pl.BlockSpec(gmh.shape, lambda bi, c: (0, 0))],
        out_specs=(tok(W_M),
                   pl.BlockSpec((1, H_M, DK_M, 2 * DV_M), lambda bi, c: (bi, 0, 0, 0)),
                   pl.BlockSpec((1, 8, LANES), lambda bi, c: (bi, 0, 0))),
        out_shape=(jax.ShapeDtypeStruct((b * s, W_M), BF16),
                   jax.ShapeDtypeStruct((b, H_M, DK_M, 2 * DV_M), F32),
                   jax.ShapeDtypeStruct((b, 8, LANES), F32)),
        compiler_params=_params("parallel", "arbitrary"),
        name="mlstm_prompt",
    )(mq, mk, mv, smo, c_tok, cT, gmh)


def _fox_prompt_kernel(fqT_ref, fk_ref, fvT_ref, c_ref, o_ref, m_sc, l_sc, acc_sc, *, tq, tk):
    qi = pl.program_id(1)
    ratio = tq // tk
    rowq = lax.broadcasted_iota(jnp.int32, (2 * DH_F, tq), 0)
    srow = lax.broadcasted_iota(jnp.int32, (tk, tq), 0)
    tcol = lax.broadcasted_iota(jnp.int32, (tk, tq), 1)

    for pair in range(H_F // 2):
        psl = slice(pair * 2 * DH_F, (pair + 1) * 2 * DH_F)
        q2 = fqT_ref[0, psl, :]
        qh = (jnp.where(rowq < DH_F, q2, 0), jnp.where(rowq >= DH_F, q2, 0))
        m_sc[...] = jnp.full_like(m_sc, NEG_INF)
        l_sc[...] = jnp.zeros_like(l_sc)
        acc_sc[...] = jnp.zeros_like(acc_sc)

        def step(j, diag_off):
            k0 = pl.multiple_of(j * tk, tk)
            k2 = fk_ref[pl.ds(k0, tk), psl]
            for hh in range(2):
                h = 2 * pair + hh
                st = _dot(k2, qh[hh])
                st = st - c_ref[pl.ds(k0, tk), 2 * H_M + h:2 * H_M + h + 1]
                if diag_off is not None:
                    st = jnp.where(srow + diag_off <= tcol, st, NEG_INF)
                m_old = m_sc[hh]
                m_new = jnp.maximum(m_old, jnp.max(st, axis=0, keepdims=True))
                alpha = jnp.exp(m_old - m_new)
                p = jnp.exp(st - m_new)
                l_sc[hh] = alpha * l_sc[hh] + jnp.sum(p, axis=0, keepdims=True)
                vT = fvT_ref[0, pair * 2 * DH_F + hh * DH_F:pair * 2 * DH_F + (hh + 1) * DH_F,
                             pl.ds(k0, tk)].astype(BF16)
                acc_sc[hh] = alpha * acc_sc[hh] + _dot(vT, p.astype(BF16))
                m_sc[hh] = m_new

        def body(j, carry):
            step(j, None)
            return carry

        lax.fori_loop(0, qi * ratio, body, 0)
        for d in range(ratio):
            step(qi * ratio + d, d * tk)
        for hh in range(2):
            o_ref[0, pair * 2 * DH_F + hh * DH_F:pair * 2 * DH_F + (hh + 1) * DH_F, :] = (
                acc_sc[hh] / l_sc[hh]).astype(BF16)


def _fox_prompt(fqT, fk_tok, fvT, c_tok, tq, tk):
    b, w, s = fqT.shape
    kern = functools.partial(_fox_prompt_kernel, tq=tq, tk=tk)
    return pl.pallas_call(
        kern,
        grid=(b, s // tq),
        in_specs=[pl.BlockSpec((1, w, tq), lambda bi, qi: (bi, 0, qi)),
                  pl.BlockSpec((s, w), lambda bi, qi: (bi, 0)),
                  pl.BlockSpec((1, w, s), lambda bi, qi: (bi, 0, 0)),
                  pl.BlockSpec((s, GATE_ROWS), lambda bi, qi: (bi, 0))],
        out_specs=pl.BlockSpec((1, w, tq), lambda bi, qi: (bi, 0, qi)),
        out_shape=jax.ShapeDtypeStruct((b, w, s), BF16),
        scratch_shapes=[pltpu.VMEM((2, 1, tq), F32), pltpu.VMEM((2, 1, tq), F32),
                        pltpu.VMEM((2, DH_F, tq), F32)],
        compiler_params=_params("parallel", "arbitrary"),
        name="fox_prompt",
    )(fqT, fk_tok, fvT, c_tok)


FF_TILE = 256


def _merge_ffn_kernel(x_ref, hm_ref, yfT_ref, sga_ref, sgb_ref, p_ref,
                      wpa_ref, wpb_ref, wout_ref, gffn_ref, wgu_ref, wdown_ref,
                      gple_ref, wpg_ref, wpp_ref, y_ref, hdn_sc):
    a = _dot(hm_ref[...], wpa_ref[...])
    bb = _dot_tn(yfT_ref[0], wpb_ref[...])
    u = sga_ref[...].astype(F32) * a + sgb_ref[...].astype(F32) * bb
    x1 = x_ref[...] + _dot(u.astype(BF16), wout_ref[...])
    xn = (x1 * _rms_scale(x1) * gffn_ref[...]).astype(BF16)
    for j in range(0, D_FF, FF_TILE):
        g = _dot(xn, wgu_ref[:, j:j + FF_TILE])
        up = _dot(xn, wgu_ref[:, D_FF + j:D_FF + j + FF_TILE])
        hdn_sc[:, j:j + FF_TILE] = (jax.nn.silu(g) * up).astype(BF16)
    x2 = x1 + _dot(hdn_sc[...], wdown_ref[...])
    xg = (x2 * _rms_scale(x2) * gple_ref[...]).astype(BF16)
    gate = jax.nn.sigmoid(_dot(xg, wpg_ref[...]))
    y_ref[...] = x2 + gate * _dot(p_ref[...].astype(BF16), wpp_ref[...])


def _merge_ffn(x, hm, yfT, sga, sgb, p, wpa, wpb, wout, gffn, wgu, wdown, gple, wpg, wpp, tm):
    t, d = x.shape
    b, _, s = yfT.shape
    nt = s // tm
    tok = lambda w: pl.BlockSpec((tm, w), lambda i: (i, 0))
    full = lambda a: pl.BlockSpec(a.shape, lambda i: (0,) * a.ndim)
    return pl.pallas_call(
        _merge_ffn_kernel,
        grid=(t // tm,),
        in_specs=[tok(d), tok(W_M),
                  pl.BlockSpec((1, W_F, tm), lambda i: (i // nt, 0, i % nt)),
                  tok(d), tok(d), tok(PLE_DIM),
                  full(wpa), full(wpb), full(wout), full(gffn), full(wgu), full(wdown),
                  full(gple), full(wpg), full(wpp)],
        out_specs=tok(d),
        out_shape=jax.ShapeDtypeStruct((t, d), F32),
        scratch_shapes=[pltpu.VMEM((tm, D_FF), BF16)],
        compiler_params=_params("parallel"),
        name="merge_ffn",
    )(x, hm, yfT, sga, sgb, p, wpa, wpb, wout, gffn, wgu, wdown, gple, wpg, wpp)


_SPLIT = (W_M, W_M, W_M, W_M, H_M, H_M, W_F, W_F, W_F, H_F, D_MODEL, D_MODEL)


def _prep_weights(g_mix, w_in, b_mi, b_mf, b_ff, g_mh, g_qn, g_kn, w_pa, w_pb, w_out, g_ffn, w_gu,
                  w_down, g_ple, w_ple_gate, w_ple_proj):
    offs = [0]
    for sz in _SPLIT:
        offs.append(offs[-1] + sz)
    o_mq, _, _, _, o_mi, o_mf, o_fq, _, _, o_ff, o_ga, _, o_end = offs
    w = w_in[0]
    wT = w.T
    wn = jnp.concatenate([w[:, o_mq:o_mi], w[:, o_ga:o_end]], axis=1).astype(BF16)
    wt = jnp.concatenate([wT[o_fq:o_ff], wT[o_mi:o_fq], wT[o_ff:o_ga],
                          jnp.zeros((GATE_ROWS - N_GATES, D_MODEL), F32)], axis=0).astype(BF16)
    bias_col = jnp.concatenate([b_mi[0], b_mf[0], b_ff[0], jnp.zeros((GATE_ROWS - N_GATES,), F32)])[:, None]
    gain_col = jnp.concatenate([jnp.tile(g_qn[0], H_F) * F_SCALE, jnp.tile(g_kn[0], H_F)])[:, None]
    return dict(
        g_mix=g_mix, wn=wn, wt=wt, bias_col=bias_col, gain_col=gain_col,
        gmh=g_mh[0].reshape(1, W_M),
        wpa=w_pa[0].astype(BF16), wpb=w_pb[0].astype(BF16), wout=w_out[0].astype(BF16),
        gffn=g_ffn, wgu=w_gu[0].astype(BF16), wdown=w_down[0].astype(BF16),
        gple=g_ple, wpg=w_ple_gate[0].astype(BF16), wpp=w_ple_proj[0].astype(BF16))


def _prompt_path(x, p, wd, tm_in, tm_ffn, tq, tk):
    b, s, d = x.shape
    (mq, mk, mv, smo, sga, sgb, fqT, fkT, fvT, fk_tok, gT, _) = _in_proj(
        x, wd["g_mix"], wd["wn"], wd["wt"], wd["bias_col"], wd["gain_col"], tm_in)
    cT, c_tok = _gate_scan(gT)
    hm, caug, m_out = _mlstm_prompt(mq, mk, mv, smo, c_tok, cT, wd["gmh"], b, s)
    yfT = _fox_prompt(fqT, fk_tok, fvT, c_tok, tq, tk)
    y = _merge_ffn(x.reshape(b * s, d), hm, yfT, sga, sgb, p.reshape(b * s, PLE_DIM),
                   wd["wpa"], wd["wpb"], wd["wout"], wd["gffn"], wd["wgu"], wd["wdown"],
                   wd["gple"], wd["wpg"], wd["wpp"], tm_ffn)
    to_cache = lambda aT: aT.reshape(b, H_F, DH_F, s).transpose(0, 3, 1, 2)[None]
    logf = gT[:, 2 * H_M:N_GATES, :].transpose(0, 2, 1)[None]
    return (y.reshape(b, s, d), to_cache(fkT), to_cache(fvT), logf,
            caug[None, :, :, :, :DV_M], caug[None, :, :, :, DV_M], m_out[None, :, :H_M, 0])


L_PAD = 16
SEQ_BLOCK = 8


def _mlstm_sample_kernel(q_ref, k_ref, v_ref, smo_ref, grow_ref, gcol_ref, c_ref, n_ref, m_ref, gmh_ref,
                         hm_ref, co_ref, no_ref, mo_ref, *, n_valid):
    length = L_PAD
    row = lax.broadcasted_iota(jnp.int32, (length, length), 0)
    col = lax.broadcasted_iota(jnp.int32, (length, length), 1)
    valid = (col <= row) & (col < n_valid)
    rowc = lax.broadcasted_iota(jnp.int32, (length, 1), 0)
    for i in range(SEQ_BLOCK):
        for h in range(H_M):
            sl = slice(h * DK_M, (h + 1) * DK_M)
            q, k, v = q_ref[i, :, sl], k_ref[i, :, sl], v_ref[i, :, sl]
            ig_row = grow_ref[i, h:h + 1, 0:length]
            lf_row = grow_ref[i, H_M + h:H_M + h + 1, 0:length]
            ig_col = gcol_ref[i, :, h:h + 1]
            lf_col = gcol_ref[i, :, H_M + h:H_M + h + 1]
            b_col = jnp.sum(jnp.where(col <= row, lf_row, 0.0), axis=1, keepdims=True)
            b_row = jnp.sum(jnp.where(row <= col, lf_col, 0.0), axis=0, keepdims=True)
            m_st = m_ref[i:i + 1, h:h + 1]
            c_st = c_ref[i, h]
            n_st = n_ref[i, h:h + 1, :]

            s = _dot_nt(q, k)
            dlog = jnp.where(valid, b_col - b_row + ig_row, NEG_INF)
            a_col = b_col + m_st
            m_t = jnp.maximum(a_col, jnp.max(dlog, axis=1, keepdims=True))
            sw = s * jnp.exp(dlog - m_t)
            w_inter = jnp.exp(a_col - m_t)
            num = w_inter * _dot(q, c_st.astype(BF16)) + _dot(sw.astype(BF16), v)
            den = (w_inter * jnp.sum(q.astype(F32) * n_st, axis=1, keepdims=True)
                   + jnp.sum(sw, axis=1, keepdims=True))
            hh = num / jnp.maximum(jnp.abs(den), jnp.exp(-m_t))
            hn = hh * _rms_scale(hh) * gmh_ref[:, sl]
            hm_ref[i, :, sl] = (hn * smo_ref[i, :, sl].astype(F32)).astype(BF16)

            b_end = b_row[:, n_valid - 1:n_valid]
            a_end = b_end + m_st
            wlog = jnp.where(rowc < n_valid, b_end - b_col + ig_col, NEG_INF)
            m_new = jnp.maximum(a_end, jnp.max(wlog, axis=0, keepdims=True))
            decay = jnp.exp(a_end - m_new)
            kw = k.astype(F32) * jnp.exp(wlog - m_new)
            co_ref[i, h] = decay * c_st + _dot_tn(kw.astype(BF16), v)
            no_ref[i, h:h + 1, :] = decay * n_st + jnp.sum(kw, axis=0, keepdims=True)
            mo_ref[i:i + 1, h:h + 1] = m_new


def _mlstm_sample(q, k, v, smo, grow, gcol, c_st, n_st, m_st, gmh, n_valid):
    db = q.shape[0]
    blk = lambda a: pl.BlockSpec((SEQ_BLOCK,) + a.shape[1:], lambda i: (i,) + (0,) * (a.ndim - 1))
    kern = functools.partial(_mlstm_sample_kernel, n_valid=n_valid)
    return pl.pallas_call(
        kern,
        grid=(db // SEQ_BLOCK,),
        in_specs=[blk(q), blk(k), blk(v), blk(smo), blk(grow), blk(gcol), blk(c_st), blk(n_st), blk(m_st),
                  pl.BlockSpec(gmh.shape, lambda i: (0, 0))],
        out_specs=(blk(q), blk(c_st), blk(n_st), blk(m_st)),
        out_shape=(jax.ShapeDtypeStruct(q.shape, BF16), jax.ShapeDtypeStruct(c_st.shape, F32),
                   jax.ShapeDtypeStruct(n_st.shape, F32), jax.ShapeDtypeStruct(m_st.shape, F32)),
        compiler_params=_params("parallel"),
        name="mlstm_sample",
    )(q, k, v, smo, grow, gcol, c_st, n_st, m_st, gmh)


def _fox_sample_kernel(pt_ref, fq_ref, kn_ref, vn_ref, lfn_ref, *refs, n_pages, n_tok):
    del pt_ref
    k_refs = refs[:n_pages]
    v_refs = refs[n_pages:2 * n_pages]
    lf_refs = refs[2 * n_pages:3 * n_pages]
    o_ref = refs[3 * n_pages]
    rows = n_tok * H_F
    sub = lax.broadcasted_iota(jnp.int32, (H_F, W_F), 0)
    head_of_lane = lax.broadcasted_iota(jnp.int32, (H_F, W_F), 1) // DH_F
    own = sub == head_of_lane
    tile = lambda a: jnp.concatenate([a] * n_tok, axis=0)

    fq = fq_ref[0]
    qbd = jnp.concatenate(
        [jnp.where(own, jnp.broadcast_to(fq[t:t + 1, :], (H_F, W_F)), 0.0) for t in range(n_tok)], axis=0)
    qbd16 = qbd.astype(BF16)

    bias = [None] * n_pages
    carry = jnp.zeros((H_F, 1), F32)
    for pg in reversed(range(n_pages)):
        incl = _lane_cumsum(lf_refs[pg][0])
        tot = incl[:, LANES - 1:LANES]
        bias[pg] = tot - incl + carry
        carry = carry + tot

    logits = []
    for j in range(0, n_pages, 2):
        kcat = jnp.concatenate([k_refs[j][0].astype(BF16), k_refs[j + 1][0].astype(BF16)], axis=1)
        bcat = jnp.concatenate([tile(bias[j]), tile(bias[j + 1])], axis=1)
        logits.append(_dot(qbd16, kcat) + bcat)

    rowi = lax.broadcasted_iota(jnp.int32, (rows, 1), 0)
    qf = qbd16.astype(F32)
    lfn = lfn_ref[0]
    new_logits = []
    f_new = jnp.zeros((H_F, 1), F32)
    for s in range(n_tok):
        f_new = f_new + lfn[:, s:s + 1]
        kn = kn_ref[0, s:s + 1, :].astype(BF16).astype(F32)
        ln = jnp.sum(qf * kn, axis=1, keepdims=True) - tile(f_new)
        new_logits.append(jnp.where(rowi >= s * H_F, ln, NEG_INF))

    m = new_logits[0]
    for ln in new_logits[1:]:
        m = jnp.maximum(m, ln)
    for lg in logits:
        m = jnp.maximum(m, jnp.max(lg, axis=1, keepdims=True))

    denom = jnp.zeros((rows, 1), F32)
    acc = jnp.zeros((rows, W_F), F32)
    for idx, lg in enumerate(logits):
        p = jnp.exp(lg - m)
        denom = denom + jnp.sum(p, axis=1, keepdims=True)
        j = 2 * idx
        vcat = jnp.concatenate([v_refs[j][0].astype(BF16), v_refs[j + 1][0].astype(BF16)], axis=1)
        acc = acc + _dot_nt(p.astype(BF16), vcat)
    for s, ln in enumerate(new_logits):
        p = jnp.exp(ln - m)
        denom = denom + p
        acc = acc + p.astype(BF16).astype(F32) * vn_ref[0, s:s + 1, :]
    out = acc / denom
    for t in range(n_tok):
        blk = jnp.where(own, out[t * H_F:(t + 1) * H_F, :], 0.0)
        o_ref[0, t:t + 1, :] = jnp.sum(blk, axis=0, keepdims=True)


def _fox_sample(page_table, fq, kn, vn, lfn, ck, cv, cl):
    db, n_tok, _ = fq.shape
    n_pages = page_table.shape[1]
    page = ck.shape[-1]
    seq = lambda a: pl.BlockSpec((1,) + a.shape[1:], lambda b, pt: (b,) + (0,) * (a.ndim - 1))

    def paged(rows, j):
        return pl.BlockSpec((1, rows, page), lambda b, pt: (pt[b, j], 0, 0))

    in_specs = ([seq(fq), seq(kn), seq(vn), seq(lfn)]
                + [paged(W_F, j) for j in range(n_pages)]
                + [paged(W_F, j) for j in range(n_pages)]
                + [paged(H_F, j) for j in range(n_pages)])
    kern = functools.partial(_fox_sample_kernel, n_pages=n_pages, n_tok=n_tok)
    return pl.pallas_call(
        kern,
        grid_spec=pltpu.PrefetchScalarGridSpec(
            num_scalar_prefetch=1, grid=(db,), in_specs=in_specs, out_specs=seq(fq)),
        out_shape=jax.ShapeDtypeStruct(fq.shape, F32),
        compiler_params=_params("arbitrary"),
        name="fox_sample",
    )(page_table, fq, kn, vn, lfn, *([ck] * n_pages), *([cv] * n_pages), *([cl] * n_pages))


def _sample_path(x, p, cache_k, cache_v, cache_logf, state_c, state_n, state_m, page_table, wd):
    db, n_tok, d = x.shape
    t = db * n_tok
    (mq, mk, mv, smo, sga, sgb, fqT, fkT, fvT, fk_tok, gT, g_tok) = _in_proj(
        x.reshape(1, t, d), wd["g_mix"], wd["wn"], wd["wt"], wd["bias_col"], wd["gain_col"], t)

    pad_tok = lambda a: jnp.pad(a.reshape(db, n_tok, -1), ((0, 0), (0, L_PAD - n_tok), (0, 0)))
    gcol = pad_tok(g_tok)
    grow = jnp.pad(gT[0, :2 * 8].reshape(2 * 8, db, n_tok).transpose(1, 0, 2),
                   ((0, 0), (0, 0), (0, LANES - n_tok)))
    hm, c_new, n_new, m_new = _mlstm_sample(
        pad_tok(mq), pad_tok(mk), pad_tok(mv), pad_tok(smo), grow, gcol,
        state_c, state_n, state_m, wd["gmh"], n_tok)
    hm = hm[:, :n_tok].reshape(t, W_M)

    pool, page = cache_k.shape[0], cache_k.shape[1]
    ck = cache_k.transpose(0, 2, 3, 1).reshape(pool, W_F, page)
    cv = cache_v.transpose(0, 2, 3, 1).reshape(pool, W_F, page)
    cl = cache_logf.transpose(0, 2, 1)
    tok3 = lambda aT: aT[0].T.astype(F32).reshape(db, n_tok, W_F)
    lfn = jnp.pad(gT[0, 2 * H_M:N_GATES].reshape(H_F, db, n_tok).transpose(1, 0, 2),
                  ((0, 0), (0, 0), (0, LANES - n_tok)))
    yf = _fox_sample(page_table, tok3(fqT), fk_tok.astype(F32).reshape(db, n_tok, W_F), tok3(fvT),
                     lfn, ck, cv, cl)
    yfT = yf.reshape(t, W_F).T.astype(BF16)[None]

    y = _merge_ffn(x.reshape(t, d), hm, yfT, sga, sgb, p.reshape(t, PLE_DIM),
                   wd["wpa"], wd["wpb"], wd["wout"], wd["gffn"], wd["wgu"], wd["wdown"],
                   wd["gple"], wd["wpg"], wd["wpp"], t)
    to_cache = lambda aT: aT[0].reshape(H_F, DH_F, db, n_tok).transpose(2, 3, 0, 1)[None]
    logf = gT[0, 2 * H_M:N_GATES].reshape(H_F, db, n_tok).transpose(1, 2, 0)[None]
    return (y.reshape(db, n_tok, d), to_cache(fkT), to_cache(fvT), logf,
            c_new[None], n_new[None], m_new[None])


def kernel(x_prompt, x_sample, p_prompt, p_sample, cache_k, cache_v, cache_logf, state_C, state_n, state_m,
           page_table, g_mix, w_in, b_mi, b_mf, b_ff, g_mh, g_qn, g_kn, w_pa, w_pb, w_out, g_ffn, w_gu,
           w_down, g_ple, w_ple_gate, w_ple_proj):
    wd = _prep_weights(g_mix, w_in, b_mi, b_mf, b_ff, g_mh, g_qn, g_kn, w_pa, w_pb, w_out, g_ffn, w_gu,
                       w_down, g_ple, w_ple_gate, w_ple_proj)
    yp, kp, vp, lfp, cp, np_, mp = _prompt_path(x_prompt, p_prompt[0], wd, 512, 512, 512, 256)
    ys, ks, vs, lfs, cs, ns, ms = _sample_path(
        x_sample, p_sample[0], cache_k[0], cache_v[0], cache_logf[0],
        state_C[0], state_n[0], state_m[0], page_table, wd)
    return (yp, ys, kp, vp, lfp, cp, np_, mp, ks, vs, lfs, cs, ns, ms)
```

```python
import functools

import jax
import jax.numpy as jnp
from jax import lax
from jax.experimental import pallas as pl
from jax.experimental.pallas import tpu as pltpu

D_MODEL = 1024
H_M = 4
DK_M = 128
DV_M = 128
W_M = H_M * DV_M
H_F = 8
DH_F = 64
W_F = H_F * DH_F
D_FF = 2816
PLE_DIM = 256
CHUNK = 128
EPS = 1e-6
F_SCALE = DH_F ** -0.5
K_SCALE_M = DK_M ** -0.5
LOG2E = 1.4426950408889634
N_GATES = 2 * H_M + H_F
GATE_ROWS = 128
N_SPLIT = 3
LANES = 128
VMEM_LIMIT = 56 * 1024 * 1024

BF16 = jnp.bfloat16
F32 = jnp.float32
NEG_INF = float("-inf")


def _dot(a, b):
    return jnp.dot(a, b, preferred_element_type=F32)


def _dot_nt(a, b):
    return lax.dot_general(a, b, (((1,), (1,)), ((), ())), preferred_element_type=F32)


def _dot_tn(a, b):
    return lax.dot_general(a, b, (((0,), (0,)), ((), ())), preferred_element_type=F32)


def _log_sigmoid(u):
    return jnp.minimum(u, 0.0) - jnp.log1p(jnp.exp(-jnp.abs(u)))


def _rms_scale(x):
    return lax.rsqrt(jnp.mean(x * x, axis=-1, keepdims=True) + EPS)


def _params(*sem):
    return pltpu.CompilerParams(dimension_semantics=sem, vmem_limit_bytes=VMEM_LIMIT)


def _in_proj_kernel(x_ref, gmix_ref, wn_ref, wt_ref, bias_ref, gain_ref,
                    mq_ref, mk_ref, mv_ref, smo_ref, sga_ref, sgb_ref,
                    fqT_ref, fkT_ref, fvT_ref, fk_ref, fvT16_ref, gT_ref):
    x = x_ref[0]
    xn = (x * _rms_scale(x) * gmix_ref[...]).astype(BF16)

    def zn(lo, hi):
        return _dot(xn, wn_ref[:, lo:hi])

    mq_ref[...] = zn(0, W_M).astype(BF16)
    mk_ref[...] = (zn(W_M, 2 * W_M) * K_SCALE_M).astype(BF16)
    mv_ref[...] = zn(2 * W_M, 3 * W_M).astype(BF16)
    smo_ref[...] = jax.nn.sigmoid(zn(3 * W_M, 4 * W_M)).astype(BF16)
    off = 4 * W_M
    for ref in (sga_ref, sgb_ref):
        for c in range(0, D_MODEL, W_M):
            ref[:, c:c + W_M] = jax.nn.sigmoid(zn(off + c, off + c + W_M)).astype(BF16)
        off += D_MODEL

    def zt(lo, hi):
        return _dot_nt(wt_ref[lo:hi, :], xn)

    def head_norm(z, gain):
        tm = z.shape[-1]
        r = z.reshape(H_F, DH_F, tm)
        r = r * lax.rsqrt(jnp.mean(r * r, axis=1, keepdims=True) + EPS)
        return r.reshape(W_F, tm) * gain

    fqT_ref[0] = head_norm(zt(0, W_F), gain_ref[0:W_F, :]).astype(BF16)
    fk = head_norm(zt(W_F, 2 * W_F), gain_ref[W_F:2 * W_F, :])
    fkT_ref[0] = fk
    fk_ref[...] = fk.T.astype(BF16)
    fv = zt(2 * W_F, 3 * W_F)
    fvT_ref[0] = fv
    fvT16_ref[0] = fv.astype(BF16)

    u = zt(3 * W_F, 3 * W_F + GATE_ROWS) + bias_ref[...]
    row = lax.broadcasted_iota(jnp.int32, u.shape, 0)
    gT_ref[0] = jnp.where(row < H_M, u, jnp.where(row < N_GATES, _log_sigmoid(u), 0.0))


def _in_proj(x, g_mix, wn, wt, bias_col, gain_col, tm):
    b, s, d = x.shape
    nt = s // tm
    t = b * s
    tok = lambda w: pl.BlockSpec((tm, w), lambda bi, i: (bi * nt + i, 0))
    hm = lambda r: pl.BlockSpec((1, r, tm), lambda bi, i: (bi, 0, i))
    full = lambda a: pl.BlockSpec(a.shape, lambda bi, i: (0,) * a.ndim)
    out_shape = (
        jax.ShapeDtypeStruct((t, W_M), BF16),
        jax.ShapeDtypeStruct((t, W_M), BF16),
        jax.ShapeDtypeStruct((t, W_M), BF16),
        jax.ShapeDtypeStruct((t, W_M), BF16),
        jax.ShapeDtypeStruct((t, D_MODEL), BF16),
        jax.ShapeDtypeStruct((t, D_MODEL), BF16),
        jax.ShapeDtypeStruct((b, W_F, s), BF16),
        jax.ShapeDtypeStruct((b, W_F, s), F32),
        jax.ShapeDtypeStruct((b, W_F, s), F32),
        jax.ShapeDtypeStruct((t, W_F), BF16),
        jax.ShapeDtypeStruct((b, W_F, s), BF16),
        jax.ShapeDtypeStruct((b, GATE_ROWS, s), F32),
    )
    out_specs = (tok(W_M), tok(W_M), tok(W_M), tok(W_M), tok(D_MODEL), tok(D_MODEL),
                 hm(W_F), hm(W_F), hm(W_F), tok(W_F), hm(W_F), hm(GATE_ROWS))
    return pl.pallas_call(
        _in_proj_kernel,
        grid=(b, nt),
        in_specs=[pl.BlockSpec((1, tm, d), lambda bi, i: (bi, i, 0)),
                  full(g_mix), full(wn), full(wt), full(bias_col), full(gain_col)],
        out_specs=out_specs,
        out_shape=out_shape,
        compiler_params=_params("parallel", "parallel"),
        name="in_proj",
    )(x, g_mix, wn, wt, bias_col, gain_col)


def _lane_cumsum(x):
    lane = lax.broadcasted_iota(jnp.int32, x.shape, x.ndim - 1)
    sh = 1
    while sh < x.shape[-1]:
        x = x + jnp.where(lane >= sh, pltpu.roll(x, sh, x.ndim - 1), 0.0)
        sh *= 2
    return x


def _split_bf16(x):
    pieces = []
    for _ in range(N_SPLIT):
        p = x.astype(BF16).astype(F32)
        pieces.append(p)
        x = x - p
    return pieces


def _gate_scan_kernel(gT_ref, cT_ref, c_ref, e_ref):
    s = gT_ref.shape[-1]
    rows = 2 * 8
    row = lax.broadcasted_iota(jnp.int32, (rows, CHUNK), 0)
    carry = jnp.zeros((rows, 1), F32)
    pad = jnp.zeros((GATE_ROWS - rows, CHUNK), F32)
    epad = jnp.zeros((GATE_ROWS - N_SPLIT * H_F, CHUNK), F32)
    for c in range(s // CHUNK):
        sl = slice(c * CHUNK, (c + 1) * CHUNK)
        g = gT_ref[0, 0:rows, sl]
        cs = _lane_cumsum(g)
        out = jnp.where(row < H_M, g, jnp.where(row < 2 * H_M, cs, cs + carry))
        carry = carry + cs[:, CHUNK - 1:CHUNK]
        full = jnp.concatenate([out, pad], axis=0)
        cT_ref[0, :, sl] = full
        c_ref[sl, :] = full.T
        pieces = _split_bf16(out[2 * H_M:N_GATES, :] * LOG2E)
        e_ref[sl, :] = jnp.concatenate(pieces + [epad], axis=0).T.astype(BF16)


def _gate_scan(gT):
    b, r, s = gT.shape
    return pl.pallas_call(
        _gate_scan_kernel,
        grid=(b,),
        in_specs=[pl.BlockSpec((1, r, s), lambda bi: (bi, 0, 0))],
        out_specs=(pl.BlockSpec((1, r, s), lambda bi: (bi, 0, 0)),
                   pl.BlockSpec((s, r), lambda bi: (bi, 0)),
                   pl.BlockSpec((s, r), lambda bi: (bi, 0))),
        out_shape=(jax.ShapeDtypeStruct((b, r, s), F32), jax.ShapeDtypeStruct((b * s, r), F32),
                   jax.ShapeDtypeStruct((b * s, r), BF16)),
        compiler_params=_params("parallel"),
        name="gate_scan",
    )(gT)


def _each(f, *lists):
    return [f(*a) for a in zip(*lists)]


def _mlstm_gates(ig_row, b_row, ig_col, b_col, m_st, valid, n_valid):
    length = b_col[0].shape[0]
    dlog = _each(lambda bc, br, ir: jnp.where(valid, bc - br + ir, NEG_INF), b_col, b_row, ig_row)
    a_col = _each(lambda bc, m: bc + m, b_col, m_st)
    mx = _each(lambda d: jnp.max(d, axis=1, keepdims=True), dlog)
    m_t = _each(jnp.maximum, a_col, mx)
    w_intra = _each(lambda d, m: jnp.exp(d - m), dlog, m_t)
    w_inter = _each(lambda a, m: jnp.exp(a - m), a_col, m_t)
    floor = _each(lambda m: jnp.exp(-m), m_t)
    b_end = _each(lambda br: br[:, n_valid - 1:n_valid], b_row)
    a_end = _each(lambda be, m: be + m, b_end, m_st)
    wlog = _each(lambda be, bc, ic: be - bc + ic, b_end, b_col, ig_col)
    if n_valid < length:
        keep = lax.broadcasted_iota(jnp.int32, (length, 1), 0) < n_valid
        wlog = _each(lambda w: jnp.where(keep, w, NEG_INF), wlog)
    wmax = _each(lambda w: jnp.max(w, axis=0, keepdims=True), wlog)
    m_new = _each(jnp.maximum, a_end, wmax)
    ws = _each(lambda w, m: jnp.exp(w - m), wlog, m_new)
    decay = _each(lambda a, m: jnp.exp(a - m), a_end, m_new)
    return w_intra, w_inter, floor, ws, decay, m_new


def _mlstm_prompt_kernel(mq_ref, mk_ref, mv_ref, smo_ref, c_ref, cT_ref, gmh_ref,
                         hm_ref, caug_ref, m_ref):
    nb = mq_ref.shape[0]

    @pl.when(pl.program_id(0) == 0)
    def _():
        caug_ref[...] = jnp.zeros_like(caug_ref)
        m_ref[...] = jnp.zeros_like(m_ref)

    row = lax.broadcasted_iota(jnp.int32, (CHUNK, CHUNK), 0)
    col = lax.broadcasted_iota(jnp.int32, (CHUNK, CHUNK), 1)
    valid = col <= row
    lane = lax.broadcasted_iota(jnp.int32, (CHUNK, DV_M), 1)
    ones_col = jnp.where(lane == 0, 1.0, 0.0).astype(BF16)
    chains = [(b, h) for b in range(nb) for h in range(H_M)]
    sl = lambda h: slice(h * DK_M, (h + 1) * DK_M)

    q = [mq_ref[b, :, sl(h)] for b, h in chains]
    k = [mk_ref[b, :, sl(h)] for b, h in chains]
    v_aug = [jnp.concatenate([mv_ref[b, :, sl(h)], ones_col], axis=1) for b, h in chains]
    s_mat = _each(_dot_nt, q, k)
    r2 = _each(lambda qq, bh: _dot(qq, caug_ref[bh[0], bh[1]].astype(BF16)), q, chains)

    w_intra, w_inter, floor, ws, decay, m_new = _mlstm_gates(
        [cT_ref[b, h:h + 1, :] for b, h in chains],
        [cT_ref[b, H_M + h:H_M + h + 1, :] for b, h in chains],
        [c_ref[b, :, h:h + 1] for b, h in chains],
        [c_ref[b, :, H_M + h:H_M + h + 1] for b, h in chains],
        [m_ref[b, h:h + 1, 0:1] for b, h in chains], valid, CHUNK)
    sw = _each(lambda s, w: (s * w).astype(BF16), s_mat, w_intra)
    kw = _each(lambda kk, w: (kk.astype(F32) * w).astype(BF16), k, ws)

    r1 = _each(_dot, sw, v_aug)
    upd = _each(_dot_tn, kw, v_aug)

    num = _each(lambda w, a, c: w * a[:, :DV_M] + c[:, :DV_M], w_inter, r2, r1)
    den = _each(lambda w, a, c: w * a[:, DV_M:DV_M + 1] + c[:, DV_M:DV_M + 1], w_inter, r2, r1)
    hh = _each(lambda n, d, f: n / jnp.maximum(jnp.abs(d), f), num, den, floor)
    scale = _each(_rms_scale, hh)
    for (b, h), x, sc, dc, u, mn in zip(chains, hh, scale, decay, upd, m_new):
        hm_ref[b, :, sl(h)] = (x * sc * gmh_ref[:, sl(h)] * smo_ref[b, :, sl(h)].astype(F32)).astype(BF16)
        caug_ref[b, h] = dc * caug_ref[b, h] + u
        m_ref[b, h:h + 1, :] = jnp.broadcast_to(mn, (1, LANES))


def _mlstm_prompt(mq, mk, mv, smo, c_tok, cT, gmh):
    b, s, _ = mq.shape
    tok = lambda w: pl.BlockSpec((b, CHUNK, w), lambda c: (0, c, 0))
    whole = lambda shape: pl.BlockSpec(shape, lambda c: (0,) * len(shape))
    state = (b, H_M, DK_M, 2 * DV_M)
    return pl.pallas_call(
        _mlstm_prompt_kernel,
        grid=(s // CHUNK,),
        in_specs=[tok(W_M), tok(W_M), tok(W_M), tok(W_M), tok(GATE_ROWS),
                  pl.BlockSpec((b, GATE_ROWS, CHUNK), lambda c: (0, 0, c)),
                  whole(gmh.shape)],
        out_specs=(tok(W_M), whole(state), whole((b, 8, LANES))),
        out_shape=(jax.ShapeDtypeStruct((b, s, W_M), BF16),
                   jax.ShapeDtypeStruct(state, F32),
                   jax.ShapeDtypeStruct((b, 8, LANES), F32)),
        compiler_params=_params("arbitrary"),
        name="mlstm_prompt",
    )(mq, mk, mv, smo, c_tok, cT, gmh)


HEAD_GROUP = 2


def _fox_prompt_kernel(fqT_ref, fk_ref, e_ref, vT_ref, o_ref, qa_sc, m_sc, l_sc, acc_sc, *, tq, tk):
    qi = pl.program_id(1)
    ratio = tq // tk
    rowq = lax.broadcasted_iota(jnp.int32, (2 * DH_F, tq), 0)
    srow = lax.broadcasted_iota(jnp.int32, (tk, tq), 0)
    tcol = lax.broadcasted_iota(jnp.int32, (tk, tq), 1)
    hsl = lambda h: slice(h * DH_F, (h + 1) * DH_F)
    psl = lambda h: slice((h // 2) * 2 * DH_F, (h // 2 + 1) * 2 * DH_F)

    for h in range(H_F):
        q2 = fqT_ref[0, psl(h), :]
        lo = (h % 2) * DH_F
        qh = jnp.where((rowq >= lo) & (rowq < lo + DH_F), q2, 0)
        sel = jnp.where((rowq % H_F == h) & (rowq < N_SPLIT * H_F), -1.0, 0.0).astype(BF16)
        qa_sc[h] = jnp.concatenate([qh, sel], axis=0)
    m_sc[...] = jnp.full_like(m_sc, NEG_INF)
    l_sc[...] = jnp.zeros_like(l_sc)
    acc_sc[...] = jnp.zeros_like(acc_sc)

    def step(j, diag_off):
        k0 = pl.multiple_of(j * tk, tk)
        e = e_ref[pl.ds(k0, tk), :]
        groups = [range(g0, g0 + HEAD_GROUP) for g0 in range(0, H_F, HEAD_GROUP)]
        st = {}

        def scores(heads):
            for h in heads:
                ka = jnp.concatenate([fk_ref[pl.ds(k0, tk), psl(h)], e], axis=1)
                st[h] = _dot(ka, qa_sc[h])

        scores(groups[0])
        for gi, heads in enumerate(groups):
            if gi + 1 < len(groups):
                scores(groups[gi + 1])
            p, alpha = {}, {}
            for h in heads:
                s_h = st[h]
                if diag_off is not None:
                    s_h = jnp.where(srow + diag_off <= tcol, s_h, NEG_INF)
                m_old = m_sc[h]
                m_new = jnp.maximum(m_old, jnp.max(s_h, axis=0, keepdims=True))
                alpha[h] = jnp.exp2(m_old - m_new)
                p_h = jnp.exp2(s_h - m_new)
                l_sc[h] = alpha[h] * l_sc[h] + jnp.sum(p_h, axis=0, keepdims=True)
                m_sc[h] = m_new
                p[h] = p_h.astype(BF16)
            for h in heads:
                vT = vT_ref[0, hsl(h), pl.ds(k0, tk)]
                acc_sc[h] = alpha[h] * acc_sc[h] + _dot(vT, p[h])

    def body(j, carry):
        step(j, None)
        return carry

    lax.fori_loop(0, qi * ratio, body, 0)
    for d in range(ratio):
        step(qi * ratio + d, d * tk)
    for h in range(H_F):
        o_ref[0, hsl(h), :] = (acc_sc[h] / l_sc[h]).astype(BF16)


def _fox_prompt(fqT, fk_tok, e_tok, fvT16, tq, tk):
    b, w, s = fqT.shape
    kern = functools.partial(_fox_prompt_kernel, tq=tq, tk=tk)
    return pl.pallas_call(
        kern,
        grid=(b, s // tq),
        in_specs=[pl.BlockSpec((1, w, tq), lambda bi, qi: (bi, 0, qi)),
                  pl.BlockSpec((s, w), lambda bi, qi: (bi, 0)),
                  pl.BlockSpec((s, GATE_ROWS), lambda bi, qi: (bi, 0)),
                  pl.BlockSpec((1, w, s), lambda bi, qi: (bi, 0, 0))],
        out_specs=pl.BlockSpec((1, w, tq), lambda bi, qi: (bi, 0, qi)),
        out_shape=jax.ShapeDtypeStruct((b, w, s), BF16),
        scratch_shapes=[pltpu.VMEM((H_F, 2 * DH_F + GATE_ROWS, tq), BF16),
                        pltpu.VMEM((H_F, 1, tq), F32), pltpu.VMEM((H_F, 1, tq), F32),
                        pltpu.VMEM((H_F, DH_F, tq), F32)],
        compiler_params=_params("parallel", "arbitrary"),
        name="fox_prompt",
    )(fqT, fk_tok, e_tok, fvT16)


FF_TILE = 256


def _merge_ffn_kernel(x_ref, hm_ref, yfT_ref, sga_ref, sgb_ref, p_ref,
                      wpa_ref, wpb_ref, wout_ref, gffn_ref, wgu_ref, wdown_ref,
                      gple_ref, wpg_ref, wpp_ref, y_ref, hdn_sc):
    a = _dot(hm_ref[...], wpa_ref[...])
    bb = _dot_tn(yfT_ref[0], wpb_ref[...])
    u = sga_ref[...].astype(F32) * a + sgb_ref[...].astype(F32) * bb
    x1 = x_ref[...] + _dot(u.astype(BF16), wout_ref[...])
    xn = (x1 * _rms_scale(x1) * gffn_ref[...]).astype(BF16)
    for j in range(0, D_FF, FF_TILE):
        g = _dot(xn, wgu_ref[:, j:j + FF_TILE])
        up = _dot(xn, wgu_ref[:, D_FF + j:D_FF + j + FF_TILE])
        hdn_sc[:, j:j + FF_TILE] = (jax.nn.silu(g) * up).astype(BF16)
    x2 = x1 + _dot(hdn_sc[...], wdown_ref[...])
    xg = (x2 * _rms_scale(x2) * gple_ref[...]).astype(BF16)
    gate = jax.nn.sigmoid(_dot(xg, wpg_ref[...]))
    y_ref[...] = x2 + gate * _dot(p_ref[...].astype(BF16), wpp_ref[...])


def _merge_ffn(x, hm, yfT, sga, sgb, p, wpa, wpb, wout, gffn, wgu, wdown, gple, wpg, wpp, tm):
    t, d = x.shape
    b, _, s = yfT.shape
    nt = s // tm
    tok = lambda w: pl.BlockSpec((tm, w), lambda i: (i, 0))
    full = lambda a: pl.BlockSpec(a.shape, lambda i: (0,) * a.ndim)
    return pl.pallas_call(
        _merge_ffn_kernel,
        grid=(t // tm,),
        in_specs=[tok(d), tok(W_M),
                  pl.BlockSpec((1, W_F, tm), lambda i: (i // nt, 0, i % nt)),
                  tok(d), tok(d), tok(PLE_DIM),
                  full(wpa), full(wpb), full(wout), full(gffn), full(wgu), full(wdown),
                  full(gple), full(wpg), full(wpp)],
        out_specs=tok(d),
        out_shape=jax.ShapeDtypeStruct((t, d), F32),
        scratch_shapes=[pltpu.VMEM((tm, D_FF), BF16)],
        compiler_params=_params("parallel"),
        name="merge_ffn",
    )(x, hm, yfT, sga, sgb, p, wpa, wpb, wout, gffn, wgu, wdown, gple, wpg, wpp)


_SPLIT = (W_M, W_M, W_M, W_M, H_M, H_M, W_F, W_F, W_F, H_F, D_MODEL, D_MODEL)


def _prep_weights(g_mix, w_in, b_mi, b_mf, b_ff, g_mh, g_qn, g_kn, w_pa, w_pb, w_out, g_ffn, w_gu,
                  w_down, g_ple, w_ple_gate, w_ple_proj):
    offs = [0]
    for sz in _SPLIT:
        offs.append(offs[-1] + sz)
    o_mq, _, _, _, o_mi, o_mf, o_fq, _, _, o_ff, o_ga, _, o_end = offs
    w = w_in[0]
    wT = w.T
    wn = jnp.concatenate([w[:, o_mq:o_mi], w[:, o_ga:o_end]], axis=1).astype(BF16)
    wt = jnp.concatenate([wT[o_fq:o_ff], wT[o_mi:o_fq], wT[o_ff:o_ga],
                          jnp.zeros((GATE_ROWS - N_GATES, D_MODEL), F32)], axis=0).astype(BF16)
    bias_col = jnp.concatenate([b_mi[0], b_mf[0], b_ff[0], jnp.zeros((GATE_ROWS - N_GATES,), F32)])[:, None]
    gain_col = jnp.concatenate([jnp.tile(g_qn[0], H_F) * (F_SCALE * LOG2E), jnp.tile(g_kn[0], H_F)])[:, None]
    return dict(
        g_mix=g_mix, wn=wn, wt=wt, bias_col=bias_col, gain_col=gain_col,
        gmh=g_mh[0].reshape(1, W_M),
        wpa=w_pa[0].astype(BF16), wpb=w_pb[0].astype(BF16), wout=w_out[0].astype(BF16),
        gffn=g_ffn, wgu=w_gu[0].astype(BF16), wdown=w_down[0].astype(BF16),
        gple=g_ple, wpg=w_ple_gate[0].astype(BF16), wpp=w_ple_proj[0].astype(BF16))


def _prompt_path(x, p, wd, tm_in, tm_ffn, tq, tk):
    b, s, d = x.shape
    (mq, mk, mv, smo, sga, sgb, fqT, fkT, fvT, fk_tok, fvT16, gT) = _in_proj(
        x, wd["g_mix"], wd["wn"], wd["wt"], wd["bias_col"], wd["gain_col"], tm_in)
    cT, c_tok, e_tok = _gate_scan(gT)
    seq3 = lambda a: a.reshape(b, s, a.shape[-1])
    hm, caug, m_out = _mlstm_prompt(seq3(mq), seq3(mk), seq3(mv), seq3(smo), seq3(c_tok), cT, wd["gmh"])
    yfT = _fox_prompt(fqT, fk_tok, e_tok, fvT16, tq, tk)
    y = _merge_ffn(x.reshape(b * s, d), hm.reshape(b * s, W_M), yfT, sga, sgb, p.reshape(b * s, PLE_DIM),
                   wd["wpa"], wd["wpb"], wd["wout"], wd["gffn"], wd["wgu"], wd["wdown"],
                   wd["gple"], wd["wpg"], wd["wpp"], tm_ffn)
    to_cache = lambda aT: aT.reshape(b, H_F, DH_F, s).transpose(0, 3, 1, 2)[None]
    logf = gT[:, 2 * H_M:N_GATES, :].transpose(0, 2, 1)[None]
    return (y.reshape(b, s, d), to_cache(fkT), to_cache(fvT), logf,
            caug[None, :, :, :, :DV_M], caug[None, :, :, :, DV_M], m_out[None, :, :H_M, 0])


L_PAD = 16
SEQ_BLOCK = 8


def _mlstm_sample_kernel(q_ref, k_ref, v_ref, smo_ref, grow_ref, gcol_ref, c_ref, n_ref, m_ref, gmh_ref,
                         hm_ref, co_ref, no_ref, mo_ref, *, n_valid):
    length = L_PAD
    row = lax.broadcasted_iota(jnp.int32, (length, length), 0)
    col = lax.broadcasted_iota(jnp.int32, (length, length), 1)
    valid = (col <= row) & (col < n_valid)
    chains = [(i, h) for i in range(SEQ_BLOCK) for h in range(H_M)]
    sl = lambda h: slice(h * DK_M, (h + 1) * DK_M)

    q = [q_ref[i, :, sl(h)] for i, h in chains]
    k = [k_ref[i, :, sl(h)] for i, h in chains]
    v = [v_ref[i, :, sl(h)] for i, h in chains]
    n_st = [n_ref[i, h:h + 1, :] for i, h in chains]
    s_mat = _each(_dot_nt, q, k)
    r2 = _each(lambda qq, ih: _dot(qq, c_ref[ih[0], ih[1]].astype(BF16)), q, chains)
    qn = _each(lambda qq, n: jnp.sum(qq.astype(F32) * n, axis=1, keepdims=True), q, n_st)

    lf_row = [grow_ref[i, H_M + h:H_M + h + 1, 0:length] for i, h in chains]
    lf_col = [gcol_ref[i, :, H_M + h:H_M + h + 1] for i, h in chains]
    b_col = _each(lambda r: jnp.sum(jnp.where(col <= row, r, 0.0), axis=1, keepdims=True), lf_row)
    b_row = _each(lambda c: jnp.sum(jnp.where(row <= col, c, 0.0), axis=0, keepdims=True), lf_col)
    w_intra, w_inter, floor, ws, decay, m_new = _mlstm_gates(
        [grow_ref[i, h:h + 1, 0:length] for i, h in chains], b_row,
        [gcol_ref[i, :, h:h + 1] for i, h in chains], b_col,
        [m_ref[i:i + 1, h:h + 1] for i, h in chains], valid, n_valid)
    sw = _each(lambda s, w: s * w, s_mat, w_intra)
    kw = _each(lambda kk, w: kk.astype(F32) * w, k, ws)

    r1 = _each(lambda s, vv: _dot(s.astype(BF16), vv), sw, v)
    upd = _each(lambda kk, vv: _dot_tn(kk.astype(BF16), vv), kw, v)
    sw_sum = _each(lambda s: jnp.sum(s, axis=1, keepdims=True), sw)
    kw_sum = _each(lambda kk: jnp.sum(kk, axis=0, keepdims=True), kw)

    num = _each(lambda w, a, c: w * a + c, w_inter, r2, r1)
    den = _each(lambda w, a, c: w * a + c, w_inter, qn, sw_sum)
    hh = _each(lambda n, d, f: n / jnp.maximum(jnp.abs(d), f), num, den, floor)
    scale = _each(_rms_scale, hh)
    for (i, h), x, sc, dc, u, n, ks, mn in zip(chains, hh, scale, decay, upd, n_st, kw_sum, m_new):
        hm_ref[i, :, sl(h)] = (x * sc * gmh_ref[:, sl(h)] * smo_ref[i, :, sl(h)].astype(F32)).astype(BF16)
        co_ref[i, h] = dc * c_ref[i, h] + u
        no_ref[i, h:h + 1, :] = dc * n + ks
        mo_ref[i:i + 1, h:h + 1] = mn


def _mlstm_sample(q, k, v, smo, grow, gcol, c_st, n_st, m_st, gmh, n_valid):
    db = q.shape[0]
    blk = lambda a: pl.BlockSpec((SEQ_BLOCK,) + a.shape[1:], lambda i: (i,) + (0,) * (a.ndim - 1))
    kern = functools.partial(_mlstm_sample_kernel, n_valid=n_valid)
    return pl.pallas_call(
        kern,
        grid=(db // SEQ_BLOCK,),
        in_specs=[blk(q), blk(k), blk(v), blk(smo), blk(grow), blk(gcol), blk(c_st), blk(n_st), blk(m_st),
                  pl.BlockSpec(gmh.shape, lambda i: (0, 0))],
        out_specs=(blk(q), blk(c_st), blk(n_st), blk(m_st)),
        out_shape=(jax.ShapeDtypeStruct(q.shape, BF16), jax.ShapeDtypeStruct(c_st.shape, F32),
                   jax.ShapeDtypeStruct(n_st.shape, F32), jax.ShapeDtypeStruct(m_st.shape, F32)),
        compiler_params=_params("parallel"),
        name="mlstm_sample",
    )(q, k, v, smo, grow, gcol, c_st, n_st, m_st, gmh)


def _fox_sample_kernel(pt_ref, fq_ref, kn_ref, vn_ref, lfn_ref, *refs, n_pages, n_tok):
    del pt_ref
    k_refs = refs[:n_pages]
    v_refs = refs[n_pages:2 * n_pages]
    lf_refs = refs[2 * n_pages:3 * n_pages]
    o_ref = refs[3 * n_pages]
    rows = n_tok * H_F
    sub = lax.broadcasted_iota(jnp.int32, (H_F, W_F), 0)
    head_of_lane = lax.broadcasted_iota(jnp.int32, (H_F, W_F), 1) // DH_F
    own = sub == head_of_lane
    tile = lambda a: jnp.concatenate([a] * n_tok, axis=0)

    fq = fq_ref[0]
    qbd = jnp.concatenate(
        [jnp.where(own, jnp.broadcast_to(fq[t:t + 1, :], (H_F, W_F)), 0.0) for t in range(n_tok)], axis=0)
    qbd16 = qbd.astype(BF16)

    incl_all = _lane_cumsum(jnp.concatenate([lf_refs[pg][0] for pg in range(n_pages)], axis=0))
    bias = [None] * n_pages
    carry = jnp.zeros((H_F, 1), F32)
    for pg in reversed(range(n_pages)):
        incl = incl_all[pg * H_F:(pg + 1) * H_F, :]
        tot = incl[:, LANES - 1:LANES]
        bias[pg] = (tot - incl + carry) * LOG2E
        carry = carry + tot

    logits = []
    for j in range(0, n_pages, 2):
        kcat = jnp.concatenate([k_refs[j][0].astype(BF16), k_refs[j + 1][0].astype(BF16)], axis=1)
        bcat = jnp.concatenate([tile(bias[j]), tile(bias[j + 1])], axis=1)
        logits.append(_dot(qbd16, kcat) + bcat)

    rowi = lax.broadcasted_iota(jnp.int32, (rows, 1), 0)
    qf = qbd16.astype(F32)
    lfn = lfn_ref[0]
    new_logits = []
    f_new = jnp.zeros((H_F, 1), F32)
    for s in range(n_tok):
        f_new = f_new + lfn[:, s:s + 1]
        kn = kn_ref[0, s:s + 1, :].astype(BF16).astype(F32)
        ln = jnp.sum(qf * kn, axis=1, keepdims=True) - tile(f_new) * LOG2E
        new_logits.append(jnp.where(rowi >= s * H_F, ln, NEG_INF))

    m = new_logits[0]
    for ln in new_logits[1:]:
        m = jnp.maximum(m, ln)
    for lg in logits:
        m = jnp.maximum(m, jnp.max(lg, axis=1, keepdims=True))

    denom = jnp.zeros((rows, 1), F32)
    acc = jnp.zeros((rows, W_F), F32)
    for idx, lg in enumerate(logits):
        p = jnp.exp2(lg - m)
        denom = denom + jnp.sum(p, axis=1, keepdims=True)
        j = 2 * idx
        vcat = jnp.concatenate([v_refs[j][0].astype(BF16), v_refs[j + 1][0].astype(BF16)], axis=1)
        acc = acc + _dot_nt(p.astype(BF16), vcat)
    for s, ln in enumerate(new_logits):
        p = jnp.exp2(ln - m)
        denom = denom + p
        acc = acc + p.astype(BF16).astype(F32) * vn_ref[0, s:s + 1, :]
    out = acc / denom
    for t in range(n_tok):
        blk = jnp.where(own, out[t * H_F:(t + 1) * H_F, :], 0.0)
        o_ref[0, t:t + 1, :] = jnp.sum(blk, axis=0, keepdims=True)


def _fox_sample(page_table, fq, kn, vn, lfn, ck, cv, cl):
    db, n_tok, _ = fq.shape
    n_pages = page_table.shape[1]
    page = ck.shape[-1]
    seq = lambda a: pl.BlockSpec((1,) + a.shape[1:], lambda b, pt: (b,) + (0,) * (a.ndim - 1))

    def paged(rows, j):
        return pl.BlockSpec((1, rows, page), lambda b, pt: (pt[b, j], 0, 0))

    in_specs = ([seq(fq), seq(kn), seq(vn), seq(lfn)]
                + [paged(W_F, j) for j in range(n_pages)]
                + [paged(W_F, j) for j in range(n_pages)]
                + [paged(H_F, j) for j in range(n_pages)])
    kern = functools.partial(_fox_sample_kernel, n_pages=n_pages, n_tok=n_tok)
    return pl.pallas_call(
        kern,
        grid_spec=pltpu.PrefetchScalarGridSpec(
            num_scalar_prefetch=1, grid=(db,), in_specs=in_specs, out_specs=seq(fq)),
        out_shape=jax.ShapeDtypeStruct(fq.shape, F32),
        compiler_params=_params("arbitrary"),
        name="fox_sample",
    )(page_table, fq, kn, vn, lfn, *([ck] * n_pages), *([cv] * n_pages), *([cl] * n_pages))


def _sample_path(x, p, cache_k, cache_v, cache_logf, state_c, state_n, state_m, page_table, wd):
    db, n_tok, d = x.shape
    t = db * n_tok
    (mq, mk, mv, smo, sga, sgb, fqT, fkT, fvT, fk_tok, _, gT) = _in_proj(
        x.reshape(1, t, d), wd["g_mix"], wd["wn"], wd["wt"], wd["bias_col"], wd["gain_col"], t)

    pad_tok = lambda a: jnp.pad(a.reshape(db, n_tok, -1), ((0, 0), (0, L_PAD - n_tok), (0, 0)))
    g16 = gT[0, :2 * 8].reshape(2 * 8, db, n_tok)
    grow = jnp.pad(g16.transpose(1, 0, 2), ((0, 0), (0, 0), (0, LANES - n_tok)))
    gcol = jnp.pad(g16.transpose(1, 2, 0), ((0, 0), (0, L_PAD - n_tok), (0, LANES - 2 * 8)))
    hm, c_new, n_new, m_new = _mlstm_sample(
        pad_tok(mq), pad_tok(mk), pad_tok(mv), pad_tok(smo), grow, gcol,
        state_c, state_n, state_m, wd["gmh"], n_tok)
    hm = hm[:, :n_tok].reshape(t, W_M)

    pool, page = cache_k.shape[0], cache_k.shape[1]
    ck = cache_k.transpose(0, 2, 3, 1).reshape(pool, W_F, page)
    cv = cache_v.transpose(0, 2, 3, 1).reshape(pool, W_F, page)
    cl = cache_logf.transpose(0, 2, 1)
    tok3 = lambda aT: aT[0].T.astype(F32).reshape(db, n_tok, W_F)
    lfn = jnp.pad(gT[0, 2 * H_M:N_GATES].reshape(H_F, db, n_tok).transpose(1, 0, 2),
                  ((0, 0), (0, 0), (0, LANES - n_tok)))
    yf = _fox_sample(page_table, tok3(fqT), fk_tok.astype(F32).reshape(db, n_tok, W_F), tok3(fvT),
                     lfn, ck, cv, cl)
    yfT = yf.reshape(t, W_F).T.astype(BF16)[None]

    y = _merge_ffn(x.reshape(t, d), hm, yfT, sga, sgb, p.reshape(t, PLE_DIM),
                   wd["wpa"], wd["wpb"], wd["wout"], wd["gffn"], wd["wgu"], wd["wdown"],
                   wd["gple"], wd["wpg"], wd["wpp"], t)
    to_cache = lambda aT: aT[0].reshape(H_F, DH_F, db, n_tok).transpose(2, 3, 0, 1)[None]
    logf = gT[0, 2 * H_M:N_GATES].reshape(H_F, db, n_tok).transpose(1, 2, 0)[None]
    return (y.reshape(db, n_tok, d), to_cache(fkT), to_cache(fvT), logf,
            c_new[None], n_new[None], m_new[None])


def kernel(x_prompt, x_sample, p_prompt, p_sample, cache_k, cache_v, cache_logf, state_C, state_n, state_m,
           page_table, g_mix, w_in, b_mi, b_mf, b_ff, g_mh, g_qn, g_kn, w_pa, w_pb, w_out, g_ffn, w_gu,
           w_down, g_ple, w_ple_gate, w_ple_proj):
    wd = _prep_weights(g_mix, w_in, b_mi, b_mf, b_ff, g_mh, g_qn, g_kn, w_pa, w_pb, w_out, g_ffn, w_gu,
                       w_down, g_ple, w_ple_gate, w_ple_proj)
    yp, kp, vp, lfp, cp, np_, mp = _prompt_path(x_prompt, p_prompt[0], wd, 512, 512, 512, 256)
    ys, ks, vs, lfs, cs, ns, ms = _sample_path(
        x_sample, p_sample[0], cache_k[0], cache_v[0], cache_logf[0],
        state_C[0], state_n[0], state_m[0], page_table, wd)
    return (yp, ys, kp, vp, lfp, cp, np_, mp, ks, vs, lfs, cs, ns, ms)
```

```python
import functools

import jax
import jax.numpy as jnp
from jax import lax
from jax.experimental import pallas as pl
from jax.experimental.pallas import tpu as pltpu

D_MODEL = 1024
H_M = 4
DK_M = 128
DV_M = 128
W_M = H_M * DV_M
H_F = 8
DH_F = 64
W_F = H_F * DH_F
D_FF = 2816
PLE_DIM = 256
CHUNK = 128
EPS = 1e-6
F_SCALE = DH_F ** -0.5
K_SCALE_M = DK_M ** -0.5
LOG2E = 1.4426950408889634
N_GATES = 2 * H_M + H_F
GATE_ROWS = 128
N_SPLIT = 3
V_ROWS = DH_F + 16
LANES = 128
VMEM_LIMIT = 56 * 1024 * 1024

BF16 = jnp.bfloat16
F32 = jnp.float32
NEG_INF = float("-inf")


def _dot(a, b):
    return jnp.dot(a, b, preferred_element_type=F32)


def _dot_nt(a, b):
    return lax.dot_general(a, b, (((1,), (1,)), ((), ())), preferred_element_type=F32)


def _dot_tn(a, b):
    return lax.dot_general(a, b, (((0,), (0,)), ((), ())), preferred_element_type=F32)


def _log_sigmoid(u):
    return jnp.minimum(u, 0.0) - jnp.log1p(jnp.exp(-jnp.abs(u)))


def _rms_scale(x):
    return lax.rsqrt(jnp.mean(x * x, axis=-1, keepdims=True) + EPS)


def _params(*sem):
    return pltpu.CompilerParams(dimension_semantics=sem, vmem_limit_bytes=VMEM_LIMIT)


def _in_proj_kernel(x_ref, gmix_ref, wn_ref, wt_ref, bias_ref, gain_ref,
                    mq_ref, mk_ref, mv_ref, smo_ref, sga_ref, sgb_ref,
                    fqT_ref, fkT_ref, fvT_ref, fk_ref, fvT16_ref, gT_ref):
    x = x_ref[0]
    xn = (x * _rms_scale(x) * gmix_ref[...]).astype(BF16)

    def zn(lo, hi):
        return _dot(xn, wn_ref[:, lo:hi])

    mq_ref[...] = zn(0, W_M).astype(BF16)
    mk_ref[...] = (zn(W_M, 2 * W_M) * K_SCALE_M).astype(BF16)
    mv_ref[...] = zn(2 * W_M, 3 * W_M).astype(BF16)
    smo_ref[...] = jax.nn.sigmoid(zn(3 * W_M, 4 * W_M)).astype(BF16)
    off = 4 * W_M
    for ref in (sga_ref, sgb_ref):
        for c in range(0, D_MODEL, W_M):
            ref[:, c:c + W_M] = jax.nn.sigmoid(zn(off + c, off + c + W_M)).astype(BF16)
        off += D_MODEL

    def zt(lo, hi):
        return _dot_nt(wt_ref[lo:hi, :], xn)

    def head_norm(z, gain):
        tm = z.shape[-1]
        r = z.reshape(H_F, DH_F, tm)
        r = r * lax.rsqrt(jnp.mean(r * r, axis=1, keepdims=True) + EPS)
        return r.reshape(W_F, tm) * gain

    fqT_ref[0] = head_norm(zt(0, W_F), gain_ref[0:W_F, :]).astype(BF16)
    fk = head_norm(zt(W_F, 2 * W_F), gain_ref[W_F:2 * W_F, :])
    fkT_ref[0] = fk
    fk_ref[...] = fk.T.astype(BF16)
    fv = zt(2 * W_F, 3 * W_F)
    fvT_ref[0] = fv
    tm = fv.shape[-1]
    fv_aug = jnp.concatenate([fv.reshape(H_F, DH_F, tm), jnp.ones((H_F, V_ROWS - DH_F, tm), F32)], axis=1)
    fvT16_ref[0] = fv_aug.reshape(H_F * V_ROWS, tm).astype(BF16)

    u = zt(3 * W_F, 3 * W_F + GATE_ROWS) + bias_ref[...]
    row = lax.broadcasted_iota(jnp.int32, u.shape, 0)
    gT_ref[0] = jnp.where(row < H_M, u, jnp.where(row < N_GATES, _log_sigmoid(u), 0.0))


def _in_proj(x, g_mix, wn, wt, bias_col, gain_col, tm):
    b, s, d = x.shape
    nt = s // tm
    t = b * s
    tok = lambda w: pl.BlockSpec((tm, w), lambda bi, i: (bi * nt + i, 0))
    hm = lambda r: pl.BlockSpec((1, r, tm), lambda bi, i: (bi, 0, i))
    full = lambda a: pl.BlockSpec(a.shape, lambda bi, i: (0,) * a.ndim)
    out_shape = (
        jax.ShapeDtypeStruct((t, W_M), BF16),
        jax.ShapeDtypeStruct((t, W_M), BF16),
        jax.ShapeDtypeStruct((t, W_M), BF16),
        jax.ShapeDtypeStruct((t, W_M), BF16),
        jax.ShapeDtypeStruct((t, D_MODEL), BF16),
        jax.ShapeDtypeStruct((t, D_MODEL), BF16),
        jax.ShapeDtypeStruct((b, W_F, s), BF16),
        jax.ShapeDtypeStruct((b, W_F, s), F32),
        jax.ShapeDtypeStruct((b, W_F, s), F32),
        jax.ShapeDtypeStruct((t, W_F), BF16),
        jax.ShapeDtypeStruct((b, H_F * V_ROWS, s), BF16),
        jax.ShapeDtypeStruct((b, GATE_ROWS, s), F32),
    )
    out_specs = (tok(W_M), tok(W_M), tok(W_M), tok(W_M), tok(D_MODEL), tok(D_MODEL),
                 hm(W_F), hm(W_F), hm(W_F), tok(W_F), hm(H_F * V_ROWS), hm(GATE_ROWS))
    return pl.pallas_call(
        _in_proj_kernel,
        grid=(b, nt),
        in_specs=[pl.BlockSpec((1, tm, d), lambda bi, i: (bi, i, 0)),
                  full(g_mix), full(wn), full(wt), full(bias_col), full(gain_col)],
        out_specs=out_specs,
        out_shape=out_shape,
        compiler_params=_params("parallel", "parallel"),
        name="in_proj",
    )(x, g_mix, wn, wt, bias_col, gain_col)


def _lane_cumsum(x):
    lane = lax.broadcasted_iota(jnp.int32, x.shape, x.ndim - 1)
    sh = 1
    while sh < x.shape[-1]:
        x = x + jnp.where(lane >= sh, pltpu.roll(x, sh, x.ndim - 1), 0.0)
        sh *= 2
    return x


def _split_bf16(x):
    pieces = []
    for _ in range(N_SPLIT):
        p = x.astype(BF16).astype(F32)
        pieces.append(p)
        x = x - p
    return pieces


def _gate_scan_kernel(gT_ref, cT_ref, c_ref, e_ref):
    s = gT_ref.shape[-1]
    rows = 2 * 8
    row = lax.broadcasted_iota(jnp.int32, (rows, CHUNK), 0)
    pad = jnp.zeros((GATE_ROWS - rows, CHUNK), F32)
    epad = jnp.zeros((GATE_ROWS - N_SPLIT * H_F, CHUNK), F32)
    sls = [slice(c * CHUNK, (c + 1) * CHUNK) for c in range(s // CHUNK)]
    g = [gT_ref[0, 0:rows, sl] for sl in sls]
    lane = lax.broadcasted_iota(jnp.int32, (rows, CHUNK), 1)
    cs = g
    sh = 1
    while sh < CHUNK:
        cs = _each(lambda x: x + jnp.where(lane >= sh, pltpu.roll(x, sh, 1), 0.0), cs)
        sh *= 2
    carry = jnp.zeros((rows, 1), F32)
    outs = []
    for gc, csc in zip(g, cs):
        outs.append(jnp.where(row < H_M, gc, jnp.where(row < 2 * H_M, csc, csc + carry)))
        carry = carry + csc[:, CHUNK - 1:CHUNK]
    fulls = _each(lambda o: jnp.concatenate([o, pad], axis=0), outs)
    fullsT = _each(lambda f: f.T, fulls)
    pieces = _each(lambda o: jnp.concatenate(_split_bf16(o[2 * H_M:N_GATES, :] * LOG2E) + [epad], axis=0), outs)
    piecesT = _each(lambda p: p.T.astype(BF16), pieces)
    for sl, f, fT, pT in zip(sls, fulls, fullsT, piecesT):
        cT_ref[0, :, sl] = f
        c_ref[sl, :] = fT
        e_ref[sl, :] = pT


def _gate_scan(gT):
    b, r, s = gT.shape
    return pl.pallas_call(
        _gate_scan_kernel,
        grid=(b,),
        in_specs=[pl.BlockSpec((1, r, s), lambda bi: (bi, 0, 0))],
        out_specs=(pl.BlockSpec((1, r, s), lambda bi: (bi, 0, 0)),
                   pl.BlockSpec((s, r), lambda bi: (bi, 0)),
                   pl.BlockSpec((s, r), lambda bi: (bi, 0))),
        out_shape=(jax.ShapeDtypeStruct((b, r, s), F32), jax.ShapeDtypeStruct((b * s, r), F32),
                   jax.ShapeDtypeStruct((b * s, r), BF16)),
        compiler_params=_params("parallel"),
        name="gate_scan",
    )(gT)


def _each(f, *lists):
    return [f(*a) for a in zip(*lists)]


def _mlstm_gates(ig_row, b_row, ig_col, b_col, m_st, valid, n_valid):
    length = b_col[0].shape[0]
    dlog = _each(lambda bc, br, ir: jnp.where(valid, bc - br + ir, NEG_INF), b_col, b_row, ig_row)
    a_col = _each(lambda bc, m: bc + m, b_col, m_st)
    mx = _each(lambda d: jnp.max(d, axis=1, keepdims=True), dlog)
    m_t = _each(jnp.maximum, a_col, mx)
    w_intra = _each(lambda d, m: jnp.exp(d - m), dlog, m_t)
    w_inter = _each(lambda a, m: jnp.exp(a - m), a_col, m_t)
    floor = _each(lambda m: jnp.exp(-m), m_t)
    b_end = _each(lambda br: br[:, n_valid - 1:n_valid], b_row)
    a_end = _each(lambda be, m: be + m, b_end, m_st)
    wlog = _each(lambda be, bc, ic: be - bc + ic, b_end, b_col, ig_col)
    if n_valid < length:
        keep = lax.broadcasted_iota(jnp.int32, (length, 1), 0) < n_valid
        wlog = _each(lambda w: jnp.where(keep, w, NEG_INF), wlog)
    wmax = _each(lambda w: jnp.max(w, axis=0, keepdims=True), wlog)
    m_new = _each(jnp.maximum, a_end, wmax)
    ws = _each(lambda w, m: jnp.exp(w - m), wlog, m_new)
    decay = _each(lambda a, m: jnp.exp(a - m), a_end, m_new)
    return w_intra, w_inter, floor, ws, decay, m_new


def _mlstm_prompt_kernel(mq_ref, mk_ref, mv_ref, smo_ref, c_ref, cT_ref, gmh_ref,
                         hm_ref, caug_ref, m_ref):
    nb = mq_ref.shape[0]

    @pl.when(pl.program_id(0) == 0)
    def _():
        caug_ref[...] = jnp.zeros_like(caug_ref)
        m_ref[...] = jnp.zeros_like(m_ref)

    row = lax.broadcasted_iota(jnp.int32, (CHUNK, CHUNK), 0)
    col = lax.broadcasted_iota(jnp.int32, (CHUNK, CHUNK), 1)
    valid = col <= row
    lane = lax.broadcasted_iota(jnp.int32, (CHUNK, DV_M), 1)
    ones_col = jnp.where(lane == 0, 1.0, 0.0).astype(BF16)
    chains = [(b, h) for b in range(nb) for h in range(H_M)]
    sl = lambda h: slice(h * DK_M, (h + 1) * DK_M)

    q = [mq_ref[b, :, sl(h)] for b, h in chains]
    k = [mk_ref[b, :, sl(h)] for b, h in chains]
    v_aug = [jnp.concatenate([mv_ref[b, :, sl(h)], ones_col], axis=1) for b, h in chains]
    s_mat = _each(_dot_nt, q, k)
    r2 = _each(lambda qq, bh: _dot(qq, caug_ref[bh[0], bh[1]].astype(BF16)), q, chains)

    w_intra, w_inter, floor, ws, decay, m_new = _mlstm_gates(
        [cT_ref[b, h:h + 1, :] for b, h in chains],
        [cT_ref[b, H_M + h:H_M + h + 1, :] for b, h in chains],
        [c_ref[b, :, h:h + 1] for b, h in chains],
        [c_ref[b, :, H_M + h:H_M + h + 1] for b, h in chains],
        [m_ref[b, h:h + 1, 0:1] for b, h in chains], valid, CHUNK)
    sw = _each(lambda s, w: (s * w).astype(BF16), s_mat, w_intra)
    kw = _each(lambda kk, w: (kk.astype(F32) * w).astype(BF16), k, ws)

    r1 = _each(_dot, sw, v_aug)
    upd = _each(_dot_tn, kw, v_aug)

    num = _each(lambda w, a, c: w * a[:, :DV_M] + c[:, :DV_M], w_inter, r2, r1)
    den = _each(lambda w, a, c: w * a[:, DV_M:DV_M + 1] + c[:, DV_M:DV_M + 1], w_inter, r2, r1)
    hh = _each(lambda n, d, f: n / jnp.maximum(jnp.abs(d), f), num, den, floor)
    scale = _each(_rms_scale, hh)
    for (b, h), x, sc, dc, u, mn in zip(chains, hh, scale, decay, upd, m_new):
        hm_ref[b, :, sl(h)] = (x * sc * gmh_ref[:, sl(h)] * smo_ref[b, :, sl(h)].astype(F32)).astype(BF16)
        caug_ref[b, h] = dc * caug_ref[b, h] + u
        m_ref[b, h:h + 1, :] = jnp.broadcast_to(mn, (1, LANES))


def _mlstm_prompt(mq, mk, mv, smo, c_tok, cT, gmh):
    b, s, _ = mq.shape
    tok = lambda w: pl.BlockSpec((b, CHUNK, w), lambda c: (0, c, 0))
    whole = lambda shape: pl.BlockSpec(shape, lambda c: (0,) * len(shape))
    state = (b, H_M, DK_M, 2 * DV_M)
    return pl.pallas_call(
        _mlstm_prompt_kernel,
        grid=(s // CHUNK,),
        in_specs=[tok(W_M), tok(W_M), tok(W_M), tok(W_M), tok(GATE_ROWS),
                  pl.BlockSpec((b, GATE_ROWS, CHUNK), lambda c: (0, 0, c)),
                  whole(gmh.shape)],
        out_specs=(tok(W_M), whole(state), whole((b, 8, LANES))),
        out_shape=(jax.ShapeDtypeStruct((b, s, W_M), BF16),
                   jax.ShapeDtypeStruct(state, F32),
                   jax.ShapeDtypeStruct((b, 8, LANES), F32)),
        compiler_params=_params("arbitrary"),
        name="mlstm_prompt",
    )(mq, mk, mv, smo, c_tok, cT, gmh)


HEAD_GROUP = 2
QT = 256


def _fox_prompt_kernel(fqT_ref, fk_ref, e_ref, vT_ref, o_ref, qa_sc, m_sc, acc_sc, *, tq, tk):
    qi = pl.program_id(1)
    ratio = tq // tk
    nqt = tq // QT
    rowq = lax.broadcasted_iota(jnp.int32, (2 * DH_F, QT), 0)
    srow = lax.broadcasted_iota(jnp.int32, (tk, QT), 0)
    tcol = lax.broadcasted_iota(jnp.int32, (tk, QT), 1)
    hsl = lambda h: slice(h * DH_F, (h + 1) * DH_F)
    psl = lambda h: slice((h // 2) * 2 * DH_F, (h // 2 + 1) * 2 * DH_F)
    qsl = lambda n: slice(n * QT, (n + 1) * QT)

    for h in range(H_F):
        lo = (h % 2) * DH_F
        sel = jnp.where((rowq % H_F == h) & (rowq < N_SPLIT * H_F), -1.0, 0.0).astype(BF16)
        for n in range(nqt):
            q2 = fqT_ref[0, psl(h), qsl(n)]
            qh = jnp.where((rowq >= lo) & (rowq < lo + DH_F), q2, 0)
            qa_sc[h, n] = jnp.concatenate([qh, sel], axis=0)
    m_sc[...] = jnp.full_like(m_sc, NEG_INF)
    acc_sc[...] = jnp.zeros_like(acc_sc)
    groups = [range(g0, g0 + HEAD_GROUP) for g0 in range(0, H_F, HEAD_GROUP)]

    def run(tiles):
        work = [(ti, heads) for ti in range(len(tiles)) for heads in groups]
        k0 = [pl.multiple_of(j * tk, tk) for j, _ in tiles]
        e = [e_ref[pl.ds(k, tk), :] for k in k0]
        halves = [range(nqt) if d is None else range(d // QT, nqt) for _, d in tiles]
        st = {}

        def scores(ti, heads):
            for h in heads:
                ka = jnp.concatenate([fk_ref[pl.ds(k0[ti], tk), psl(h)], e[ti]], axis=1)
                for n in halves[ti]:
                    st[ti, h, n] = _dot(ka, qa_sc[h, n])

        scores(*work[0])
        for wi, (ti, heads) in enumerate(work):
            if wi + 1 < len(work):
                scores(*work[wi + 1])
            units = [(h, n) for h in heads for n in halves[ti]]
            p, alpha = {}, {}
            for h, n in units:
                s_u = st.pop((ti, h, n))
                d = tiles[ti][1]
                if d is not None and d + tk > n * QT:
                    s_u = jnp.where(srow + d <= tcol + n * QT, s_u, NEG_INF)
                m_old = m_sc[h, n]
                m_new = jnp.maximum(m_old, jnp.max(s_u, axis=0, keepdims=True))
                alpha[h, n] = jnp.exp2(m_old - m_new)
                p[h, n] = jnp.exp2(s_u - m_new).astype(BF16)
                m_sc[h, n] = m_new
            for h, n in units:
                vT = vT_ref[0, h * V_ROWS:(h + 1) * V_ROWS, pl.ds(k0[ti], tk)]
                acc_sc[h, n] = alpha[h, n] * acc_sc[h, n] + _dot(vT, p[h, n])

    def body(jj, carry):
        run([(jj * ratio + r, None) for r in range(ratio)])
        return carry

    lax.fori_loop(0, qi, body, 0)
    run([(qi * ratio + r, r * tk) for r in range(ratio)])
    for h in range(H_F):
        for n in range(nqt):
            o_ref[0, hsl(h), qsl(n)] = (acc_sc[h, n, :DH_F, :] / acc_sc[h, n, DH_F:DH_F + 1, :]).astype(BF16)


def _fox_prompt(fqT, fk_tok, e_tok, fvT16, tq, tk):
    b, w, s = fqT.shape
    kern = functools.partial(_fox_prompt_kernel, tq=tq, tk=tk)
    return pl.pallas_call(
        kern,
        grid=(b, s // tq),
        in_specs=[pl.BlockSpec((1, w, tq), lambda bi, qi: (bi, 0, qi)),
                  pl.BlockSpec((s, w), lambda bi, qi: (bi, 0)),
                  pl.BlockSpec((s, GATE_ROWS), lambda bi, qi: (bi, 0)),
                  pl.BlockSpec((1, H_F * V_ROWS, s), lambda bi, qi: (bi, 0, 0))],
        out_specs=pl.BlockSpec((1, w, tq), lambda bi, qi: (bi, 0, qi)),
        out_shape=jax.ShapeDtypeStruct((b, w, s), BF16),
        scratch_shapes=[pltpu.VMEM((H_F, tq // QT, 2 * DH_F + GATE_ROWS, QT), BF16),
                        pltpu.VMEM((H_F, tq // QT, 1, QT), F32),
                        pltpu.VMEM((H_F, tq // QT, V_ROWS, QT), F32)],
        compiler_params=_params("parallel", "arbitrary"),
        name="fox_prompt",
    )(fqT, fk_tok, e_tok, fvT16)


FF_TILE = 256


def _merge_ffn_kernel(x_ref, hm_ref, yfT_ref, sga_ref, sgb_ref, p_ref,
                      wpa_ref, wpb_ref, wout_ref, gffn_ref, wgu_ref, wdown_ref,
                      gple_ref, wpg_ref, wpp_ref, y_ref, hdn_sc):
    a = _dot(hm_ref[...], wpa_ref[...])
    bb = _dot_tn(yfT_ref[0], wpb_ref[...])
    u = sga_ref[...].astype(F32) * a + sgb_ref[...].astype(F32) * bb
    x1 = x_ref[...] + _dot(u.astype(BF16), wout_ref[...])
    xn = (x1 * _rms_scale(x1) * gffn_ref[...]).astype(BF16)
    for j in range(0, D_FF, FF_TILE):
        g = _dot(xn, wgu_ref[:, j:j + FF_TILE])
        up = _dot(xn, wgu_ref[:, D_FF + j:D_FF + j + FF_TILE])
        hdn_sc[:, j:j + FF_TILE] = (jax.nn.silu(g) * up).astype(BF16)
    x2 = x1 + _dot(hdn_sc[...], wdown_ref[...])
    xg = (x2 * _rms_scale(x2) * gple_ref[...]).astype(BF16)
    gate = jax.nn.sigmoid(_dot(xg, wpg_ref[...]))
    y_ref[...] = x2 + gate * _dot(p_ref[...].astype(BF16), wpp_ref[...])


def _merge_ffn(x, hm, yfT, sga, sgb, p, wpa, wpb, wout, gffn, wgu, wdown, gple, wpg, wpp, tm):
    t, d = x.shape
    b, _, s = yfT.shape
    nt = s // tm
    tok = lambda w: pl.BlockSpec((tm, w), lambda i: (i, 0))
    full = lambda a: pl.BlockSpec(a.shape, lambda i: (0,) * a.ndim)
    return pl.pallas_call(
        _merge_ffn_kernel,
        grid=(t // tm,),
        in_specs=[tok(d), tok(W_M),
                  pl.BlockSpec((1, W_F, tm), lambda i: (i // nt, 0, i % nt)),
                  tok(d), tok(d), tok(PLE_DIM),
                  full(wpa), full(wpb), full(wout), full(gffn), full(wgu), full(wdown),
                  full(gple), full(wpg), full(wpp)],
        out_specs=tok(d),
        out_shape=jax.ShapeDtypeStruct((t, d), F32),
        scratch_shapes=[pltpu.VMEM((tm, D_FF), BF16)],
        compiler_params=_params("parallel"),
        name="merge_ffn",
    )(x, hm, yfT, sga, sgb, p, wpa, wpb, wout, gffn, wgu, wdown, gple, wpg, wpp)


_SPLIT = (W_M, W_M, W_M, W_M, H_M, H_M, W_F, W_F, W_F, H_F, D_MODEL, D_MODEL)


def _prep_weights(g_mix, w_in, b_mi, b_mf, b_ff, g_mh, g_qn, g_kn, w_pa, w_pb, w_out, g_ffn, w_gu,
                  w_down, g_ple, w_ple_gate, w_ple_proj):
    offs = [0]
    for sz in _SPLIT:
        offs.append(offs[-1] + sz)
    o_mq, _, _, _, o_mi, o_mf, o_fq, _, _, o_ff, o_ga, _, o_end = offs
    w = w_in[0]
    wT = w.T
    wn = jnp.concatenate([w[:, o_mq:o_mi], w[:, o_ga:o_end]], axis=1).astype(BF16)
    wt = jnp.concatenate([wT[o_fq:o_ff], wT[o_mi:o_fq], wT[o_ff:o_ga],
                          jnp.zeros((GATE_ROWS - N_GATES, D_MODEL), F32)], axis=0).astype(BF16)
    bias_col = jnp.concatenate([b_mi[0], b_mf[0], b_ff[0], jnp.zeros((GATE_ROWS - N_GATES,), F32)])[:, None]
    gain_col = jnp.concatenate([jnp.tile(g_qn[0], H_F) * (F_SCALE * LOG2E), jnp.tile(g_kn[0], H_F)])[:, None]
    return dict(
        g_mix=g_mix, wn=wn, wt=wt, bias_col=bias_col, gain_col=gain_col,
        gmh=g_mh[0].reshape(1, W_M),
        wpa=w_pa[0].astype(BF16), wpb=w_pb[0].astype(BF16), wout=w_out[0].astype(BF16),
        gffn=g_ffn, wgu=w_gu[0].astype(BF16), wdown=w_down[0].astype(BF16),
        gple=g_ple, wpg=w_ple_gate[0].astype(BF16), wpp=w_ple_proj[0].astype(BF16))


def _prompt_path(x, p, wd, tm_in, tm_ffn, tq, tk):
    b, s, d = x.shape
    (mq, mk, mv, smo, sga, sgb, fqT, fkT, fvT, fk_tok, fvT16, gT) = _in_proj(
        x, wd["g_mix"], wd["wn"], wd["wt"], wd["bias_col"], wd["gain_col"], tm_in)
    cT, c_tok, e_tok = _gate_scan(gT)
    seq3 = lambda a: a.reshape(b, s, a.shape[-1])
    hm, caug, m_out = _mlstm_prompt(seq3(mq), seq3(mk), seq3(mv), seq3(smo), seq3(c_tok), cT, wd["gmh"])
    yfT = _fox_prompt(fqT, fk_tok, e_tok, fvT16, tq, tk)
    y = _merge_ffn(x.reshape(b * s, d), hm.reshape(b * s, W_M), yfT, sga, sgb, p.reshape(b * s, PLE_DIM),
                   wd["wpa"], wd["wpb"], wd["wout"], wd["gffn"], wd["wgu"], wd["wdown"],
                   wd["gple"], wd["wpg"], wd["wpp"], tm_ffn)
    to_cache = lambda aT: aT.reshape(b, H_F, DH_F, s).transpose(0, 3, 1, 2)[None]
    logf = gT[:, 2 * H_M:N_GATES, :].transpose(0, 2, 1)[None]
    return (y.reshape(b, s, d), to_cache(fkT), to_cache(fvT), logf,
            caug[None, :, :, :, :DV_M], caug[None, :, :, :, DV_M], m_out[None, :, :H_M, 0])


L_PAD = 16
SEQ_BLOCK = 8


def _mlstm_sample_kernel(q_ref, k_ref, v_ref, smo_ref, grow_ref, gcol_ref, c_ref, n_ref, m_ref, gmh_ref,
                         hm_ref, co_ref, no_ref, mo_ref, *, n_valid):
    length = L_PAD
    row = lax.broadcasted_iota(jnp.int32, (length, length), 0)
    col = lax.broadcasted_iota(jnp.int32, (length, length), 1)
    valid = (col <= row) & (col < n_valid)
    chains = [(i, h) for i in range(SEQ_BLOCK) for h in range(H_M)]
    sl = lambda h: slice(h * DK_M, (h + 1) * DK_M)

    q = [q_ref[i, :, sl(h)] for i, h in chains]
    k = [k_ref[i, :, sl(h)] for i, h in chains]
    v = [v_ref[i, :, sl(h)] for i, h in chains]
    n_st = [n_ref[i, h:h + 1, :] for i, h in chains]
    s_mat = _each(_dot_nt, q, k)
    r2 = _each(lambda qq, ih: _dot(qq, c_ref[ih[0], ih[1]].astype(BF16)), q, chains)
    qn = _each(lambda qq, n: jnp.sum(qq.astype(F32) * n, axis=1, keepdims=True), q, n_st)

    lf_row = [grow_ref[i, H_M + h:H_M + h + 1, 0:length] for i, h in chains]
    lf_col = [gcol_ref[i, :, H_M + h:H_M + h + 1] for i, h in chains]
    b_col = _each(lambda r: jnp.sum(jnp.where(col <= row, r, 0.0), axis=1, keepdims=True), lf_row)
    b_row = _each(lambda c: jnp.sum(jnp.where(row <= col, c, 0.0), axis=0, keepdims=True), lf_col)
    w_intra, w_inter, floor, ws, decay, m_new = _mlstm_gates(
        [grow_ref[i, h:h + 1, 0:length] for i, h in chains], b_row,
        [gcol_ref[i, :, h:h + 1] for i, h in chains], b_col,
        [m_ref[i:i + 1, h:h + 1] for i, h in chains], valid, n_valid)
    sw = _each(lambda s, w: s * w, s_mat, w_intra)
    kw = _each(lambda kk, w: kk.astype(F32) * w, k, ws)

    r1 = _each(lambda s, vv: _dot(s.astype(BF16), vv), sw, v)
    upd = _each(lambda kk, vv: _dot_tn(kk.astype(BF16), vv), kw, v)
    sw_sum = _each(lambda s: jnp.sum(s, axis=1, keepdims=True), sw)
    kw_sum = _each(lambda kk: jnp.sum(kk, axis=0, keepdims=True), kw)

    num = _each(lambda w, a, c: w * a + c, w_inter, r2, r1)
    den = _each(lambda w, a, c: w * a + c, w_inter, qn, sw_sum)
    hh = _each(lambda n, d, f: n / jnp.maximum(jnp.abs(d), f), num, den, floor)
    scale = _each(_rms_scale, hh)
    for (i, h), x, sc, dc, u, n, ks, mn in zip(chains, hh, scale, decay, upd, n_st, kw_sum, m_new):
        hm_ref[i, :, sl(h)] = (x * sc * gmh_ref[:, sl(h)] * smo_ref[i, :, sl(h)].astype(F32)).astype(BF16)
        co_ref[i, h] = dc * c_ref[i, h] + u
        no_ref[i, h:h + 1, :] = dc * n + ks
        mo_ref[i:i + 1, h:h + 1] = mn


def _mlstm_sample(q, k, v, smo, grow, gcol, c_st, n_st, m_st, gmh, n_valid):
    db = q.shape[0]
    blk = lambda a: pl.BlockSpec((SEQ_BLOCK,) + a.shape[1:], lambda i: (i,) + (0,) * (a.ndim - 1))
    kern = functools.partial(_mlstm_sample_kernel, n_valid=n_valid)
    return pl.pallas_call(
        kern,
        grid=(db // SEQ_BLOCK,),
        in_specs=[blk(q), blk(k), blk(v), blk(smo), blk(grow), blk(gcol), blk(c_st), blk(n_st), blk(m_st),
                  pl.BlockSpec(gmh.shape, lambda i: (0, 0))],
        out_specs=(blk(q), blk(c_st), blk(n_st), blk(m_st)),
        out_shape=(jax.ShapeDtypeStruct(q.shape, BF16), jax.ShapeDtypeStruct(c_st.shape, F32),
                   jax.ShapeDtypeStruct(n_st.shape, F32), jax.ShapeDtypeStruct(m_st.shape, F32)),
        compiler_params=_params("parallel"),
        name="mlstm_sample",
    )(q, k, v, smo, grow, gcol, c_st, n_st, m_st, gmh)


def _fox_sample_kernel(pt_ref, fq_ref, kn_ref, vn_ref, lfn_ref, ck_hbm, cv_hbm, cl_hbm, o_ref,
                       kbuf, vbuf, lbuf, sem, *, n_pages, n_tok):
    seq = pl.program_id(0)
    slot = lax.rem(seq, 2)

    def page_copies(s, sl):
        copies = []
        for j in range(n_pages):
            pid = pt_ref[s, j]
            copies.append(pltpu.make_async_copy(ck_hbm.at[pid], kbuf.at[sl, j], sem.at[sl]))
            copies.append(pltpu.make_async_copy(cv_hbm.at[pid], vbuf.at[sl, j], sem.at[sl]))
            copies.append(pltpu.make_async_copy(cl_hbm.at[pid], lbuf.at[sl, j], sem.at[sl]))
        return copies

    @pl.when(seq == 0)
    def _():
        for c in page_copies(0, 0):
            c.start()

    @pl.when(seq + 1 < pl.num_programs(0))
    def _():
        for c in page_copies(seq + 1, 1 - slot):
            c.start()

    for c in page_copies(seq, slot):
        c.wait()

    k_page = lambda j: kbuf[slot, j]
    v_page = lambda j: vbuf[slot, j]
    rows = n_tok * H_F
    sub = lax.broadcasted_iota(jnp.int32, (H_F, W_F), 0)
    head_of_lane = lax.broadcasted_iota(jnp.int32, (H_F, W_F), 1) // DH_F
    own = sub == head_of_lane
    tile = lambda a: jnp.concatenate([a] * n_tok, axis=0)

    fq = fq_ref[0]
    qbd = jnp.concatenate(
        [jnp.where(own, jnp.broadcast_to(fq[t:t + 1, :], (H_F, W_F)), 0.0) for t in range(n_tok)], axis=0)
    qbd16 = qbd.astype(BF16)

    incl_all = _lane_cumsum(lbuf[slot].reshape(n_pages * H_F, LANES))
    bias = [None] * n_pages
    carry = jnp.zeros((H_F, 1), F32)
    for pg in reversed(range(n_pages)):
        incl = incl_all[pg * H_F:(pg + 1) * H_F, :]
        tot = incl[:, LANES - 1:LANES]
        bias[pg] = (tot - incl + carry) * LOG2E
        carry = carry + tot

    logits = []
    for j in range(0, n_pages, 2):
        kcat = jnp.concatenate([k_page(j).astype(BF16), k_page(j + 1).astype(BF16)], axis=1)
        bcat = jnp.concatenate([tile(bias[j]), tile(bias[j + 1])], axis=1)
        logits.append(_dot(qbd16, kcat) + bcat)

    rowi = lax.broadcasted_iota(jnp.int32, (rows, 1), 0)
    qf = qbd16.astype(F32)
    lfn = lfn_ref[0]
    new_logits = []
    f_new = jnp.zeros((H_F, 1), F32)
    for s in range(n_tok):
        f_new = f_new + lfn[:, s:s + 1]
        kn = kn_ref[0, s:s + 1, :].astype(BF16).astype(F32)
        ln = jnp.sum(qf * kn, axis=1, keepdims=True) - tile(f_new) * LOG2E
        new_logits.append(jnp.where(rowi >= s * H_F, ln, NEG_INF))

    m = new_logits[0]
    for ln in new_logits[1:]:
        m = jnp.maximum(m, ln)
    for lg in logits:
        m = jnp.maximum(m, jnp.max(lg, axis=1, keepdims=True))

    denom = jnp.zeros((rows, 1), F32)
    acc = jnp.zeros((rows, W_F), F32)
    for idx, lg in enumerate(logits):
        p = jnp.exp2(lg - m)
        denom = denom + jnp.sum(p, axis=1, keepdims=True)
        j = 2 * idx
        vcat = jnp.concatenate([v_page(j).astype(BF16), v_page(j + 1).astype(BF16)], axis=1)
        acc = acc + _dot_nt(p.astype(BF16), vcat)
    for s, ln in enumerate(new_logits):
        p = jnp.exp2(ln - m)
        denom = denom + p
        acc = acc + p.astype(BF16).astype(F32) * vn_ref[0, s:s + 1, :]
    out = acc / denom
    for t in range(n_tok):
        blk = jnp.where(own, out[t * H_F:(t + 1) * H_F, :], 0.0)
        o_ref[0, t:t + 1, :] = jnp.sum(blk, axis=0, keepdims=True)


def _fox_sample(page_table, fq, kn, vn, lfn, ck, cv, cl):
    db, n_tok, _ = fq.shape
    n_pages = page_table.shape[1]
    page = ck.shape[-1]
    seq = lambda a: pl.BlockSpec((1,) + a.shape[1:], lambda b, pt: (b,) + (0,) * (a.ndim - 1))

    hbm = pl.BlockSpec(memory_space=pl.ANY)
    in_specs = [seq(fq), seq(kn), seq(vn), seq(lfn), hbm, hbm, hbm]
    kern = functools.partial(_fox_sample_kernel, n_pages=n_pages, n_tok=n_tok)
    return pl.pallas_call(
        kern,
        grid_spec=pltpu.PrefetchScalarGridSpec(
            num_scalar_prefetch=1, grid=(db,), in_specs=in_specs, out_specs=seq(fq),
            scratch_shapes=[pltpu.VMEM((2, n_pages, W_F, page), F32),
                            pltpu.VMEM((2, n_pages, W_F, page), F32),
                            pltpu.VMEM((2, n_pages, H_F, page), F32),
                            pltpu.SemaphoreType.DMA((2,))]),
        out_shape=jax.ShapeDtypeStruct(fq.shape, F32),
        compiler_params=_params("arbitrary"),
        name="fox_sample",
    )(page_table, fq, kn, vn, lfn, ck, cv, cl)


def _sample_path(x, p, cache_k, cache_v, cache_logf, state_c, state_n, state_m, page_table, wd):
    db, n_tok, d = x.shape
    t = db * n_tok
    (mq, mk, mv, smo, sga, sgb, fqT, fkT, fvT, fk_tok, _, gT) = _in_proj(
        x.reshape(1, t, d), wd["g_mix"], wd["wn"], wd["wt"], wd["bias_col"], wd["gain_col"], t)

    pad_tok = lambda a: jnp.pad(a.reshape(db, n_tok, -1), ((0, 0), (0, L_PAD - n_tok), (0, 0)))
    g16 = gT[0, :2 * 8].reshape(2 * 8, db, n_tok)
    grow = jnp.pad(g16.transpose(1, 0, 2), ((0, 0), (0, 0), (0, LANES - n_tok)))
    gcol = jnp.pad(g16.transpose(1, 2, 0), ((0, 0), (0, L_PAD - n_tok), (0, LANES - 2 * 8)))
    hm, c_new, n_new, m_new = _mlstm_sample(
        pad_tok(mq), pad_tok(mk), pad_tok(mv), pad_tok(smo), grow, gcol,
        state_c, state_n, state_m, wd["gmh"], n_tok)
    hm = hm[:, :n_tok].reshape(t, W_M)

    pool, page = cache_k.shape[0], cache_k.shape[1]
    ck = cache_k.transpose(0, 2, 3, 1).reshape(pool, W_F, page)
    cv = cache_v.transpose(0, 2, 3, 1).reshape(pool, W_F, page)
    cl = cache_logf.transpose(0, 2, 1)
    tok3 = lambda aT: aT[0].T.astype(F32).reshape(db, n_tok, W_F)
    lfn = jnp.pad(gT[0, 2 * H_M:N_GATES].reshape(H_F, db, n_tok).transpose(1, 0, 2),
                  ((0, 0), (0, 0), (0, LANES - n_tok)))
    yf = _fox_sample(page_table, tok3(fqT), fk_tok.astype(F32).reshape(db, n_tok, W_F), tok3(fvT),
                     lfn, ck, cv, cl)
    yfT = yf.reshape(t, W_F).T.astype(BF16)[None]

    y = _merge_ffn(x.reshape(t, d), hm, yfT, sga, sgb, p.reshape(t, PLE_DIM),
                   wd["wpa"], wd["wpb"], wd["wout"], wd["gffn"], wd["wgu"], wd["wdown"],
                   wd["gple"], wd["wpg"], wd["wpp"], t)
    to_cache = lambda aT: aT[0].reshape(H_F, DH_F, db, n_tok).transpose(2, 3, 0, 1)[None]
    logf = gT[0, 2 * H_M:N_GATES].reshape(H_F, db, n_tok).transpose(1, 2, 0)[None]
    return (y.reshape(db, n_tok, d), to_cache(fkT), to_cache(fvT), logf,
            c_new[None], n_new[None], m_new[None])


def kernel(x_prompt, x_sample, p_prompt, p_sample, cache_k, cache_v, cache_logf, state_C, state_n, state_m,
           page_table, g_mix, w_in, b_mi, b_mf, b_ff, g_mh, g_qn, g_kn, w_pa, w_pb, w_out, g_ffn, w_gu,
           w_down, g_ple, w_ple_gate, w_ple_proj):
    wd = _prep_weights(g_mix, w_in, b_mi, b_mf, b_ff, g_mh, g_qn, g_kn, w_pa, w_pb, w_out, g_ffn, w_gu,
                       w_down, g_ple, w_ple_gate, w_ple_proj)
    yp, kp, vp, lfp, cp, np_, mp = _prompt_path(x_prompt, p_prompt[0], wd, 512, 512, 512, 256)
    ys, ks, vs, lfs, cs, ns, ms = _sample_path(
        x_sample, p_sample[0], cache_k[0], cache_v[0], cache_logf[0],
        state_C[0], state_n[0], state_m[0], page_table, wd)
    return (yp, ys, kp, vp, lfp, cp, np_, mp, ks, vs, lfs, cs, ns, ms)
```

```python
import functools

import jax
import jax.numpy as jnp
from jax import lax
from jax.experimental import pallas as pl
from jax.experimental.pallas import tpu as pltpu

D_MODEL = 1024
H_M = 4
DK_M = 128
DV_M = 128
W_M = H_M * DV_M
H_F = 8
DH_F = 64
W_F = H_F * DH_F
D_FF = 2816
PLE_DIM = 256
CHUNK = 128
EPS = 1e-6
F_SCALE = DH_F ** -0.5
K_SCALE_M = DK_M ** -0.5
LOG2E = 1.4426950408889634
N_GATES = 2 * H_M + H_F
GATE_ROWS = 128
N_SPLIT = 3
V_ROWS = DH_F + 16
LANES = 128
VMEM_LIMIT = 56 * 1024 * 1024

BF16 = jnp.bfloat16
F32 = jnp.float32
NEG_INF = float("-inf")


def _dot(a, b):
    return jnp.dot(a, b, preferred_element_type=F32)


def _dot_nt(a, b):
    return lax.dot_general(a, b, (((1,), (1,)), ((), ())), preferred_element_type=F32)


def _dot_tn(a, b):
    return lax.dot_general(a, b, (((0,), (0,)), ((), ())), preferred_element_type=F32)


def _log_sigmoid(u):
    return jnp.minimum(u, 0.0) - jnp.log1p(jnp.exp(-jnp.abs(u)))


def _rms_scale(x):
    return lax.rsqrt(jnp.mean(x * x, axis=-1, keepdims=True) + EPS)


def _params(*sem):
    return pltpu.CompilerParams(dimension_semantics=sem, vmem_limit_bytes=VMEM_LIMIT)


def _in_proj_kernel(*refs, fox):
    refs = list(refs)
    pt_ref = refs.pop(0) if fox else None
    x_ref, gmix_ref, wn_ref, wt_ref, bias_ref, gain_ref = refs[:6]
    del refs[:6]
    if fox:
        fq_ref, kn_ref, vn_ref, lfn_ref, ck_hbm, cv_hbm, cl_hbm = refs[:7]
        del refs[:7]
    (mq_ref, mk_ref, mv_ref, smo_ref, sga_ref, sgb_ref,
     fqT_ref, fkT_ref, fvT_ref, fk_ref, fvT16_ref, gT_ref) = refs[:12]
    del refs[:12]

    x = x_ref[0]
    xn = (x * _rms_scale(x) * gmix_ref[...]).astype(BF16)

    def zn(lo, hi):
        return _dot(xn, wn_ref[:, lo:hi])

    def zt(lo, hi):
        return _dot_nt(wt_ref[lo:hi, :], xn)

    def head_norm(z, gain):
        tm = z.shape[-1]
        r = z.reshape(H_F, DH_F, tm)
        r = r * lax.rsqrt(jnp.mean(r * r, axis=1, keepdims=True) + EPS)
        return r.reshape(W_F, tm) * gain

    def g_mq():
        mq_ref[...] = zn(0, W_M).astype(BF16)

    def g_mk():
        mk_ref[...] = (zn(W_M, 2 * W_M) * K_SCALE_M).astype(BF16)

    def g_mv():
        mv_ref[...] = zn(2 * W_M, 3 * W_M).astype(BF16)

    def g_smo():
        smo_ref[...] = jax.nn.sigmoid(zn(3 * W_M, 4 * W_M)).astype(BF16)

    def g_gate(ref, off, c):
        def run():
            ref[:, c:c + W_M] = jax.nn.sigmoid(zn(off + c, off + c + W_M)).astype(BF16)
        return run

    def g_fq():
        fqT_ref[0] = head_norm(zt(0, W_F), gain_ref[0:W_F, :]).astype(BF16)

    def g_fk():
        fk = head_norm(zt(W_F, 2 * W_F), gain_ref[W_F:2 * W_F, :])
        fkT_ref[0] = fk
        fk_ref[...] = fk.T.astype(BF16)

    def g_fv():
        fv = zt(2 * W_F, 3 * W_F)
        fvT_ref[0] = fv
        tm = fv.shape[-1]
        fv_aug = jnp.concatenate([fv.reshape(H_F, DH_F, tm), jnp.ones((H_F, V_ROWS - DH_F, tm), F32)], axis=1)
        fvT16_ref[0] = fv_aug.reshape(H_F * V_ROWS, tm).astype(BF16)

    def g_gates():
        u = zt(3 * W_F, 3 * W_F + GATE_ROWS) + bias_ref[...]
        row = lax.broadcasted_iota(jnp.int32, u.shape, 0)
        gT_ref[0] = jnp.where(row < H_M, u, jnp.where(row < N_GATES, _log_sigmoid(u), 0.0))

    groups = [g_mq, g_mk, g_mv, g_smo]
    groups += [g_gate(sga_ref, 4 * W_M, c) for c in range(0, D_MODEL, W_M)]
    groups += [g_gate(sgb_ref, 4 * W_M + D_MODEL, c) for c in range(0, D_MODEL, W_M)]
    groups += [g_fq, g_fk, g_fv, g_gates]
    if not fox:
        for g in groups:
            g()
        return

    per, total, n_pages, n_tok = fox
    yf_ref, kbuf, vbuf, lbuf, sem = refs
    step = pl.program_id(0) * pl.num_programs(1) + pl.program_id(1)
    first = step * per

    def page_copies(n, half):
        copies = []
        for j in range(n_pages):
            pid = pt_ref[n, j]
            copies.append(pltpu.make_async_copy(ck_hbm.at[pid], kbuf.at[half, j], sem.at[half]))
            copies.append(pltpu.make_async_copy(cv_hbm.at[pid], vbuf.at[half, j], sem.at[half]))
            copies.append(pltpu.make_async_copy(cl_hbm.at[pid], lbuf.at[half, j], sem.at[half]))
        return copies

    @pl.when(step == 0)
    def _():
        for c in page_copies(0, 0):
            c.start()

    todo = iter(groups)
    share = len(groups) // per
    for j in range(per):
        n, half = first + j, j % 2

        @pl.when(n + 1 < total)
        def _():
            for c in page_copies(n + 1, 1 - half):
                c.start()

        for c in page_copies(n, half):
            c.wait()
        scores = _fox_sample_scores(fq_ref[j], kn_ref[j], lfn_ref[j], lambda pg: kbuf[half, pg],
                                    lbuf[half].reshape(n_pages * H_F, LANES), n_pages, n_tok)
        for _ in range(share - 1):
            next(todo)()
        yf_ref[j] = _fox_sample_output(*scores, lambda pg: vbuf[half, pg], vn_ref[j], n_pages, n_tok)
        next(todo)()
    for g in todo:
        g()


def _in_proj(x, g_mix, wn, wt, bias_col, gain_col, tm, fox=None):
    b, s, d = x.shape
    nt = s // tm
    t = b * s
    tok = lambda w: pl.BlockSpec((tm, w), lambda bi, i, *_: (bi * nt + i, 0))
    hm = lambda r: pl.BlockSpec((1, r, tm), lambda bi, i, *_: (bi, 0, i))
    full = lambda a: pl.BlockSpec(a.shape, lambda bi, i, *_: (0,) * a.ndim)
    out_shape = (
        jax.ShapeDtypeStruct((t, W_M), BF16),
        jax.ShapeDtypeStruct((t, W_M), BF16),
        jax.ShapeDtypeStruct((t, W_M), BF16),
        jax.ShapeDtypeStruct((t, W_M), BF16),
        jax.ShapeDtypeStruct((t, D_MODEL), BF16),
        jax.ShapeDtypeStruct((t, D_MODEL), BF16),
        jax.ShapeDtypeStruct((b, W_F, s), BF16),
        jax.ShapeDtypeStruct((b, W_F, s), F32),
        jax.ShapeDtypeStruct((b, W_F, s), F32),
        jax.ShapeDtypeStruct((t, W_F), BF16),
        jax.ShapeDtypeStruct((b, H_F * V_ROWS, s), BF16),
        jax.ShapeDtypeStruct((b, GATE_ROWS, s), F32),
    )
    out_specs = (tok(W_M), tok(W_M), tok(W_M), tok(W_M), tok(D_MODEL), tok(D_MODEL),
                 hm(W_F), hm(W_F), hm(W_F), tok(W_F), hm(H_F * V_ROWS), hm(GATE_ROWS))
    in_specs = [pl.BlockSpec((1, tm, d), lambda bi, i, *_: (bi, i, 0)),
                full(g_mix), full(wn), full(wt), full(bias_col), full(gain_col)]
    if fox is None:
        return pl.pallas_call(
            functools.partial(_in_proj_kernel, fox=None),
            grid=(b, nt),
            in_specs=in_specs,
            out_specs=out_specs,
            out_shape=out_shape,
            compiler_params=_params("parallel", "parallel"),
            name="in_proj",
        )(x, g_mix, wn, wt, bias_col, gain_col)

    page_table, fq, kn, vn, lfn, ck, cv, cl = fox
    db, n_tok, _ = fq.shape
    n_pages, page = page_table.shape[1], ck.shape[-1]
    per = db // (b * nt)
    assert per * b * nt == db and per % 2 == 0 and len(out_specs) % per == 0
    seqs = lambda a: pl.BlockSpec((per,) + a.shape[1:], lambda bi, i, *_: (bi * nt + i,) + (0,) * (a.ndim - 1))
    hbm = pl.BlockSpec(memory_space=pl.ANY)
    return pl.pallas_call(
        functools.partial(_in_proj_kernel, fox=(per, db, n_pages, n_tok)),
        grid_spec=pltpu.PrefetchScalarGridSpec(
            num_scalar_prefetch=1, grid=(b, nt),
            in_specs=in_specs + [seqs(fq), seqs(kn), seqs(vn), seqs(lfn), hbm, hbm, hbm],
            out_specs=out_specs + (seqs(fq),),
            scratch_shapes=[pltpu.VMEM((2, n_pages, W_F, page), F32),
                            pltpu.VMEM((2, n_pages, W_F, page), F32),
                            pltpu.VMEM((2, n_pages, H_F, page), F32),
                            pltpu.SemaphoreType.DMA((2,))]),
        out_shape=out_shape + (jax.ShapeDtypeStruct(fq.shape, F32),),
        compiler_params=_params("arbitrary", "arbitrary"),
        name="in_proj_fox_sample",
    )(page_table, x, g_mix, wn, wt, bias_col, gain_col, fq, kn, vn, lfn, ck, cv, cl)


def _lane_cumsum(x):
    lane = lax.broadcasted_iota(jnp.int32, x.shape, x.ndim - 1)
    sh = 1
    while sh < x.shape[-1]:
        x = x + jnp.where(lane >= sh, pltpu.roll(x, sh, x.ndim - 1), 0.0)
        sh *= 2
    return x


def _split_bf16(x):
    pieces = []
    for _ in range(N_SPLIT):
        p = x.astype(BF16).astype(F32)
        pieces.append(p)
        x = x - p
    return pieces


def _gate_scan_kernel(gT_ref, cT_ref, c_ref, e_ref):
    s = gT_ref.shape[-1]
    rows = 2 * 8
    row = lax.broadcasted_iota(jnp.int32, (rows, CHUNK), 0)
    pad = jnp.zeros((GATE_ROWS - rows, CHUNK), F32)
    epad = jnp.zeros((GATE_ROWS - N_SPLIT * H_F, CHUNK), F32)
    sls = [slice(c * CHUNK, (c + 1) * CHUNK) for c in range(s // CHUNK)]
    g = [gT_ref[0, 0:rows, sl] for sl in sls]
    lane = lax.broadcasted_iota(jnp.int32, (rows, CHUNK), 1)
    cs = g
    sh = 1
    while sh < CHUNK:
        cs = _each(lambda x: x + jnp.where(lane >= sh, pltpu.roll(x, sh, 1), 0.0), cs)
        sh *= 2
    carry = jnp.zeros((rows, 1), F32)
    outs = []
    for gc, csc in zip(g, cs):
        outs.append(jnp.where(row < H_M, gc, jnp.where(row < 2 * H_M, csc, csc + carry)))
        carry = carry + csc[:, CHUNK - 1:CHUNK]
    fulls = _each(lambda o: jnp.concatenate([o, pad], axis=0), outs)
    fullsT = _each(lambda f: f.T, fulls)
    pieces = _each(lambda o: jnp.concatenate(_split_bf16(o[2 * H_M:N_GATES, :] * LOG2E) + [epad], axis=0), outs)
    piecesT = _each(lambda p: p.T.astype(BF16), pieces)
    for sl, f, fT, pT in zip(sls, fulls, fullsT, piecesT):
        cT_ref[0, :, sl] = f
        c_ref[sl, :] = fT
        e_ref[sl, :] = pT


def _gate_scan(gT):
    b, r, s = gT.shape
    return pl.pallas_call(
        _gate_scan_kernel,
        grid=(b,),
        in_specs=[pl.BlockSpec((1, r, s), lambda bi: (bi, 0, 0))],
        out_specs=(pl.BlockSpec((1, r, s), lambda bi: (bi, 0, 0)),
                   pl.BlockSpec((s, r), lambda bi: (bi, 0)),
                   pl.BlockSpec((s, r), lambda bi: (bi, 0))),
        out_shape=(jax.ShapeDtypeStruct((b, r, s), F32), jax.ShapeDtypeStruct((b * s, r), F32),
                   jax.ShapeDtypeStruct((b * s, r), BF16)),
        compiler_params=_params("parallel"),
        name="gate_scan",
    )(gT)


def _each(f, *lists):
    return [f(*a) for a in zip(*lists)]


def _mlstm_gates(ig_row, b_row, ig_col, b_col, m_st, valid, n_valid):
    length = b_col[0].shape[0]
    dlog = _each(lambda bc, br, ir: jnp.where(valid, bc - br + ir, NEG_INF), b_col, b_row, ig_row)
    a_col = _each(lambda bc, m: bc + m, b_col, m_st)
    mx = _each(lambda d: jnp.max(d, axis=1, keepdims=True), dlog)
    m_t = _each(jnp.maximum, a_col, mx)
    w_intra = _each(lambda d, m: jnp.exp(d - m), dlog, m_t)
    w_inter = _each(lambda a, m: jnp.exp(a - m), a_col, m_t)
    floor = _each(lambda m: jnp.exp(-m), m_t)
    b_end = _each(lambda br: br[:, n_valid - 1:n_valid], b_row)
    a_end = _each(lambda be, m: be + m, b_end, m_st)
    wlog = _each(lambda be, bc, ic: be - bc + ic, b_end, b_col, ig_col)
    if n_valid < length:
        keep = lax.broadcasted_iota(jnp.int32, (length, 1), 0) < n_valid
        wlog = _each(lambda w: jnp.where(keep, w, NEG_INF), wlog)
    wmax = _each(lambda w: jnp.max(w, axis=0, keepdims=True), wlog)
    m_new = _each(jnp.maximum, a_end, wmax)
    ws = _each(lambda w, m: jnp.exp(w - m), wlog, m_new)
    decay = _each(lambda a, m: jnp.exp(a - m), a_end, m_new)
    return w_intra, w_inter, floor, ws, decay, m_new


def _mlstm_prompt_kernel(mq_ref, mk_ref, mv_ref, smo_ref, c_ref, cT_ref, gmh_ref,
                         hm_ref, caug_ref, m_ref):
    nb = mq_ref.shape[0]

    @pl.when(pl.program_id(0) == 0)
    def _():
        caug_ref[...] = jnp.zeros_like(caug_ref)
        m_ref[...] = jnp.zeros_like(m_ref)

    row = lax.broadcasted_iota(jnp.int32, (CHUNK, CHUNK), 0)
    col = lax.broadcasted_iota(jnp.int32, (CHUNK, CHUNK), 1)
    valid = col <= row
    lane = lax.broadcasted_iota(jnp.int32, (CHUNK, DV_M), 1)
    ones_col = jnp.where(lane == 0, 1.0, 0.0).astype(BF16)
    chains = [(b, h) for b in range(nb) for h in range(H_M)]
    sl = lambda h: slice(h * DK_M, (h + 1) * DK_M)

    q = [mq_ref[b, :, sl(h)] for b, h in chains]
    k = [mk_ref[b, :, sl(h)] for b, h in chains]
    v_aug = [jnp.concatenate([mv_ref[b, :, sl(h)], ones_col], axis=1) for b, h in chains]
    s_mat = _each(_dot_nt, q, k)
    r2 = _each(lambda qq, bh: _dot(qq, caug_ref[bh[0], bh[1]].astype(BF16)), q, chains)

    w_intra, w_inter, floor, ws, decay, m_new = _mlstm_gates(
        [cT_ref[b, h:h + 1, :] for b, h in chains],
        [cT_ref[b, H_M + h:H_M + h + 1, :] for b, h in chains],
        [c_ref[b, :, h:h + 1] for b, h in chains],
        [c_ref[b, :, H_M + h:H_M + h + 1] for b, h in chains],
        [m_ref[b, h:h + 1, 0:1] for b, h in chains], valid, CHUNK)
    sw = _each(lambda s, w: (s * w).astype(BF16), s_mat, w_intra)
    kw = _each(lambda kk, w: (kk.astype(F32) * w).astype(BF16), k, ws)

    r1 = _each(_dot, sw, v_aug)
    upd = _each(_dot_tn, kw, v_aug)

    num = _each(lambda w, a, c: w * a[:, :DV_M] + c[:, :DV_M], w_inter, r2, r1)
    den = _each(lambda w, a, c: w * a[:, DV_M:DV_M + 1] + c[:, DV_M:DV_M + 1], w_inter, r2, r1)
    hh = _each(lambda n, d, f: n / jnp.maximum(jnp.abs(d), f), num, den, floor)
    scale = _each(_rms_scale, hh)
    for (b, h), x, sc, dc, u, mn in zip(chains, hh, scale, decay, upd, m_new):
        hm_ref[b, :, sl(h)] = (x * sc * gmh_ref[:, sl(h)] * smo_ref[b, :, sl(h)].astype(F32)).astype(BF16)
        caug_ref[b, h] = dc * caug_ref[b, h] + u
        m_ref[b, h:h + 1, :] = jnp.broadcast_to(mn, (1, LANES))


def _mlstm_prompt(mq, mk, mv, smo, c_tok, cT, gmh):
    b, s, _ = mq.shape
    tok = lambda w: pl.BlockSpec((b, CHUNK, w), lambda c: (0, c, 0))
    whole = lambda shape: pl.BlockSpec(shape, lambda c: (0,) * len(shape))
    state = (b, H_M, DK_M, 2 * DV_M)
    return pl.pallas_call(
        _mlstm_prompt_kernel,
        grid=(s // CHUNK,),
        in_specs=[tok(W_M), tok(W_M), tok(W_M), tok(W_M), tok(GATE_ROWS),
                  pl.BlockSpec((b, GATE_ROWS, CHUNK), lambda c: (0, 0, c)),
                  whole(gmh.shape)],
        out_specs=(tok(W_M), whole(state), whole((b, 8, LANES))),
        out_shape=(jax.ShapeDtypeStruct((b, s, W_M), BF16),
                   jax.ShapeDtypeStruct(state, F32),
                   jax.ShapeDtypeStruct((b, 8, LANES), F32)),
        compiler_params=_params("arbitrary"),
        name="mlstm_prompt",
    )(mq, mk, mv, smo, c_tok, cT, gmh)


HEAD_GROUP = 2
QT = 256


def _fox_prompt_kernel(fqT_ref, fk_ref, e_ref, vT_ref, o_ref, qa_sc, m_sc, acc_sc, *, tq, tk):
    qi = pl.program_id(1)
    ratio = tq // tk
    nqt = tq // QT
    rowq = lax.broadcasted_iota(jnp.int32, (2 * DH_F, QT), 0)
    srow = lax.broadcasted_iota(jnp.int32, (tk, QT), 0)
    tcol = lax.broadcasted_iota(jnp.int32, (tk, QT), 1)
    hsl = lambda h: slice(h * DH_F, (h + 1) * DH_F)
    psl = lambda h: slice((h // 2) * 2 * DH_F, (h // 2 + 1) * 2 * DH_F)
    qsl = lambda n: slice(n * QT, (n + 1) * QT)

    for h in range(H_F):
        lo = (h % 2) * DH_F
        sel = jnp.where((rowq % H_F == h) & (rowq < N_SPLIT * H_F), -1.0, 0.0).astype(BF16)
        for n in range(nqt):
            q2 = fqT_ref[0, psl(h), qsl(n)]
            qh = jnp.where((rowq >= lo) & (rowq < lo + DH_F), q2, 0)
            qa_sc[h, n] = jnp.concatenate([qh, sel], axis=0)
    m_sc[...] = jnp.full_like(m_sc, NEG_INF)
    acc_sc[...] = jnp.zeros_like(acc_sc)
    groups = [range(g0, g0 + HEAD_GROUP) for g0 in range(0, H_F, HEAD_GROUP)]

    def run(tiles):
        work = [(ti, heads) for ti in range(len(tiles)) for heads in groups]
        k0 = [pl.multiple_of(j * tk, tk) for j, _ in tiles]
        e = [e_ref[pl.ds(k, tk), :] for k in k0]
        halves = [range(nqt) if d is None else range(d // QT, nqt) for _, d in tiles]
        st = {}

        def scores(ti, heads):
            for h in heads:
                ka = jnp.concatenate([fk_ref[pl.ds(k0[ti], tk), psl(h)], e[ti]], axis=1)
                for n in halves[ti]:
                    st[ti, h, n] = _dot(ka, qa_sc[h, n])

        scores(*work[0])
        for wi, (ti, heads) in enumerate(work):
            if wi + 1 < len(work):
                scores(*work[wi + 1])
            units = [(h, n) for h in heads for n in halves[ti]]
            p, alpha = {}, {}
            for h, n in units:
                s_u = st.pop((ti, h, n))
                d = tiles[ti][1]
                if d is not None and d + tk > n * QT:
                    s_u = jnp.where(srow + d <= tcol + n * QT, s_u, NEG_INF)
                m_old = m_sc[h, n]
                m_new = jnp.maximum(m_old, jnp.max(s_u, axis=0, keepdims=True))
                alpha[h, n] = jnp.exp2(m_old - m_new)
                p[h, n] = jnp.exp2(s_u - m_new).astype(BF16)
                m_sc[h, n] = m_new
            for h, n in units:
                vT = vT_ref[0, h * V_ROWS:(h + 1) * V_ROWS, pl.ds(k0[ti], tk)]
                acc_sc[h, n] = alpha[h, n] * acc_sc[h, n] + _dot(vT, p[h, n])

    def body(jj, carry):
        run([(jj * ratio + r, None) for r in range(ratio)])
        return carry

    lax.fori_loop(0, qi, body, 0)
    run([(qi * ratio + r, r * tk) for r in range(ratio)])
    for h in range(H_F):
        for n in range(nqt):
            o_ref[0, hsl(h), qsl(n)] = (acc_sc[h, n, :DH_F, :] / acc_sc[h, n, DH_F:DH_F + 1, :]).astype(BF16)


def _fox_prompt(fqT, fk_tok, e_tok, fvT16, tq, tk):
    b, w, s = fqT.shape
    kern = functools.partial(_fox_prompt_kernel, tq=tq, tk=tk)
    return pl.pallas_call(
        kern,
        grid=(b, s // tq),
        in_specs=[pl.BlockSpec((1, w, tq), lambda bi, qi: (bi, 0, qi)),
                  pl.BlockSpec((s, w), lambda bi, qi: (bi, 0)),
                  pl.BlockSpec((s, GATE_ROWS), lambda bi, qi: (bi, 0)),
                  pl.BlockSpec((1, H_F * V_ROWS, s), lambda bi, qi: (bi, 0, 0))],
        out_specs=pl.BlockSpec((1, w, tq), lambda bi, qi: (bi, 0, qi)),
        out_shape=jax.ShapeDtypeStruct((b, w, s), BF16),
        scratch_shapes=[pltpu.VMEM((H_F, tq // QT, 2 * DH_F + GATE_ROWS, QT), BF16),
                        pltpu.VMEM((H_F, tq // QT, 1, QT), F32),
                        pltpu.VMEM((H_F, tq // QT, V_ROWS, QT), F32)],
        compiler_params=_params("parallel", "arbitrary"),
        name="fox_prompt",
    )(fqT, fk_tok, e_tok, fvT16)


FF_TILE = 256


def _merge_ffn_kernel(x_ref, hm_ref, yfT_ref, sga_ref, sgb_ref, p_ref,
                      wpa_ref, wpb_ref, wout_ref, gffn_ref, wgu_ref, wdown_ref,
                      gple_ref, wpg_ref, wpp_ref, y_ref, hdn_sc):
    a = _dot(hm_ref[...], wpa_ref[...])
    bb = _dot_tn(yfT_ref[0], wpb_ref[...])
    u = sga_ref[...].astype(F32) * a + sgb_ref[...].astype(F32) * bb
    x1 = x_ref[...] + _dot(u.astype(BF16), wout_ref[...])
    xn = (x1 * _rms_scale(x1) * gffn_ref[...]).astype(BF16)
    for j in range(0, D_FF, FF_TILE):
        g = _dot(xn, wgu_ref[:, j:j + FF_TILE])
        up = _dot(xn, wgu_ref[:, D_FF + j:D_FF + j + FF_TILE])
        hdn_sc[:, j:j + FF_TILE] = (jax.nn.silu(g) * up).astype(BF16)
    x2 = x1 + _dot(hdn_sc[...], wdown_ref[...])
    xg = (x2 * _rms_scale(x2) * gple_ref[...]).astype(BF16)
    gate = jax.nn.sigmoid(_dot(xg, wpg_ref[...]))
    y_ref[...] = x2 + gate * _dot(p_ref[...].astype(BF16), wpp_ref[...])


def _merge_ffn(x, hm, yfT, sga, sgb, p, wpa, wpb, wout, gffn, wgu, wdown, gple, wpg, wpp, tm):
    t, d = x.shape
    b, _, s = yfT.shape
    nt = s // tm
    tok = lambda w: pl.BlockSpec((tm, w), lambda i: (i, 0))
    full = lambda a: pl.BlockSpec(a.shape, lambda i: (0,) * a.ndim)
    return pl.pallas_call(
        _merge_ffn_kernel,
        grid=(t // tm,),
        in_specs=[tok(d), tok(W_M),
                  pl.BlockSpec((1, W_F, tm), lambda i: (i // nt, 0, i % nt)),
                  tok(d), tok(d), tok(PLE_DIM),
                  full(wpa), full(wpb), full(wout), full(gffn), full(wgu), full(wdown),
                  full(gple), full(wpg), full(wpp)],
        out_specs=tok(d),
        out_shape=jax.ShapeDtypeStruct((t, d), F32),
        scratch_shapes=[pltpu.VMEM((tm, D_FF), BF16)],
        compiler_params=_params("parallel"),
        name="merge_ffn",
    )(x, hm, yfT, sga, sgb, p, wpa, wpb, wout, gffn, wgu, wdown, gple, wpg, wpp)


_SPLIT = (W_M, W_M, W_M, W_M, H_M, H_M, W_F, W_F, W_F, H_F, D_MODEL, D_MODEL)


def _prep_weights(g_mix, w_in, b_mi, b_mf, b_ff, g_mh, g_qn, g_kn, w_pa, w_pb, w_out, g_ffn, w_gu,
                  w_down, g_ple, w_ple_gate, w_ple_proj):
    offs = [0]
    for sz in _SPLIT:
        offs.append(offs[-1] + sz)
    o_mq, _, _, _, o_mi, o_mf, o_fq, _, _, o_ff, o_ga, _, o_end = offs
    w = w_in[0]
    wT = w.T
    wn = jnp.concatenate([w[:, o_mq:o_mi], w[:, o_ga:o_end]], axis=1).astype(BF16)
    wt = jnp.concatenate([wT[o_fq:o_ff], wT[o_mi:o_fq], wT[o_ff:o_ga],
                          jnp.zeros((GATE_ROWS - N_GATES, D_MODEL), F32)], axis=0).astype(BF16)
    bias_col = jnp.concatenate([b_mi[0], b_mf[0], b_ff[0], jnp.zeros((GATE_ROWS - N_GATES,), F32)])[:, None]
    gain_col = jnp.concatenate([jnp.tile(g_qn[0], H_F) * (F_SCALE * LOG2E), jnp.tile(g_kn[0], H_F)])[:, None]
    return dict(
        g_mix=g_mix, wn=wn, wt=wt, bias_col=bias_col, gain_col=gain_col,
        gmh=g_mh[0].reshape(1, W_M),
        wpa=w_pa[0].astype(BF16), wpb=w_pb[0].astype(BF16), wout=w_out[0].astype(BF16),
        gffn=g_ffn, wgu=w_gu[0].astype(BF16), wdown=w_down[0].astype(BF16),
        gple=g_ple, wpg=w_ple_gate[0].astype(BF16), wpp=w_ple_proj[0].astype(BF16))


def _prompt_path(x, p, wd, tm_in, tm_ffn, tq, tk, fox):
    b, s, d = x.shape
    (mq, mk, mv, smo, sga, sgb, fqT, fkT, fvT, fk_tok, fvT16, gT, yf_sample) = _in_proj(
        x, wd["g_mix"], wd["wn"], wd["wt"], wd["bias_col"], wd["gain_col"], tm_in, fox)
    cT, c_tok, e_tok = _gate_scan(gT)
    seq3 = lambda a: a.reshape(b, s, a.shape[-1])
    hm, caug, m_out = _mlstm_prompt(seq3(mq), seq3(mk), seq3(mv), seq3(smo), seq3(c_tok), cT, wd["gmh"])
    yfT = _fox_prompt(fqT, fk_tok, e_tok, fvT16, tq, tk)
    y = _merge_ffn(x.reshape(b * s, d), hm.reshape(b * s, W_M), yfT, sga, sgb, p.reshape(b * s, PLE_DIM),
                   wd["wpa"], wd["wpb"], wd["wout"], wd["gffn"], wd["wgu"], wd["wdown"],
                   wd["gple"], wd["wpg"], wd["wpp"], tm_ffn)
    to_cache = lambda aT: aT.reshape(b, H_F, DH_F, s).transpose(0, 3, 1, 2)[None]
    logf = gT[:, 2 * H_M:N_GATES, :].transpose(0, 2, 1)[None]
    return (y.reshape(b, s, d), to_cache(fkT), to_cache(fvT), logf,
            caug[None, :, :, :, :DV_M], caug[None, :, :, :, DV_M], m_out[None, :, :H_M, 0]), yf_sample


L_PAD = 16
SEQ_BLOCK = 8


def _mlstm_sample_kernel(q_ref, k_ref, v_ref, smo_ref, grow_ref, gcol_ref, c_ref, n_ref, m_ref, gmh_ref,
                         hm_ref, co_ref, no_ref, mo_ref, *, n_valid):
    length = L_PAD
    row = lax.broadcasted_iota(jnp.int32, (length, length), 0)
    col = lax.broadcasted_iota(jnp.int32, (length, length), 1)
    valid = (col <= row) & (col < n_valid)
    chains = [(i, h) for i in range(SEQ_BLOCK) for h in range(H_M)]
    sl = lambda h: slice(h * DK_M, (h + 1) * DK_M)

    q = [q_ref[i, :, sl(h)] for i, h in chains]
    k = [k_ref[i, :, sl(h)] for i, h in chains]
    v = [v_ref[i, :, sl(h)] for i, h in chains]
    n_st = [n_ref[i, h:h + 1, :] for i, h in chains]
    s_mat = _each(_dot_nt, q, k)
    r2 = _each(lambda qq, ih: _dot(qq, c_ref[ih[0], ih[1]].astype(BF16)), q, chains)
    qn = _each(lambda qq, n: jnp.sum(qq.astype(F32) * n, axis=1, keepdims=True), q, n_st)

    lf_row = [grow_ref[i, H_M + h:H_M + h + 1, 0:length] for i, h in chains]
    lf_col = [gcol_ref[i, :, H_M + h:H_M + h + 1] for i, h in chains]
    b_col = _each(lambda r: jnp.sum(jnp.where(col <= row, r, 0.0), axis=1, keepdims=True), lf_row)
    b_row = _each(lambda c: jnp.sum(jnp.where(row <= col, c, 0.0), axis=0, keepdims=True), lf_col)
    w_intra, w_inter, floor, ws, decay, m_new = _mlstm_gates(
        [grow_ref[i, h:h + 1, 0:length] for i, h in chains], b_row,
        [gcol_ref[i, :, h:h + 1] for i, h in chains], b_col,
        [m_ref[i:i + 1, h:h + 1] for i, h in chains], valid, n_valid)
    sw = _each(lambda s, w: s * w, s_mat, w_intra)
    kw = _each(lambda kk, w: kk.astype(F32) * w, k, ws)

    r1 = _each(lambda s, vv: _dot(s.astype(BF16), vv), sw, v)
    upd = _each(lambda kk, vv: _dot_tn(kk.astype(BF16), vv), kw, v)
    sw_sum = _each(lambda s: jnp.sum(s, axis=1, keepdims=True), sw)
    kw_sum = _each(lambda kk: jnp.sum(kk, axis=0, keepdims=True), kw)

    num = _each(lambda w, a, c: w * a + c, w_inter, r2, r1)
    den = _each(lambda w, a, c: w * a + c, w_inter, qn, sw_sum)
    hh = _each(lambda n, d, f: n / jnp.maximum(jnp.abs(d), f), num, den, floor)
    scale = _each(_rms_scale, hh)
    for (i, h), x, sc, dc, u, n, ks, mn in zip(chains, hh, scale, decay, upd, n_st, kw_sum, m_new):
        hm_ref[i, :, sl(h)] = (x * sc * gmh_ref[:, sl(h)] * smo_ref[i, :, sl(h)].astype(F32)).astype(BF16)
        co_ref[i, h] = dc * c_ref[i, h] + u
        no_ref[i, h:h + 1, :] = dc * n + ks
        mo_ref[i:i + 1, h:h + 1] = mn


def _mlstm_sample(q, k, v, smo, grow, gcol, c_st, n_st, m_st, gmh, n_valid):
    db = q.shape[0]
    blk = lambda a: pl.BlockSpec((SEQ_BLOCK,) + a.shape[1:], lambda i: (i,) + (0,) * (a.ndim - 1))
    kern = functools.partial(_mlstm_sample_kernel, n_valid=n_valid)
    return pl.pallas_call(
        kern,
        grid=(db // SEQ_BLOCK,),
        in_specs=[blk(q), blk(k), blk(v), blk(smo), blk(grow), blk(gcol), blk(c_st), blk(n_st), blk(m_st),
                  pl.BlockSpec(gmh.shape, lambda i: (0, 0))],
        out_specs=(blk(q), blk(c_st), blk(n_st), blk(m_st)),
        out_shape=(jax.ShapeDtypeStruct(q.shape, BF16), jax.ShapeDtypeStruct(c_st.shape, F32),
                   jax.ShapeDtypeStruct(n_st.shape, F32), jax.ShapeDtypeStruct(m_st.shape, F32)),
        compiler_params=_params("parallel"),
        name="mlstm_sample",
    )(q, k, v, smo, grow, gcol, c_st, n_st, m_st, gmh)


def _own_lanes():
    sub = lax.broadcasted_iota(jnp.int32, (H_F, W_F), 0)
    head_of_lane = lax.broadcasted_iota(jnp.int32, (H_F, W_F), 1) // DH_F
    return sub == head_of_lane


def _fox_sample_scores(fq, kn_new, lfn, k_page, lf_pages, n_pages, n_tok):
    rows = n_tok * H_F
    own = _own_lanes()
    tile = lambda a: jnp.concatenate([a] * n_tok, axis=0)
    qbd = jnp.concatenate(
        [jnp.where(own, jnp.broadcast_to(fq[t:t + 1, :], (H_F, W_F)), 0.0) for t in range(n_tok)], axis=0)
    qbd16 = qbd.astype(BF16)

    incl_all = _lane_cumsum(lf_pages)
    bias = [None] * n_pages
    carry = jnp.zeros((H_F, 1), F32)
    for pg in reversed(range(n_pages)):
        incl = incl_all[pg * H_F:(pg + 1) * H_F, :]
        tot = incl[:, LANES - 1:LANES]
        bias[pg] = (tot - incl + carry) * LOG2E
        carry = carry + tot

    logits = []
    for j in range(0, n_pages, 2):
        kcat = jnp.concatenate([k_page(j).astype(BF16), k_page(j + 1).astype(BF16)], axis=1)
        bcat = jnp.concatenate([tile(bias[j]), tile(bias[j + 1])], axis=1)
        logits.append(_dot(qbd16, kcat) + bcat)

    rowi = lax.broadcasted_iota(jnp.int32, (rows, 1), 0)
    qf = qbd16.astype(F32)
    new_logits = []
    f_new = jnp.zeros((H_F, 1), F32)
    for s in range(n_tok):
        f_new = f_new + lfn[:, s:s + 1]
        kn = kn_new[s:s + 1, :].astype(BF16).astype(F32)
        ln = jnp.sum(qf * kn, axis=1, keepdims=True) - tile(f_new) * LOG2E
        new_logits.append(jnp.where(rowi >= s * H_F, ln, NEG_INF))

    m = new_logits[0]
    for ln in new_logits[1:]:
        m = jnp.maximum(m, ln)
    for lg in logits:
        m = jnp.maximum(m, jnp.max(lg, axis=1, keepdims=True))
    return logits, new_logits, m


def _fox_sample_output(logits, new_logits, m, v_page, vn_new, n_pages, n_tok):
    rows = n_tok * H_F
    own = _own_lanes()
    denom = jnp.zeros((rows, 1), F32)
    acc = jnp.zeros((rows, W_F), F32)
    for idx, lg in enumerate(logits):
        p = jnp.exp2(lg - m)
        denom = denom + jnp.sum(p, axis=1, keepdims=True)
        j = 2 * idx
        vcat = jnp.concatenate([v_page(j).astype(BF16), v_page(j + 1).astype(BF16)], axis=1)
        acc = acc + _dot_nt(p.astype(BF16), vcat)
    for s, ln in enumerate(new_logits):
        p = jnp.exp2(ln - m)
        denom = denom + p
        acc = acc + p.astype(BF16).astype(F32) * vn_new[s:s + 1, :]
    out = acc / denom
    return jnp.concatenate(
        [jnp.sum(jnp.where(own, out[t * H_F:(t + 1) * H_F, :], 0.0), axis=0, keepdims=True)
         for t in range(n_tok)], axis=0)


def _sample_front(x, cache_k, cache_v, cache_logf, state_c, state_n, state_m, page_table, wd):
    db, n_tok, d = x.shape
    t = db * n_tok
    (mq, mk, mv, smo, sga, sgb, fqT, fkT, fvT, fk_tok, _, gT) = _in_proj(
        x.reshape(1, t, d), wd["g_mix"], wd["wn"], wd["wt"], wd["bias_col"], wd["gain_col"], t)

    pad_tok = lambda a: jnp.pad(a.reshape(db, n_tok, -1), ((0, 0), (0, L_PAD - n_tok), (0, 0)))
    g16 = gT[0, :2 * 8].reshape(2 * 8, db, n_tok)
    grow = jnp.pad(g16.transpose(1, 0, 2), ((0, 0), (0, 0), (0, LANES - n_tok)))
    gcol = jnp.pad(g16.transpose(1, 2, 0), ((0, 0), (0, L_PAD - n_tok), (0, LANES - 2 * 8)))
    hm, c_new, n_new, m_new = _mlstm_sample(
        pad_tok(mq), pad_tok(mk), pad_tok(mv), pad_tok(smo), grow, gcol,
        state_c, state_n, state_m, wd["gmh"], n_tok)
    hm = hm[:, :n_tok].reshape(t, W_M)

    pool, page = cache_k.shape[0], cache_k.shape[1]
    ck = cache_k.transpose(0, 2, 3, 1).reshape(pool, W_F, page)
    cv = cache_v.transpose(0, 2, 3, 1).reshape(pool, W_F, page)
    cl = cache_logf.transpose(0, 2, 1)
    tok3 = lambda aT: aT[0].T.astype(F32).reshape(db, n_tok, W_F)
    lfn = jnp.pad(gT[0, 2 * H_M:N_GATES].reshape(H_F, db, n_tok).transpose(1, 0, 2),
                  ((0, 0), (0, 0), (0, LANES - n_tok)))
    fox = (page_table, tok3(fqT), fk_tok.astype(F32).reshape(db, n_tok, W_F), tok3(fvT), lfn, ck, cv, cl)
    to_cache = lambda aT: aT[0].reshape(H_F, DH_F, db, n_tok).transpose(2, 3, 0, 1)[None]
    logf = gT[0, 2 * H_M:N_GATES].reshape(H_F, db, n_tok).transpose(1, 2, 0)[None]
    outs = (to_cache(fkT), to_cache(fvT), logf, c_new[None], n_new[None], m_new[None])
    return fox, (hm, sga, sgb), outs


def _sample_back(x, p, yf, hm, sga, sgb, wd):
    db, n_tok, d = x.shape
    t = db * n_tok
    yfT = yf.reshape(t, W_F).T.astype(BF16)[None]
    y = _merge_ffn(x.reshape(t, d), hm, yfT, sga, sgb, p.reshape(t, PLE_DIM),
                   wd["wpa"], wd["wpb"], wd["wout"], wd["gffn"], wd["wgu"], wd["wdown"],
                   wd["gple"], wd["wpg"], wd["wpp"], t)
    return y.reshape(db, n_tok, d)


def kernel(x_prompt, x_sample, p_prompt, p_sample, cache_k, cache_v, cache_logf, state_C, state_n, state_m,
           page_table, g_mix, w_in, b_mi, b_mf, b_ff, g_mh, g_qn, g_kn, w_pa, w_pb, w_out, g_ffn, w_gu,
           w_down, g_ple, w_ple_gate, w_ple_proj):
    wd = _prep_weights(g_mix, w_in, b_mi, b_mf, b_ff, g_mh, g_qn, g_kn, w_pa, w_pb, w_out, g_ffn, w_gu,
                       w_down, g_ple, w_ple_gate, w_ple_proj)
    fox, mid, (ks, vs, lfs, cs, ns, ms) = _sample_front(
        x_sample, cache_k[0], cache_v[0], cache_logf[0], state_C[0], state_n[0], state_m[0], page_table, wd)
    (yp, kp, vp, lfp, cp, np_, mp), yf_sample = _prompt_path(x_prompt, p_prompt[0], wd, 512, 512, 512, 256, fox)
    ys = _sample_back(x_sample, p_sample[0], yf_sample, *mid, wd)
    return (yp, ys, kp, vp, lfp, cp, np_, mp, ks, vs, lfs, cs, ns, ms)
```

```python
import functools

import jax
import jax.numpy as jnp
from jax import lax
from jax.experimental import pallas as pl
from jax.experimental.pallas import tpu as pltpu

D_MODEL = 1024
H_M = 4
DK_M = 128
DV_M = 128
W_M = H_M * DV_M
H_F = 8
DH_F = 64
W_F = H_F * DH_F
D_FF = 2816
PLE_DIM = 256
CHUNK = 128
EPS = 1e-6
F_SCALE = DH_F ** -0.5
K_SCALE_M = DK_M ** -0.5
LOG2E = 1.4426950408889634
N_GATES = 2 * H_M + H_F
GATE_ROWS = 128
N_SPLIT = 3
V_ROWS = DH_F + 16
LANES = 128
VMEM_LIMIT = 56 * 1024 * 1024
VMEM_LIMIT_HOST = 62 * 1024 * 1024

BF16 = jnp.bfloat16
F32 = jnp.float32
NEG_INF = float("-inf")


def _dot(a, b):
    return jnp.dot(a, b, preferred_element_type=F32)


def _dot_nt(a, b):
    return lax.dot_general(a, b, (((1,), (1,)), ((), ())), preferred_element_type=F32)


def _dot_tn(a, b):
    return lax.dot_general(a, b, (((0,), (0,)), ((), ())), preferred_element_type=F32)


def _log_sigmoid(u):
    return jnp.minimum(u, 0.0) - jnp.log1p(jnp.exp(-jnp.abs(u)))


def _rms_scale(x):
    return lax.rsqrt(jnp.mean(x * x, axis=-1, keepdims=True) + EPS)


def _params(*sem, vmem=VMEM_LIMIT):
    return pltpu.CompilerParams(dimension_semantics=sem, vmem_limit_bytes=vmem)


def _in_proj_kernel(x_ref, gmix_ref, wn_ref, wt_ref, bias_ref, gain_ref,
                    mq_ref, mk_ref, mv_ref, smo_ref, sga_ref, sgb_ref,
                    fqT_ref, fkT_ref, fvT_ref, fk_ref, fvT16_ref, gT_ref):
    x = x_ref[0]
    xn = (x * _rms_scale(x) * gmix_ref[...]).astype(BF16)

    def zn(lo, hi):
        return _dot(xn, wn_ref[:, lo:hi])

    def zt(lo, hi):
        return _dot_nt(wt_ref[lo:hi, :], xn)

    def head_norm(z, gain):
        tm = z.shape[-1]
        r = z.reshape(H_F, DH_F, tm)
        r = r * lax.rsqrt(jnp.mean(r * r, axis=1, keepdims=True) + EPS)
        return r.reshape(W_F, tm) * gain

    def g_mq():
        mq_ref[...] = zn(0, W_M).astype(BF16)

    def g_mk():
        mk_ref[...] = (zn(W_M, 2 * W_M) * K_SCALE_M).astype(BF16)

    def g_mv():
        mv_ref[...] = zn(2 * W_M, 3 * W_M).astype(BF16)

    def g_smo():
        smo_ref[...] = jax.nn.sigmoid(zn(3 * W_M, 4 * W_M)).astype(BF16)

    def g_gate(ref, off, c):
        def run():
            ref[:, c:c + W_M] = jax.nn.sigmoid(zn(off + c, off + c + W_M)).astype(BF16)
        return run

    def g_fq():
        fqT_ref[0] = head_norm(zt(0, W_F), gain_ref[0:W_F, :]).astype(BF16)

    def g_fk():
        fk = head_norm(zt(W_F, 2 * W_F), gain_ref[W_F:2 * W_F, :])
        fkT_ref[0] = fk
        fk_ref[...] = fk.T.astype(BF16)

    def g_fv():
        fv = zt(2 * W_F, 3 * W_F)
        fvT_ref[0] = fv
        tm = fv.shape[-1]
        fv_aug = jnp.concatenate([fv.reshape(H_F, DH_F, tm), jnp.ones((H_F, V_ROWS - DH_F, tm), F32)], axis=1)
        fvT16_ref[0] = fv_aug.reshape(H_F * V_ROWS, tm).astype(BF16)

    def g_gates():
        u = zt(3 * W_F, 3 * W_F + GATE_ROWS) + bias_ref[...]
        row = lax.broadcasted_iota(jnp.int32, u.shape, 0)
        gT_ref[0] = jnp.where(row < H_M, u, jnp.where(row < N_GATES, _log_sigmoid(u), 0.0))

    g_mq()
    g_mk()
    g_mv()
    g_smo()
    for ref, off in ((sga_ref, 4 * W_M), (sgb_ref, 4 * W_M + D_MODEL)):
        for c in range(0, D_MODEL, W_M):
            g_gate(ref, off, c)()
    g_fq()
    g_fk()
    g_fv()
    g_gates()


def _in_proj(x, g_mix, wn, wt, bias_col, gain_col, tm):
    b, s, d = x.shape
    nt = s // tm
    t = b * s
    tok = lambda w: pl.BlockSpec((tm, w), lambda bi, i: (bi * nt + i, 0))
    hm = lambda r: pl.BlockSpec((1, r, tm), lambda bi, i: (bi, 0, i))
    full = lambda a: pl.BlockSpec(a.shape, lambda bi, i: (0,) * a.ndim)
    out_shape = (
        jax.ShapeDtypeStruct((t, W_M), BF16),
        jax.ShapeDtypeStruct((t, W_M), BF16),
        jax.ShapeDtypeStruct((t, W_M), BF16),
        jax.ShapeDtypeStruct((t, W_M), BF16),
        jax.ShapeDtypeStruct((t, D_MODEL), BF16),
        jax.ShapeDtypeStruct((t, D_MODEL), BF16),
        jax.ShapeDtypeStruct((b, W_F, s), BF16),
        jax.ShapeDtypeStruct((b, W_F, s), F32),
        jax.ShapeDtypeStruct((b, W_F, s), F32),
        jax.ShapeDtypeStruct((t, W_F), BF16),
        jax.ShapeDtypeStruct((b, H_F * V_ROWS, s), BF16),
        jax.ShapeDtypeStruct((b, GATE_ROWS, s), F32),
    )
    out_specs = (tok(W_M), tok(W_M), tok(W_M), tok(W_M), tok(D_MODEL), tok(D_MODEL),
                 hm(W_F), hm(W_F), hm(W_F), tok(W_F), hm(H_F * V_ROWS), hm(GATE_ROWS))
    return pl.pallas_call(
        _in_proj_kernel,
        grid=(b, nt),
        in_specs=[pl.BlockSpec((1, tm, d), lambda bi, i: (bi, i, 0)),
                  full(g_mix), full(wn), full(wt), full(bias_col), full(gain_col)],
        out_specs=out_specs,
        out_shape=out_shape,
        compiler_params=_params("parallel", "parallel"),
        name="in_proj",
    )(x, g_mix, wn, wt, bias_col, gain_col)


def _lane_cumsum(x):
    lane = lax.broadcasted_iota(jnp.int32, x.shape, x.ndim - 1)
    sh = 1
    while sh < x.shape[-1]:
        x = x + jnp.where(lane >= sh, pltpu.roll(x, sh, x.ndim - 1), 0.0)
        sh *= 2
    return x


def _split_bf16(x):
    pieces = []
    for _ in range(N_SPLIT):
        p = x.astype(BF16).astype(F32)
        pieces.append(p)
        x = x - p
    return pieces


def _gate_scan_kernel(gT_ref, cT_ref, c_ref, e_ref):
    s = gT_ref.shape[-1]
    rows = 2 * 8
    row = lax.broadcasted_iota(jnp.int32, (rows, CHUNK), 0)
    pad = jnp.zeros((GATE_ROWS - rows, CHUNK), F32)
    epad = jnp.zeros((GATE_ROWS - N_SPLIT * H_F, CHUNK), F32)
    sls = [slice(c * CHUNK, (c + 1) * CHUNK) for c in range(s // CHUNK)]
    g = [gT_ref[0, 0:rows, sl] for sl in sls]
    lane = lax.broadcasted_iota(jnp.int32, (rows, CHUNK), 1)
    cs = g
    sh = 1
    while sh < CHUNK:
        cs = _each(lambda x: x + jnp.where(lane >= sh, pltpu.roll(x, sh, 1), 0.0), cs)
        sh *= 2
    carry = jnp.zeros((rows, 1), F32)
    outs = []
    for gc, csc in zip(g, cs):
        outs.append(jnp.where(row < H_M, gc, jnp.where(row < 2 * H_M, csc, csc + carry)))
        carry = carry + csc[:, CHUNK - 1:CHUNK]
    fulls = _each(lambda o: jnp.concatenate([o, pad], axis=0), outs)
    fullsT = _each(lambda f: f.T, fulls)
    pieces = _each(lambda o: jnp.concatenate(_split_bf16(o[2 * H_M:N_GATES, :] * LOG2E) + [epad], axis=0), outs)
    piecesT = _each(lambda p: p.T.astype(BF16), pieces)
    for sl, f, fT, pT in zip(sls, fulls, fullsT, piecesT):
        cT_ref[0, :, sl] = f
        c_ref[sl, :] = fT
        e_ref[sl, :] = pT


def _gate_scan(gT):
    b, r, s = gT.shape
    return pl.pallas_call(
        _gate_scan_kernel,
        grid=(b,),
        in_specs=[pl.BlockSpec((1, r, s), lambda bi: (bi, 0, 0))],
        out_specs=(pl.BlockSpec((1, r, s), lambda bi: (bi, 0, 0)),
                   pl.BlockSpec((s, r), lambda bi: (bi, 0)),
                   pl.BlockSpec((s, r), lambda bi: (bi, 0))),
        out_shape=(jax.ShapeDtypeStruct((b, r, s), F32), jax.ShapeDtypeStruct((b * s, r), F32),
                   jax.ShapeDtypeStruct((b * s, r), BF16)),
        compiler_params=_params("parallel"),
        name="gate_scan",
    )(gT)


def _each(f, *lists):
    return [f(*a) for a in zip(*lists)]


def _mlstm_gates(ig_row, b_row, ig_col, b_col, m_st, valid, n_valid):
    length = b_col[0].shape[0]
    dlog = _each(lambda bc, br, ir: jnp.where(valid, bc - br + ir, NEG_INF), b_col, b_row, ig_row)
    a_col = _each(lambda bc, m: bc + m, b_col, m_st)
    mx = _each(lambda d: jnp.max(d, axis=1, keepdims=True), dlog)
    m_t = _each(jnp.maximum, a_col, mx)
    w_intra = _each(lambda d, m: jnp.exp(d - m), dlog, m_t)
    w_inter = _each(lambda a, m: jnp.exp(a - m), a_col, m_t)
    floor = _each(lambda m: jnp.exp(-m), m_t)
    b_end = _each(lambda br: br[:, n_valid - 1:n_valid], b_row)
    a_end = _each(lambda be, m: be + m, b_end, m_st)
    wlog = _each(lambda be, bc, ic: be - bc + ic, b_end, b_col, ig_col)
    if n_valid < length:
        keep = lax.broadcasted_iota(jnp.int32, (length, 1), 0) < n_valid
        wlog = _each(lambda w: jnp.where(keep, w, NEG_INF), wlog)
    wmax = _each(lambda w: jnp.max(w, axis=0, keepdims=True), wlog)
    m_new = _each(jnp.maximum, a_end, wmax)
    ws = _each(lambda w, m: jnp.exp(w - m), wlog, m_new)
    decay = _each(lambda a, m: jnp.exp(a - m), a_end, m_new)
    return w_intra, w_inter, floor, ws, decay, m_new


def _mlstm_prompt_kernel(mq_ref, mk_ref, mv_ref, smo_ref, c_ref, cT_ref, gmh_ref,
                         hm_ref, caug_ref, m_ref):
    nb = mq_ref.shape[0]

    @pl.when(pl.program_id(0) == 0)
    def _():
        caug_ref[...] = jnp.zeros_like(caug_ref)
        m_ref[...] = jnp.zeros_like(m_ref)

    row = lax.broadcasted_iota(jnp.int32, (CHUNK, CHUNK), 0)
    col = lax.broadcasted_iota(jnp.int32, (CHUNK, CHUNK), 1)
    valid = col <= row
    lane = lax.broadcasted_iota(jnp.int32, (CHUNK, DV_M), 1)
    ones_col = jnp.where(lane == 0, 1.0, 0.0).astype(BF16)
    chains = [(b, h) for b in range(nb) for h in range(H_M)]
    sl = lambda h: slice(h * DK_M, (h + 1) * DK_M)

    q = [mq_ref[b, :, sl(h)] for b, h in chains]
    k = [mk_ref[b, :, sl(h)] for b, h in chains]
    v_aug = [jnp.concatenate([mv_ref[b, :, sl(h)], ones_col], axis=1) for b, h in chains]
    s_mat = _each(_dot_nt, q, k)
    r2 = _each(lambda qq, bh: _dot(qq, caug_ref[bh[0], bh[1]].astype(BF16)), q, chains)

    w_intra, w_inter, floor, ws, decay, m_new = _mlstm_gates(
        [cT_ref[b, h:h + 1, :] for b, h in chains],
        [cT_ref[b, H_M + h:H_M + h + 1, :] for b, h in chains],
        [c_ref[b, :, h:h + 1] for b, h in chains],
        [c_ref[b, :, H_M + h:H_M + h + 1] for b, h in chains],
        [m_ref[b, h:h + 1, 0:1] for b, h in chains], valid, CHUNK)
    sw = _each(lambda s, w: (s * w).astype(BF16), s_mat, w_intra)
    kw = _each(lambda kk, w: (kk.astype(F32) * w).astype(BF16), k, ws)

    r1 = _each(_dot, sw, v_aug)
    upd = _each(_dot_tn, kw, v_aug)

    num = _each(lambda w, a, c: w * a[:, :DV_M] + c[:, :DV_M], w_inter, r2, r1)
    den = _each(lambda w, a, c: w * a[:, DV_M:DV_M + 1] + c[:, DV_M:DV_M + 1], w_inter, r2, r1)
    hh = _each(lambda n, d, f: n / jnp.maximum(jnp.abs(d), f), num, den, floor)
    scale = _each(_rms_scale, hh)
    for (b, h), x, sc, dc, u, mn in zip(chains, hh, scale, decay, upd, m_new):
        hm_ref[b, :, sl(h)] = (x * sc * gmh_ref[:, sl(h)] * smo_ref[b, :, sl(h)].astype(F32)).astype(BF16)
        caug_ref[b, h] = dc * caug_ref[b, h] + u
        m_ref[b, h:h + 1, :] = jnp.broadcast_to(mn, (1, LANES))


def _mlstm_prompt(mq, mk, mv, smo, c_tok, cT, gmh):
    b, s, _ = mq.shape
    tok = lambda w: pl.BlockSpec((b, CHUNK, w), lambda c: (0, c, 0))
    whole = lambda shape: pl.BlockSpec(shape, lambda c: (0,) * len(shape))
    state = (b, H_M, DK_M, 2 * DV_M)
    return pl.pallas_call(
        _mlstm_prompt_kernel,
        grid=(s // CHUNK,),
        in_specs=[tok(W_M), tok(W_M), tok(W_M), tok(W_M), tok(GATE_ROWS),
                  pl.BlockSpec((b, GATE_ROWS, CHUNK), lambda c: (0, 0, c)),
                  whole(gmh.shape)],
        out_specs=(tok(W_M), whole(state), whole((b, 8, LANES))),
        out_shape=(jax.ShapeDtypeStruct((b, s, W_M), BF16),
                   jax.ShapeDtypeStruct(state, F32),
                   jax.ShapeDtypeStruct((b, 8, LANES), F32)),
        compiler_params=_params("arbitrary"),
        name="mlstm_prompt",
    )(mq, mk, mv, smo, c_tok, cT, gmh)


HEAD_GROUP = 2
QT = 256


def _fox_prompt_kernel(fqT_ref, fk_ref, e_ref, vT_ref, o_ref, qa_sc, m_sc, acc_sc, *, tq, tk):
    qi = pl.program_id(1)
    ratio = tq // tk
    nqt = tq // QT
    rowq = lax.broadcasted_iota(jnp.int32, (2 * DH_F, QT), 0)
    srow = lax.broadcasted_iota(jnp.int32, (tk, QT), 0)
    tcol = lax.broadcasted_iota(jnp.int32, (tk, QT), 1)
    hsl = lambda h: slice(h * DH_F, (h + 1) * DH_F)
    psl = lambda h: slice((h // 2) * 2 * DH_F, (h // 2 + 1) * 2 * DH_F)
    qsl = lambda n: slice(n * QT, (n + 1) * QT)

    for h in range(H_F):
        lo = (h % 2) * DH_F
        sel = jnp.where((rowq % H_F == h) & (rowq < N_SPLIT * H_F), -1.0, 0.0).astype(BF16)
        for n in range(nqt):
            q2 = fqT_ref[0, psl(h), qsl(n)]
            qh = jnp.where((rowq >= lo) & (rowq < lo + DH_F), q2, 0)
            qa_sc[h, n] = jnp.concatenate([qh, sel], axis=0)
    m_sc[...] = jnp.full_like(m_sc, NEG_INF)
    acc_sc[...] = jnp.zeros_like(acc_sc)
    groups = [range(g0, g0 + HEAD_GROUP) for g0 in range(0, H_F, HEAD_GROUP)]

    def run(tiles):
        work = [(ti, heads) for ti in range(len(tiles)) for heads in groups]
        k0 = [pl.multiple_of(j * tk, tk) for j, _ in tiles]
        e = [e_ref[pl.ds(k, tk), :] for k in k0]
        halves = [range(nqt) if d is None else range(d // QT, nqt) for _, d in tiles]
        st = {}

        def scores(ti, heads):
            for h in heads:
                ka = jnp.concatenate([fk_ref[pl.ds(k0[ti], tk), psl(h)], e[ti]], axis=1)
                for n in halves[ti]:
                    st[ti, h, n] = _dot(ka, qa_sc[h, n])

        scores(*work[0])
        for wi, (ti, heads) in enumerate(work):
            if wi + 1 < len(work):
                scores(*work[wi + 1])
            units = [(h, n) for h in heads for n in halves[ti]]
            p, alpha = {}, {}
            for h, n in units:
                s_u = st.pop((ti, h, n))
                d = tiles[ti][1]
                if d is not None and d + tk > n * QT:
                    s_u = jnp.where(srow + d <= tcol + n * QT, s_u, NEG_INF)
                m_old = m_sc[h, n]
                m_new = jnp.maximum(m_old, jnp.max(s_u, axis=0, keepdims=True))
                alpha[h, n] = jnp.exp2(m_old - m_new)
                p[h, n] = jnp.exp2(s_u - m_new).astype(BF16)
                m_sc[h, n] = m_new
            for h, n in units:
                vT = vT_ref[0, h * V_ROWS:(h + 1) * V_ROWS, pl.ds(k0[ti], tk)]
                acc_sc[h, n] = alpha[h, n] * acc_sc[h, n] + _dot(vT, p[h, n])

    def body(jj, carry):
        run([(jj * ratio + r, None) for r in range(ratio)])
        return carry

    lax.fori_loop(0, qi, body, 0)
    run([(qi * ratio + r, r * tk) for r in range(ratio)])
    for h in range(H_F):
        for n in range(nqt):
            o_ref[0, hsl(h), qsl(n)] = (acc_sc[h, n, :DH_F, :] / acc_sc[h, n, DH_F:DH_F + 1, :]).astype(BF16)


def _fox_prompt(fqT, fk_tok, e_tok, fvT16, tq, tk):
    b, w, s = fqT.shape
    kern = functools.partial(_fox_prompt_kernel, tq=tq, tk=tk)
    return pl.pallas_call(
        kern,
        grid=(b, s // tq),
        in_specs=[pl.BlockSpec((1, w, tq), lambda bi, qi: (bi, 0, qi)),
                  pl.BlockSpec((s, w), lambda bi, qi: (bi, 0)),
                  pl.BlockSpec((s, GATE_ROWS), lambda bi, qi: (bi, 0)),
                  pl.BlockSpec((1, H_F * V_ROWS, s), lambda bi, qi: (bi, 0, 0))],
        out_specs=pl.BlockSpec((1, w, tq), lambda bi, qi: (bi, 0, qi)),
        out_shape=jax.ShapeDtypeStruct((b, w, s), BF16),
        scratch_shapes=[pltpu.VMEM((H_F, tq // QT, 2 * DH_F + GATE_ROWS, QT), BF16),
                        pltpu.VMEM((H_F, tq // QT, 1, QT), F32),
                        pltpu.VMEM((H_F, tq // QT, V_ROWS, QT), F32)],
        compiler_params=_params("parallel", "arbitrary"),
        name="fox_prompt",
    )(fqT, fk_tok, e_tok, fvT16)


FF_TILE = 256


def _merge_ffn_kernel(*refs, fox):
    refs = list(refs)
    pt_ref = refs.pop(0) if fox else None
    (x_ref, hm_ref, yfT_ref, sga_ref, sgb_ref, p_ref, wpa_ref, wpb_ref, wout_ref, gffn_ref, wgu_ref,
     wdown_ref, gple_ref, wpg_ref, wpp_ref) = refs[:15]
    del refs[:15]
    fox_in = [refs.pop(0) for _ in range(7)] if fox else None
    y_ref = refs.pop(0)
    yf_ref = refs.pop(0) if fox else None
    hdn_sc = refs.pop(0)
    val = {}

    def g_merge():
        a = _dot(hm_ref[...], wpa_ref[...])
        bb = _dot_tn(yfT_ref[0], wpb_ref[...])
        u = sga_ref[...].astype(F32) * a + sgb_ref[...].astype(F32) * bb
        x1 = x_ref[...] + _dot(u.astype(BF16), wout_ref[...])
        val["x1"] = x1
        val["xn"] = (x1 * _rms_scale(x1) * gffn_ref[...]).astype(BF16)

    def g_ffn(j):
        def run():
            g = _dot(val["xn"], wgu_ref[:, j:j + FF_TILE])
            up = _dot(val["xn"], wgu_ref[:, D_FF + j:D_FF + j + FF_TILE])
            hdn_sc[:, j:j + FF_TILE] = (jax.nn.silu(g) * up).astype(BF16)
        return run

    def g_out():
        x2 = val["x1"] + _dot(hdn_sc[...], wdown_ref[...])
        xg = (x2 * _rms_scale(x2) * gple_ref[...]).astype(BF16)
        gate = jax.nn.sigmoid(_dot(xg, wpg_ref[...]))
        y_ref[...] = x2 + gate * _dot(p_ref[...].astype(BF16), wpp_ref[...])

    groups = [g_merge] + [g_ffn(j) for j in range(0, D_FF, FF_TILE)] + [g_out]
    if fox:
        _run_with_hosted_fox(groups, pl.program_id(0), fox, pt_ref, *fox_in, yf_ref, *refs)
    else:
        for g in groups:
            g()


def _merge_ffn(x, hm, yfT, sga, sgb, p, wpa, wpb, wout, gffn, wgu, wdown, gple, wpg, wpp, tm, fox=None):
    t, d = x.shape
    b, _, s = yfT.shape
    nt = s // tm
    steps = t // tm
    tok = lambda w: pl.BlockSpec((tm, w), lambda i, *_: (i, 0))
    full = lambda a: pl.BlockSpec(a.shape, lambda i, *_: (0,) * a.ndim)
    in_specs = [tok(d), tok(W_M),
                pl.BlockSpec((1, W_F, tm), lambda i, *_: (i // nt, 0, i % nt)),
                tok(d), tok(d), tok(PLE_DIM),
                full(wpa), full(wpb), full(wout), full(gffn), full(wgu), full(wdown),
                full(gple), full(wpg), full(wpp)]
    operands = (x, hm, yfT, sga, sgb, p, wpa, wpb, wout, gffn, wgu, wdown, gple, wpg, wpp)
    hdn = pltpu.VMEM((tm, D_FF), BF16)
    if fox is None:
        return pl.pallas_call(
            functools.partial(_merge_ffn_kernel, fox=None),
            grid=(steps,),
            in_specs=in_specs,
            out_specs=tok(d),
            out_shape=jax.ShapeDtypeStruct((t, d), F32),
            scratch_shapes=[hdn],
            compiler_params=_params("parallel"),
            name="merge_ffn",
        )(*operands)

    page_table, fq, kn, vn, lfn, ck, cv, cl = fox
    db, n_tok, _ = fq.shape
    n_pages, page = page_table.shape[1], ck.shape[-1]
    per = db // steps
    assert per * steps == db and per % 2 == 0
    seqs = lambda a: pl.BlockSpec((per,) + a.shape[1:], lambda i, *_: (i,) + (0,) * (a.ndim - 1))
    hbm = pl.BlockSpec(memory_space=pl.ANY)
    return pl.pallas_call(
        functools.partial(_merge_ffn_kernel, fox=(per, db, n_pages, n_tok)),
        grid_spec=pltpu.PrefetchScalarGridSpec(
            num_scalar_prefetch=1, grid=(steps,),
            in_specs=in_specs + [seqs(fq), seqs(kn), seqs(vn), seqs(lfn), hbm, hbm, hbm],
            out_specs=(tok(d), seqs(fq)),
            scratch_shapes=[hdn,
                            pltpu.VMEM((2, n_pages, W_F, page), F32),
                            pltpu.VMEM((2, n_pages, W_F, page), F32),
                            pltpu.VMEM((2, n_pages, H_F, page), F32),
                            pltpu.SemaphoreType.DMA((2,))]),
        out_shape=(jax.ShapeDtypeStruct((t, d), F32), jax.ShapeDtypeStruct(fq.shape, F32)),
        compiler_params=_params("arbitrary", vmem=VMEM_LIMIT_HOST),
        name="merge_ffn_fox_sample",
    )(page_table, *operands, fq, kn, vn, lfn, ck, cv, cl)


_SPLIT = (W_M, W_M, W_M, W_M, H_M, H_M, W_F, W_F, W_F, H_F, D_MODEL, D_MODEL)


def _prep_weights(g_mix, w_in, b_mi, b_mf, b_ff, g_mh, g_qn, g_kn, w_pa, w_pb, w_out, g_ffn, w_gu,
                  w_down, g_ple, w_ple_gate, w_ple_proj):
    offs = [0]
    for sz in _SPLIT:
        offs.append(offs[-1] + sz)
    o_mq, _, _, _, o_mi, o_mf, o_fq, _, _, o_ff, o_ga, _, o_end = offs
    w = w_in[0]
    wT = w.T
    wn = jnp.concatenate([w[:, o_mq:o_mi], w[:, o_ga:o_end]], axis=1).astype(BF16)
    wt = jnp.concatenate([wT[o_fq:o_ff], wT[o_mi:o_fq], wT[o_ff:o_ga],
                          jnp.zeros((GATE_ROWS - N_GATES, D_MODEL), F32)], axis=0).astype(BF16)
    bias_col = jnp.concatenate([b_mi[0], b_mf[0], b_ff[0], jnp.zeros((GATE_ROWS - N_GATES,), F32)])[:, None]
    gain_col = jnp.concatenate([jnp.tile(g_qn[0], H_F) * (F_SCALE * LOG2E), jnp.tile(g_kn[0], H_F)])[:, None]
    return dict(
        g_mix=g_mix, wn=wn, wt=wt, bias_col=bias_col, gain_col=gain_col,
        gmh=g_mh[0].reshape(1, W_M),
        wpa=w_pa[0].astype(BF16), wpb=w_pb[0].astype(BF16), wout=w_out[0].astype(BF16),
        gffn=g_ffn, wgu=w_gu[0].astype(BF16), wdown=w_down[0].astype(BF16),
        gple=g_ple, wpg=w_ple_gate[0].astype(BF16), wpp=w_ple_proj[0].astype(BF16))


def _prompt_path(x, p, wd, tm_in, tm_ffn, tq, tk, fox):
    b, s, d = x.shape
    (mq, mk, mv, smo, sga, sgb, fqT, fkT, fvT, fk_tok, fvT16, gT) = _in_proj(
        x, wd["g_mix"], wd["wn"], wd["wt"], wd["bias_col"], wd["gain_col"], tm_in)
    cT, c_tok, e_tok = _gate_scan(gT)
    seq3 = lambda a: a.reshape(b, s, a.shape[-1])
    hm, caug, m_out = _mlstm_prompt(seq3(mq), seq3(mk), seq3(mv), seq3(smo), seq3(c_tok), cT, wd["gmh"])
    yfT = _fox_prompt(fqT, fk_tok, e_tok, fvT16, tq, tk)
    y, yf_sample = _merge_ffn(
        x.reshape(b * s, d), hm.reshape(b * s, W_M), yfT, sga, sgb, p.reshape(b * s, PLE_DIM),
        wd["wpa"], wd["wpb"], wd["wout"], wd["gffn"], wd["wgu"], wd["wdown"],
        wd["gple"], wd["wpg"], wd["wpp"], tm_ffn, fox)
    to_cache = lambda aT: aT.reshape(b, H_F, DH_F, s).transpose(0, 3, 1, 2)[None]
    logf = gT[:, 2 * H_M:N_GATES, :].transpose(0, 2, 1)[None]
    return (y.reshape(b, s, d), to_cache(fkT), to_cache(fvT), logf,
            caug[None, :, :, :, :DV_M], caug[None, :, :, :, DV_M], m_out[None, :, :H_M, 0]), yf_sample


L_PAD = 16
SEQ_BLOCK = 8


def _mlstm_sample_kernel(q_ref, k_ref, v_ref, smo_ref, grow_ref, gcol_ref, c_ref, n_ref, m_ref, gmh_ref,
                         hm_ref, co_ref, no_ref, mo_ref, *, n_valid):
    length = L_PAD
    row = lax.broadcasted_iota(jnp.int32, (length, length), 0)
    col = lax.broadcasted_iota(jnp.int32, (length, length), 1)
    valid = (col <= row) & (col < n_valid)
    chains = [(i, h) for i in range(SEQ_BLOCK) for h in range(H_M)]
    sl = lambda h: slice(h * DK_M, (h + 1) * DK_M)

    q = [q_ref[i, :, sl(h)] for i, h in chains]
    k = [k_ref[i, :, sl(h)] for i, h in chains]
    v = [v_ref[i, :, sl(h)] for i, h in chains]
    n_st = [n_ref[i, h:h + 1, :] for i, h in chains]
    s_mat = _each(_dot_nt, q, k)
    r2 = _each(lambda qq, ih: _dot(qq, c_ref[ih[0], ih[1]].astype(BF16)), q, chains)
    qn = _each(lambda qq, n: jnp.sum(qq.astype(F32) * n, axis=1, keepdims=True), q, n_st)

    lf_row = [grow_ref[i, H_M + h:H_M + h + 1, 0:length] for i, h in chains]
    lf_col = [gcol_ref[i, :, H_M + h:H_M + h + 1] for i, h in chains]
    b_col = _each(lambda r: jnp.sum(jnp.where(col <= row, r, 0.0), axis=1, keepdims=True), lf_row)
    b_row = _each(lambda c: jnp.sum(jnp.where(row <= col, c, 0.0), axis=0, keepdims=True), lf_col)
    w_intra, w_inter, floor, ws, decay, m_new = _mlstm_gates(
        [grow_ref[i, h:h + 1, 0:length] for i, h in chains], b_row,
        [gcol_ref[i, :, h:h + 1] for i, h in chains], b_col,
        [m_ref[i:i + 1, h:h + 1] for i, h in chains], valid, n_valid)
    sw = _each(lambda s, w: s * w, s_mat, w_intra)
    kw = _each(lambda kk, w: kk.astype(F32) * w, k, ws)

    r1 = _each(lambda s, vv: _dot(s.astype(BF16), vv), sw, v)
    upd = _each(lambda kk, vv: _dot_tn(kk.astype(BF16), vv), kw, v)
    sw_sum = _each(lambda s: jnp.sum(s, axis=1, keepdims=True), sw)
    kw_sum = _each(lambda kk: jnp.sum(kk, axis=0, keepdims=True), kw)

    num = _each(lambda w, a, c: w * a + c, w_inter, r2, r1)
    den = _each(lambda w, a, c: w * a + c, w_inter, qn, sw_sum)
    hh = _each(lambda n, d, f: n / jnp.maximum(jnp.abs(d), f), num, den, floor)
    scale = _each(_rms_scale, hh)
    for (i, h), x, sc, dc, u, n, ks, mn in zip(chains, hh, scale, decay, upd, n_st, kw_sum, m_new):
        hm_ref[i, :, sl(h)] = (x * sc * gmh_ref[:, sl(h)] * smo_ref[i, :, sl(h)].astype(F32)).astype(BF16)
        co_ref[i, h] = dc * c_ref[i, h] + u
        no_ref[i, h:h + 1, :] = dc * n + ks
        mo_ref[i:i + 1, h:h + 1] = mn


def _mlstm_sample(q, k, v, smo, grow, gcol, c_st, n_st, m_st, gmh, n_valid):
    db = q.shape[0]
    blk = lambda a: pl.BlockSpec((SEQ_BLOCK,) + a.shape[1:], lambda i: (i,) + (0,) * (a.ndim - 1))
    kern = functools.partial(_mlstm_sample_kernel, n_valid=n_valid)
    return pl.pallas_call(
        kern,
        grid=(db // SEQ_BLOCK,),
        in_specs=[blk(q), blk(k), blk(v), blk(smo), blk(grow), blk(gcol), blk(c_st), blk(n_st), blk(m_st),
                  pl.BlockSpec(gmh.shape, lambda i: (0, 0))],
        out_specs=(blk(q), blk(c_st), blk(n_st), blk(m_st)),
        out_shape=(jax.ShapeDtypeStruct(q.shape, BF16), jax.ShapeDtypeStruct(c_st.shape, F32),
                   jax.ShapeDtypeStruct(n_st.shape, F32), jax.ShapeDtypeStruct(m_st.shape, F32)),
        compiler_params=_params("parallel"),
        name="mlstm_sample",
    )(q, k, v, smo, grow, gcol, c_st, n_st, m_st, gmh)


def _run_with_hosted_fox(groups, step, fox, pt_ref, fq_ref, kn_ref, vn_ref, lfn_ref, ck_hbm, cv_hbm, cl_hbm,
                         yf_ref, kbuf, vbuf, lbuf, sem):
    per, total, n_pages, n_tok = fox
    first = step * per

    def page_copies(n, half):
        copies = []
        for j in range(n_pages):
            pid = pt_ref[n, j]
            copies.append(pltpu.make_async_copy(ck_hbm.at[pid], kbuf.at[half, j], sem.at[half]))
            copies.append(pltpu.make_async_copy(cv_hbm.at[pid], vbuf.at[half, j], sem.at[half]))
            copies.append(pltpu.make_async_copy(cl_hbm.at[pid], lbuf.at[half, j], sem.at[half]))
        return copies

    @pl.when(step == 0)
    def _():
        for c in page_copies(0, 0):
            c.start()

    bounds = [len(groups) * j // per for j in range(per + 1)]
    for j in range(per):
        n, half = first + j, j % 2
        mine = groups[bounds[j]:bounds[j + 1]]

        def prefetch():
            for c in page_copies(n + 1, 1 - half):
                c.start()

        if j + 1 < per:
            prefetch()
        else:
            pl.when(n + 1 < total)(prefetch)

        for c in page_copies(n, half):
            c.wait()
        scores = _fox_sample_scores(fq_ref[j], kn_ref[j], lfn_ref[j], lambda pg: kbuf[half, pg],
                                    lbuf[half].reshape(n_pages * H_F, LANES), n_pages, n_tok)
        for g in mine[:-1]:
            g()
        yf_ref[j] = _fox_sample_output(*scores, lambda pg: vbuf[half, pg], vn_ref[j], n_pages, n_tok)
        for g in mine[-1:]:
            g()


def _own_lanes():
    sub = lax.broadcasted_iota(jnp.int32, (H_F, W_F), 0)
    head_of_lane = lax.broadcasted_iota(jnp.int32, (H_F, W_F), 1) // DH_F
    return sub == head_of_lane


def _fox_sample_scores(fq, kn_new, lfn, k_page, lf_pages, n_pages, n_tok):
    rows = n_tok * H_F
    own = _own_lanes()
    tile = lambda a: jnp.concatenate([a] * n_tok, axis=0)
    qbd = jnp.concatenate(
        [jnp.where(own, jnp.broadcast_to(fq[t:t + 1, :], (H_F, W_F)), 0.0) for t in range(n_tok)], axis=0)
    qbd16 = qbd.astype(BF16)

    incl_all = _lane_cumsum(lf_pages)
    bias = [None] * n_pages
    carry = jnp.zeros((H_F, 1), F32)
    for pg in reversed(range(n_pages)):
        incl = incl_all[pg * H_F:(pg + 1) * H_F, :]
        tot = incl[:, LANES - 1:LANES]
        bias[pg] = (tot - incl + carry) * LOG2E
        carry = carry + tot

    logits = []
    for j in range(0, n_pages, 2):
        kcat = jnp.concatenate([k_page(j).astype(BF16), k_page(j + 1).astype(BF16)], axis=1)
        bcat = jnp.concatenate([tile(bias[j]), tile(bias[j + 1])], axis=1)
        logits.append(_dot(qbd16, kcat) + bcat)

    rowi = lax.broadcasted_iota(jnp.int32, (rows, 1), 0)
    qf = qbd16.astype(F32)
    new_logits = []
    f_new = jnp.zeros((H_F, 1), F32)
    for s in range(n_tok):
        f_new = f_new + lfn[:, s:s + 1]
        kn = kn_new[s:s + 1, :].astype(BF16).astype(F32)
        ln = jnp.sum(qf * kn, axis=1, keepdims=True) - tile(f_new) * LOG2E
        new_logits.append(jnp.where(rowi >= s * H_F, ln, NEG_INF))

    m = new_logits[0]
    for ln in new_logits[1:]:
        m = jnp.maximum(m, ln)
    for lg in logits:
        m = jnp.maximum(m, jnp.max(lg, axis=1, keepdims=True))
    return logits, new_logits, m


def _fox_sample_output(logits, new_logits, m, v_page, vn_new, n_pages, n_tok):
    rows = n_tok * H_F
    own = _own_lanes()
    denom = jnp.zeros((rows, 1), F32)
    acc = jnp.zeros((rows, W_F), F32)
    for idx, lg in enumerate(logits):
        p = jnp.exp2(lg - m)
        denom = denom + jnp.sum(p, axis=1, keepdims=True)
        j = 2 * idx
        vcat = jnp.concatenate([v_page(j).astype(BF16), v_page(j + 1).astype(BF16)], axis=1)
        acc = acc + _dot_nt(p.astype(BF16), vcat)
    for s, ln in enumerate(new_logits):
        p = jnp.exp2(ln - m)
        denom = denom + p
        acc = acc + p.astype(BF16).astype(F32) * vn_new[s:s + 1, :]
    out = acc / denom
    return jnp.concatenate(
        [jnp.sum(jnp.where(own, out[t * H_F:(t + 1) * H_F, :], 0.0), axis=0, keepdims=True)
         for t in range(n_tok)], axis=0)


def _sample_front(x, cache_k, cache_v, cache_logf, state_c, state_n, state_m, page_table, wd):
    db, n_tok, d = x.shape
    t = db * n_tok
    (mq, mk, mv, smo, sga, sgb, fqT, fkT, fvT, fk_tok, _, gT) = _in_proj(
        x.reshape(1, t, d), wd["g_mix"], wd["wn"], wd["wt"], wd["bias_col"], wd["gain_col"], t)

    pad_tok = lambda a: jnp.pad(a.reshape(db, n_tok, -1), ((0, 0), (0, L_PAD - n_tok), (0, 0)))
    g16 = gT[0, :2 * 8].reshape(2 * 8, db, n_tok)
    grow = jnp.pad(g16.transpose(1, 0, 2), ((0, 0), (0, 0), (0, LANES - n_tok)))
    gcol = jnp.pad(g16.transpose(1, 2, 0), ((0, 0), (0, L_PAD - n_tok), (0, LANES - 2 * 8)))
    hm, c_new, n_new, m_new = _mlstm_sample(
        pad_tok(mq), pad_tok(mk), pad_tok(mv), pad_tok(smo), grow, gcol,
        state_c, state_n, state_m, wd["gmh"], n_tok)
    hm = hm[:, :n_tok].reshape(t, W_M)

    pool, page = cache_k.shape[0], cache_k.shape[1]
    ck = cache_k.transpose(0, 2, 3, 1).reshape(pool, W_F, page)
    cv = cache_v.transpose(0, 2, 3, 1).reshape(pool, W_F, page)
    cl = cache_logf.transpose(0, 2, 1)
    tok3 = lambda aT: aT[0].T.astype(F32).reshape(db, n_tok, W_F)
    lfn = jnp.pad(gT[0, 2 * H_M:N_GATES].reshape(H_F, db, n_tok).transpose(1, 0, 2),
                  ((0, 0), (0, 0), (0, LANES - n_tok)))
    fox = (page_table, tok3(fqT), fk_tok.astype(F32).reshape(db, n_tok, W_F), tok3(fvT), lfn, ck, cv, cl)
    to_cache = lambda aT: aT[0].reshape(H_F, DH_F, db, n_tok).transpose(2, 3, 0, 1)[None]
    logf = gT[0, 2 * H_M:N_GATES].reshape(H_F, db, n_tok).transpose(1, 2, 0)[None]
    outs = (to_cache(fkT), to_cache(fvT), logf, c_new[None], n_new[None], m_new[None])
    return fox, (hm, sga, sgb), outs


def _sample_back(x, p, yf, hm, sga, sgb, wd):
    db, n_tok, d = x.shape
    t = db * n_tok
    yfT = yf.reshape(t, W_F).T.astype(BF16)[None]
    y = _merge_ffn(x.reshape(t, d), hm, yfT, sga, sgb, p.reshape(t, PLE_DIM),
                   wd["wpa"], wd["wpb"], wd["wout"], wd["gffn"], wd["wgu"], wd["wdown"],
                   wd["gple"], wd["wpg"], wd["wpp"], t)
    return y.reshape(db, n_tok, d)


def kernel(x_prompt, x_sample, p_prompt, p_sample, cache_k, cache_v, cache_logf, state_C, state_n, state_m,
           page_table, g_mix, w_in, b_mi, b_mf, b_ff, g_mh, g_qn, g_kn, w_pa, w_pb, w_out, g_ffn, w_gu,
           w_down, g_ple, w_ple_gate, w_ple_proj):
    wd = _prep_weights(g_mix, w_in, b_mi, b_mf, b_ff, g_mh, g_qn, g_kn, w_pa, w_pb, w_out, g_ffn, w_gu,
                       w_down, g_ple, w_ple_gate, w_ple_proj)
    fox, mid, (ks, vs, lfs, cs, ns, ms) = _sample_front(
        x_sample, cache_k[0], cache_v[0], cache_logf[0], state_C[0], state_n[0], state_m[0], page_table, wd)
    (yp, kp, vp, lfp, cp, np_, mp), yf_sample = _prompt_path(x_prompt, p_prompt[0], wd, 512, 512, 512, 256, fox)
    ys = _sample_back(x_sample, p_sample[0], yf_sample, *mid, wd)
    return (yp, ys, kp, vp, lfp, cp, np_, mp, ks, vs, lfs, cs, ns, ms)
```

```python
import functools

import jax
import jax.numpy as jnp
from jax import lax
from jax.experimental import pallas as pl
from jax.experimental.pallas import tpu as pltpu

D_MODEL = 1024
H_M = 4
DK_M = 128
DV_M = 128
W_M = H_M * DV_M
H_F = 8
DH_F = 64
W_F = H_F * DH_F
D_FF = 2816
PLE_DIM = 256
CHUNK = 128
EPS = 1e-6
F_SCALE = DH_F ** -0.5
K_SCALE_M = DK_M ** -0.5
LOG2E = 1.4426950408889634
N_GATES = 2 * H_M + H_F
GATE_ROWS = 128
N_SPLIT = 3
V_ROWS = DH_F + 16
MV_ROWS = DV_M + 16
LANES = 128
VMEM_LIMIT = 56 * 1024 * 1024
VMEM_LIMIT_HOST = 62 * 1024 * 1024

BF16 = jnp.bfloat16
F32 = jnp.float32
NEG_INF = float("-inf")


def _dot(a, b):
    return jnp.dot(a, b, preferred_element_type=F32)


def _dot_nt(a, b):
    return lax.dot_general(a, b, (((1,), (1,)), ((), ())), preferred_element_type=F32)


def _dot_tn(a, b):
    return lax.dot_general(a, b, (((0,), (0,)), ((), ())), preferred_element_type=F32)


def _log_sigmoid(u):
    return jnp.minimum(u, 0.0) - jnp.log1p(jnp.exp(-jnp.abs(u)))


def _rms_scale(x):
    return lax.rsqrt(jnp.mean(x * x, axis=-1, keepdims=True) + EPS)


def _params(*sem, vmem=VMEM_LIMIT):
    return pltpu.CompilerParams(dimension_semantics=sem, vmem_limit_bytes=vmem)


def _in_proj_kernel(x_ref, gmix_ref, wn_ref, wt_ref, bias_ref, gain_ref,
                    mq_ref, mk_ref, mvT_ref, smoT_ref, sga_ref, sgb_ref,
                    fqT_ref, fkT_ref, fvT_ref, fk_ref, fvT16_ref, gT_ref):
    x = x_ref[0]
    xn = (x * _rms_scale(x) * gmix_ref[...]).astype(BF16)

    def zn(lo, hi):
        return _dot(xn, wn_ref[:, lo:hi])

    def zt(lo, hi):
        return _dot_nt(wt_ref[lo:hi, :], xn)

    def head_norm(z, gain):
        tm = z.shape[-1]
        r = z.reshape(H_F, DH_F, tm)
        r = r * lax.rsqrt(jnp.mean(r * r, axis=1, keepdims=True) + EPS)
        return r.reshape(W_F, tm) * gain

    def g_mq():
        mq_ref[...] = zn(0, W_M).astype(BF16)

    def g_mk():
        mk_ref[...] = (zn(W_M, 2 * W_M) * K_SCALE_M).astype(BF16)

    def g_mv():
        vT = zt(3 * W_F, 3 * W_F + W_M)
        tm = vT.shape[-1]
        v_aug = jnp.concatenate([vT.reshape(H_M, DV_M, tm), jnp.ones((H_M, MV_ROWS - DV_M, tm), F32)], axis=1)
        mvT_ref[0] = v_aug.reshape(H_M * MV_ROWS, tm).astype(BF16)

    def g_smo():
        smoT_ref[0] = jax.nn.sigmoid(zn(2 * W_M, 3 * W_M)).T.astype(BF16)

    def g_gate(ref, off, c):
        def run():
            ref[:, c:c + W_M] = jax.nn.sigmoid(zn(off + c, off + c + W_M)).astype(BF16)
        return run

    def g_fq():
        fqT_ref[0] = head_norm(zt(0, W_F), gain_ref[0:W_F, :]).astype(BF16)

    def g_fk():
        fk = head_norm(zt(W_F, 2 * W_F), gain_ref[W_F:2 * W_F, :])
        fkT_ref[0] = fk
        fk_ref[...] = fk.T.astype(BF16)

    def g_fv():
        fv = zt(2 * W_F, 3 * W_F)
        fvT_ref[0] = fv
        tm = fv.shape[-1]
        fv_aug = jnp.concatenate([fv.reshape(H_F, DH_F, tm), jnp.ones((H_F, V_ROWS - DH_F, tm), F32)], axis=1)
        fvT16_ref[0] = fv_aug.reshape(H_F * V_ROWS, tm).astype(BF16)

    def g_gates():
        u = zt(3 * W_F + W_M, 3 * W_F + W_M + GATE_ROWS) + bias_ref[...]
        row = lax.broadcasted_iota(jnp.int32, u.shape, 0)
        gT_ref[0] = jnp.where(row < H_M, u, jnp.where(row < N_GATES, _log_sigmoid(u), 0.0))

    g_mq()
    g_mk()
    g_mv()
    g_smo()
    for ref, off in ((sga_ref, 3 * W_M), (sgb_ref, 3 * W_M + D_MODEL)):
        for c in range(0, D_MODEL, W_M):
            g_gate(ref, off, c)()
    g_fq()
    g_fk()
    g_fv()
    g_gates()


def _in_proj(x, g_mix, wn, wt, bias_col, gain_col, tm):
    b, s, d = x.shape
    nt = s // tm
    t = b * s
    tok = lambda w: pl.BlockSpec((tm, w), lambda bi, i: (bi * nt + i, 0))
    hm = lambda r: pl.BlockSpec((1, r, tm), lambda bi, i: (bi, 0, i))
    full = lambda a: pl.BlockSpec(a.shape, lambda bi, i: (0,) * a.ndim)
    out_shape = (
        jax.ShapeDtypeStruct((t, W_M), BF16),
        jax.ShapeDtypeStruct((t, W_M), BF16),
        jax.ShapeDtypeStruct((b, H_M * MV_ROWS, s), BF16),
        jax.ShapeDtypeStruct((b, W_M, s), BF16),
        jax.ShapeDtypeStruct((t, D_MODEL), BF16),
        jax.ShapeDtypeStruct((t, D_MODEL), BF16),
        jax.ShapeDtypeStruct((b, W_F, s), BF16),
        jax.ShapeDtypeStruct((b, W_F, s), F32),
        jax.ShapeDtypeStruct((b, W_F, s), F32),
        jax.ShapeDtypeStruct((t, W_F), BF16),
        jax.ShapeDtypeStruct((b, H_F * V_ROWS, s), BF16),
        jax.ShapeDtypeStruct((b, GATE_ROWS, s), F32),
    )
    out_specs = (tok(W_M), tok(W_M), hm(H_M * MV_ROWS), hm(W_M), tok(D_MODEL), tok(D_MODEL),
                 hm(W_F), hm(W_F), hm(W_F), tok(W_F), hm(H_F * V_ROWS), hm(GATE_ROWS))
    return pl.pallas_call(
        _in_proj_kernel,
        grid=(b, nt),
        in_specs=[pl.BlockSpec((1, tm, d), lambda bi, i: (bi, i, 0)),
                  full(g_mix), full(wn), full(wt), full(bias_col), full(gain_col)],
        out_specs=out_specs,
        out_shape=out_shape,
        compiler_params=_params("parallel", "parallel"),
        name="in_proj",
    )(x, g_mix, wn, wt, bias_col, gain_col)


def _lane_cumsum(x):
    lane = lax.broadcasted_iota(jnp.int32, x.shape, x.ndim - 1)
    sh = 1
    while sh < x.shape[-1]:
        x = x + jnp.where(lane >= sh, pltpu.roll(x, sh, x.ndim - 1), 0.0)
        sh *= 2
    return x


def _split_bf16(x):
    pieces = []
    for _ in range(N_SPLIT):
        p = x.astype(BF16).astype(F32)
        pieces.append(p)
        x = x - p
    return pieces


def _gate_scan_kernel(gT_ref, cT_ref, c_ref, e_ref):
    s = gT_ref.shape[-1]
    rows = 2 * 8
    row = lax.broadcasted_iota(jnp.int32, (rows, CHUNK), 0)
    pad = jnp.zeros((GATE_ROWS - rows, CHUNK), F32)
    epad = jnp.zeros((GATE_ROWS - N_SPLIT * H_F, CHUNK), F32)
    sls = [slice(c * CHUNK, (c + 1) * CHUNK) for c in range(s // CHUNK)]
    g = [gT_ref[0, 0:rows, sl] for sl in sls]
    lane = lax.broadcasted_iota(jnp.int32, (rows, CHUNK), 1)
    cs = g
    sh = 1
    while sh < CHUNK:
        cs = _each(lambda x: x + jnp.where(lane >= sh, pltpu.roll(x, sh, 1), 0.0), cs)
        sh *= 2
    carry = jnp.zeros((rows, 1), F32)
    outs = []
    for gc, csc in zip(g, cs):
        outs.append(jnp.where(row < H_M, gc, jnp.where(row < 2 * H_M, csc, csc + carry)))
        carry = carry + csc[:, CHUNK - 1:CHUNK]
    fulls = _each(lambda o: jnp.concatenate([o, pad], axis=0), outs)
    fullsT = _each(lambda f: f.T, fulls)
    pieces = _each(lambda o: jnp.concatenate(_split_bf16(o[2 * H_M:N_GATES, :] * LOG2E) + [epad], axis=0), outs)
    piecesT = _each(lambda p: p.T.astype(BF16), pieces)
    for sl, f, fT, pT in zip(sls, fulls, fullsT, piecesT):
        cT_ref[0, :, sl] = f
        c_ref[sl, :] = fT
        e_ref[sl, :] = pT


def _gate_scan(gT):
    b, r, s = gT.shape
    return pl.pallas_call(
        _gate_scan_kernel,
        grid=(b,),
        in_specs=[pl.BlockSpec((1, r, s), lambda bi: (bi, 0, 0))],
        out_specs=(pl.BlockSpec((1, r, s), lambda bi: (bi, 0, 0)),
                   pl.BlockSpec((s, r), lambda bi: (bi, 0)),
                   pl.BlockSpec((s, r), lambda bi: (bi, 0))),
        out_shape=(jax.ShapeDtypeStruct((b, r, s), F32), jax.ShapeDtypeStruct((b * s, r), F32),
                   jax.ShapeDtypeStruct((b * s, r), BF16)),
        compiler_params=_params("parallel"),
        name="gate_scan",
    )(gT)


def _each(f, *lists):
    return [f(*a) for a in zip(*lists)]


def _mlstm_gates(ig_row, b_row, ig_col, b_col, m_st, valid, n_valid):
    length = b_col[0].shape[0]
    dlog = _each(lambda bc, br, ir: jnp.where(valid, bc - br + ir, NEG_INF), b_col, b_row, ig_row)
    a_col = _each(lambda bc, m: bc + m, b_col, m_st)
    mx = _each(lambda d: jnp.max(d, axis=1, keepdims=True), dlog)
    m_t = _each(jnp.maximum, a_col, mx)
    w_intra = _each(lambda d, m: jnp.exp(d - m), dlog, m_t)
    w_inter = _each(lambda a, m: jnp.exp(a - m), a_col, m_t)
    floor = _each(lambda m: jnp.exp(-m), m_t)
    b_end = _each(lambda br: br[:, n_valid - 1:n_valid], b_row)
    a_end = _each(lambda be, m: be + m, b_end, m_st)
    wlog = _each(lambda be, bc, ic: be - bc + ic, b_end, b_col, ig_col)
    if n_valid < length:
        keep = lax.broadcasted_iota(jnp.int32, (length, 1), 0) < n_valid
        wlog = _each(lambda w: jnp.where(keep, w, NEG_INF), wlog)
    wmax = _each(lambda w: jnp.max(w, axis=0, keepdims=True), wlog)
    m_new = _each(jnp.maximum, a_end, wmax)
    ws = _each(lambda w, m: jnp.exp(w - m), wlog, m_new)
    decay = _each(lambda a, m: jnp.exp(a - m), a_end, m_new)
    return w_intra, w_inter, floor, ws, decay, m_new


def _mlstm_prompt_kernel(mq_ref, mk_ref, mvT_ref, smoT_ref, c_ref, cT_ref, gmh_ref,
                         hmT_ref, caugT_ref, m_ref):
    nb = mq_ref.shape[0]

    @pl.when(pl.program_id(0) == 0)
    def _():
        caugT_ref[...] = jnp.zeros_like(caugT_ref)
        m_ref[...] = jnp.zeros_like(m_ref)

    srow = lax.broadcasted_iota(jnp.int32, (CHUNK, CHUNK), 0)
    tcol = lax.broadcasted_iota(jnp.int32, (CHUNK, CHUNK), 1)
    valid = srow <= tcol
    chains = [(b, h) for b in range(nb) for h in range(H_M)]
    sl = lambda h: slice(h * DK_M, (h + 1) * DK_M)
    vsl = lambda h: slice(h * MV_ROWS, (h + 1) * MV_ROWS)

    q = [mq_ref[b, :, sl(h)] for b, h in chains]
    k = [mk_ref[b, :, sl(h)] for b, h in chains]
    vT = [mvT_ref[b, vsl(h), :] for b, h in chains]
    sT = _each(_dot_nt, k, q)
    r2T = _each(lambda qq, bh: _dot_nt(caugT_ref[bh[0], bh[1]].astype(BF16), qq), q, chains)

    ig_row = [cT_ref[b, h:h + 1, :] for b, h in chains]
    b_row = [cT_ref[b, H_M + h:H_M + h + 1, :] for b, h in chains]
    c_col = [c_ref[b, :, h:h + 1] - c_ref[b, :, H_M + h:H_M + h + 1] for b, h in chains]
    m_st = [m_ref[b, h:h + 1, 0:1] for b, h in chains]
    dlog = _each(lambda br, cc: jnp.where(valid, br + cc, NEG_INF), b_row, c_col)
    a_row = _each(lambda br, m: br + m, b_row, m_st)
    m_t = _each(lambda a, d: jnp.maximum(a, jnp.max(d, axis=0, keepdims=True)), a_row, dlog)
    sw = _each(lambda s, d, m: (s * jnp.exp(d - m)).astype(BF16), sT, dlog, m_t)
    w_inter = _each(lambda a, m: jnp.exp(a - m), a_row, m_t)
    floor = _each(lambda m: jnp.exp(-m), m_t)

    b_end = _each(lambda br: br[:, CHUNK - 1:CHUNK], b_row)
    a_end = _each(lambda be, m: be + m, b_end, m_st)
    wlog = _each(lambda be, br, ir: be - br + ir, b_end, b_row, ig_row)
    m_new = _each(lambda a, w: jnp.maximum(a, jnp.max(w, axis=1, keepdims=True)), a_end, wlog)
    ws = _each(lambda w, m: jnp.exp(w - m), wlog, m_new)
    decay = _each(lambda a, m: jnp.exp(a - m), a_end, m_new)
    vw = _each(lambda v, w: (v.astype(F32) * w).astype(BF16), vT, ws)

    r1T = _each(_dot, vT, sw)
    upd = _each(_dot, vw, k)

    num = _each(lambda w, a, c: w * a[:DV_M] + c[:DV_M], w_inter, r2T, r1T)
    den = _each(lambda w, a, c: w * a[DV_M:DV_M + 1] + c[DV_M:DV_M + 1], w_inter, r2T, r1T)
    hh = _each(lambda n, d, f: n / jnp.maximum(jnp.abs(d), f), num, den, floor)
    scale = _each(lambda x: lax.rsqrt(jnp.mean(x * x, axis=0, keepdims=True) + EPS), hh)
    gain = [jnp.broadcast_to(gmh_ref[sl(h), :], (DV_M, CHUNK)) for h in range(H_M)]
    for (b, h), x, sc, dc, u, mn in zip(chains, hh, scale, decay, upd, m_new):
        hmT_ref[b, sl(h), :] = (x * sc * gain[h] * smoT_ref[b, sl(h), :].astype(F32)).astype(BF16)
        caugT_ref[b, h] = dc * caugT_ref[b, h] + u
        m_ref[b, h:h + 1, :] = jnp.broadcast_to(mn, (1, LANES))


def _mlstm_prompt(mq, mk, mvT, smoT, c_tok, cT, gmh_col):
    b, s, _ = mq.shape
    tok = lambda w: pl.BlockSpec((b, CHUNK, w), lambda c: (0, c, 0))
    hmaj = lambda r: pl.BlockSpec((b, r, CHUNK), lambda c: (0, 0, c))
    whole = lambda shape: pl.BlockSpec(shape, lambda c: (0,) * len(shape))
    state = (b, H_M, MV_ROWS, DK_M)
    return pl.pallas_call(
        _mlstm_prompt_kernel,
        grid=(s // CHUNK,),
        in_specs=[tok(W_M), tok(W_M), hmaj(H_M * MV_ROWS), hmaj(W_M), tok(GATE_ROWS), hmaj(GATE_ROWS),
                  whole(gmh_col.shape)],
        out_specs=(hmaj(W_M), whole(state), whole((b, 8, LANES))),
        out_shape=(jax.ShapeDtypeStruct((b, W_M, s), BF16),
                   jax.ShapeDtypeStruct(state, F32),
                   jax.ShapeDtypeStruct((b, 8, LANES), F32)),
        compiler_params=_params("arbitrary"),
        name="mlstm_prompt",
    )(mq, mk, mvT, smoT, c_tok, cT, gmh_col)


HEAD_GROUP = 2
QT = 256


def _fox_prompt_kernel(fqT_ref, fk_ref, e_ref, vT_ref, o_ref, qa_sc, m_sc, acc_sc, *, tq, tk):
    qi = pl.program_id(1)
    ratio = tq // tk
    nqt = tq // QT
    rowq = lax.broadcasted_iota(jnp.int32, (2 * DH_F, QT), 0)
    srow = lax.broadcasted_iota(jnp.int32, (tk, QT), 0)
    tcol = lax.broadcasted_iota(jnp.int32, (tk, QT), 1)
    hsl = lambda h: slice(h * DH_F, (h + 1) * DH_F)
    psl = lambda h: slice((h // 2) * 2 * DH_F, (h // 2 + 1) * 2 * DH_F)
    qsl = lambda n: slice(n * QT, (n + 1) * QT)

    for h in range(H_F):
        lo = (h % 2) * DH_F
        sel = jnp.where((rowq % H_F == h) & (rowq < N_SPLIT * H_F), -1.0, 0.0).astype(BF16)
        for n in range(nqt):
            q2 = fqT_ref[0, psl(h), qsl(n)]
            qh = jnp.where((rowq >= lo) & (rowq < lo + DH_F), q2, 0)
            qa_sc[h, n] = jnp.concatenate([qh, sel], axis=0)
    m_sc[...] = jnp.full_like(m_sc, NEG_INF)
    acc_sc[...] = jnp.zeros_like(acc_sc)
    groups = [range(g0, g0 + HEAD_GROUP) for g0 in range(0, H_F, HEAD_GROUP)]

    def run(tiles):
        work = [(ti, heads) for ti in range(len(tiles)) for heads in groups]
        k0 = [pl.multiple_of(j * tk, tk) for j, _ in tiles]
        e = [e_ref[pl.ds(k, tk), :] for k in k0]
        halves = [range(nqt) if d is None else range(d // QT, nqt) for _, d in tiles]
        st = {}

        def scores(ti, heads):
            for h in heads:
                ka = jnp.concatenate([fk_ref[pl.ds(k0[ti], tk), psl(h)], e[ti]], axis=1)
                for n in halves[ti]:
                    st[ti, h, n] = _dot(ka, qa_sc[h, n])

        scores(*work[0])
        for wi, (ti, heads) in enumerate(work):
            if wi + 1 < len(work):
                scores(*work[wi + 1])
            units = [(h, n) for h in heads for n in halves[ti]]
            p, alpha = {}, {}
            for h, n in units:
                s_u = st.pop((ti, h, n))
                d = tiles[ti][1]
                if d is not None and d + tk > n * QT:
                    s_u = jnp.where(srow + d <= tcol + n * QT, s_u, NEG_INF)
                m_old = m_sc[h, n]
                m_new = jnp.maximum(m_old, jnp.max(s_u, axis=0, keepdims=True))
                alpha[h, n] = jnp.exp2(m_old - m_new)
                p[h, n] = jnp.exp2(s_u - m_new).astype(BF16)
                m_sc[h, n] = m_new
            for h, n in units:
                vT = vT_ref[0, h * V_ROWS:(h + 1) * V_ROWS, pl.ds(k0[ti], tk)]
                acc_sc[h, n] = alpha[h, n] * acc_sc[h, n] + _dot(vT, p[h, n])

    def body(jj, carry):
        run([(jj * ratio + r, None) for r in range(ratio)])
        return carry

    lax.fori_loop(0, qi, body, 0)
    run([(qi * ratio + r, r * tk) for r in range(ratio)])
    for h in range(H_F):
        for n in range(nqt):
            o_ref[0, hsl(h), qsl(n)] = (acc_sc[h, n, :DH_F, :] / acc_sc[h, n, DH_F:DH_F + 1, :]).astype(BF16)


def _fox_prompt(fqT, fk_tok, e_tok, fvT16, tq, tk):
    b, w, s = fqT.shape
    kern = functools.partial(_fox_prompt_kernel, tq=tq, tk=tk)
    return pl.pallas_call(
        kern,
        grid=(b, s // tq),
        in_specs=[pl.BlockSpec((1, w, tq), lambda bi, qi: (bi, 0, qi)),
                  pl.BlockSpec((s, w), lambda bi, qi: (bi, 0)),
                  pl.BlockSpec((s, GATE_ROWS), lambda bi, qi: (bi, 0)),
                  pl.BlockSpec((1, H_F * V_ROWS, s), lambda bi, qi: (bi, 0, 0))],
        out_specs=pl.BlockSpec((1, w, tq), lambda bi, qi: (bi, 0, qi)),
        out_shape=jax.ShapeDtypeStruct((b, w, s), BF16),
        scratch_shapes=[pltpu.VMEM((H_F, tq // QT, 2 * DH_F + GATE_ROWS, QT), BF16),
                        pltpu.VMEM((H_F, tq // QT, 1, QT), F32),
                        pltpu.VMEM((H_F, tq // QT, V_ROWS, QT), F32)],
        compiler_params=_params("parallel", "arbitrary"),
        name="fox_prompt",
    )(fqT, fk_tok, e_tok, fvT16)


FF_TILE = 256


def _merge_ffn_kernel(*refs, fox):
    refs = list(refs)
    pt_ref = refs.pop(0) if fox else None
    (x_ref, hmT_ref, yfT_ref, sga_ref, sgb_ref, p_ref, wpa_ref, wpb_ref, wout_ref, gffn_ref, wgu_ref,
     wdown_ref, gple_ref, wpg_ref, wpp_ref) = refs[:15]
    del refs[:15]
    fox_in = [refs.pop(0) for _ in range(7)] if fox else None
    y_ref = refs.pop(0)
    yf_ref = refs.pop(0) if fox else None
    hdn_sc = refs.pop(0)
    val = {}

    def g_merge():
        a = _dot_tn(hmT_ref[0], wpa_ref[...])
        bb = _dot_tn(yfT_ref[0], wpb_ref[...])
        u = sga_ref[...].astype(F32) * a + sgb_ref[...].astype(F32) * bb
        x1 = x_ref[...] + _dot(u.astype(BF16), wout_ref[...])
        val["x1"] = x1
        val["xn"] = (x1 * _rms_scale(x1) * gffn_ref[...]).astype(BF16)

    def g_ffn(j):
        def run():
            g = _dot(val["xn"], wgu_ref[:, j:j + FF_TILE])
            up = _dot(val["xn"], wgu_ref[:, D_FF + j:D_FF + j + FF_TILE])
            hdn_sc[:, j:j + FF_TILE] = (jax.nn.silu(g) * up).astype(BF16)
        return run

    def g_out():
        x2 = val["x1"] + _dot(hdn_sc[...], wdown_ref[...])
        xg = (x2 * _rms_scale(x2) * gple_ref[...]).astype(BF16)
        gate = jax.nn.sigmoid(_dot(xg, wpg_ref[...]))
        y_ref[...] = x2 + gate * _dot(p_ref[...].astype(BF16), wpp_ref[...])

    groups = [g_merge] + [g_ffn(j) for j in range(0, D_FF, FF_TILE)] + [g_out]
    if fox:
        _run_with_hosted_fox(groups, pl.program_id(0), fox, pt_ref, *fox_in, yf_ref, *refs)
    else:
        for g in groups:
            g()


def _merge_ffn(x, hmT, yfT, sga, sgb, p, wpa, wpb, wout, gffn, wgu, wdown, gple, wpg, wpp, tm, fox=None):
    t, d = x.shape
    b, _, s = yfT.shape
    nt = s // tm
    steps = t // tm
    tok = lambda w: pl.BlockSpec((tm, w), lambda i, *_: (i, 0))
    full = lambda a: pl.BlockSpec(a.shape, lambda i, *_: (0,) * a.ndim)
    hmaj = lambda r: pl.BlockSpec((1, r, tm), lambda i, *_: (i // nt, 0, i % nt))
    in_specs = [tok(d), hmaj(W_M), hmaj(W_F), tok(d), tok(d), tok(PLE_DIM),
                full(wpa), full(wpb), full(wout), full(gffn), full(wgu), full(wdown),
                full(gple), full(wpg), full(wpp)]
    operands = (x, hmT, yfT, sga, sgb, p, wpa, wpb, wout, gffn, wgu, wdown, gple, wpg, wpp)
    hdn = pltpu.VMEM((tm, D_FF), BF16)
    if fox is None:
        return pl.pallas_call(
            functools.partial(_merge_ffn_kernel, fox=None),
            grid=(steps,),
            in_specs=in_specs,
            out_specs=tok(d),
            out_shape=jax.ShapeDtypeStruct((t, d), F32),
            scratch_shapes=[hdn],
            compiler_params=_params("parallel"),
            name="merge_ffn",
        )(*operands)

    page_table, fq, kn, vn, lfn, ck, cv, cl = fox
    db, n_tok, _ = fq.shape
    n_pages, page = page_table.shape[1], ck.shape[-1]
    per = db // steps
    assert per * steps == db and per % 2 == 0
    seqs = lambda a: pl.BlockSpec((per,) + a.shape[1:], lambda i, *_: (i,) + (0,) * (a.ndim - 1))
    hbm = pl.BlockSpec(memory_space=pl.ANY)
    return pl.pallas_call(
        functools.partial(_merge_ffn_kernel, fox=(per, db, n_pages, n_tok)),
        grid_spec=pltpu.PrefetchScalarGridSpec(
            num_scalar_prefetch=1, grid=(steps,),
            in_specs=in_specs + [seqs(fq), seqs(kn), seqs(vn), seqs(lfn), hbm, hbm, hbm],
            out_specs=(tok(d), seqs(fq)),
            scratch_shapes=[hdn,
                            pltpu.VMEM((2, n_pages, W_F, page), F32),
                            pltpu.VMEM((2, n_pages, W_F, page), F32),
                            pltpu.VMEM((2, n_pages, H_F, page), F32),
                            pltpu.SemaphoreType.DMA((2,))]),
        out_shape=(jax.ShapeDtypeStruct((t, d), F32), jax.ShapeDtypeStruct(fq.shape, F32)),
        compiler_params=_params("arbitrary", vmem=VMEM_LIMIT_HOST),
        name="merge_ffn_fox_sample",
    )(page_table, *operands, fq, kn, vn, lfn, ck, cv, cl)


_SPLIT = (W_M, W_M, W_M, W_M, H_M, H_M, W_F, W_F, W_F, H_F, D_MODEL, D_MODEL)


def _prep_weights(g_mix, w_in, b_mi, b_mf, b_ff, g_mh, g_qn, g_kn, w_pa, w_pb, w_out, g_ffn, w_gu,
                  w_down, g_ple, w_ple_gate, w_ple_proj):
    offs = [0]
    for sz in _SPLIT:
        offs.append(offs[-1] + sz)
    o_mq, _, o_mv, o_mo, o_mi, o_mf, o_fq, _, _, o_ff, o_ga, _, o_end = offs
    w = w_in[0]
    wT = w.T
    wn = jnp.concatenate([w[:, o_mq:o_mv], w[:, o_mo:o_mi], w[:, o_ga:o_end]], axis=1).astype(BF16)
    wt = jnp.concatenate([wT[o_fq:o_ff], wT[o_mv:o_mo], wT[o_mi:o_fq], wT[o_ff:o_ga],
                          jnp.zeros((GATE_ROWS - N_GATES, D_MODEL), F32)], axis=0).astype(BF16)
    bias_col = jnp.concatenate([b_mi[0], b_mf[0], b_ff[0], jnp.zeros((GATE_ROWS - N_GATES,), F32)])[:, None]
    gain_col = jnp.concatenate([jnp.tile(g_qn[0], H_F) * (F_SCALE * LOG2E), jnp.tile(g_kn[0], H_F)])[:, None]
    return dict(
        g_mix=g_mix, wn=wn, wt=wt, bias_col=bias_col, gain_col=gain_col,
        gmh=g_mh[0].reshape(1, W_M),
        wpa=w_pa[0].astype(BF16), wpb=w_pb[0].astype(BF16), wout=w_out[0].astype(BF16),
        gffn=g_ffn, wgu=w_gu[0].astype(BF16), wdown=w_down[0].astype(BF16),
        gple=g_ple, wpg=w_ple_gate[0].astype(BF16), wpp=w_ple_proj[0].astype(BF16))


def _prompt_path(x, p, wd, tm_in, tm_ffn, tq, tk, fox):
    b, s, d = x.shape
    (mq, mk, mvT, smoT, sga, sgb, fqT, fkT, fvT, fk_tok, fvT16, gT) = _in_proj(
        x, wd["g_mix"], wd["wn"], wd["wt"], wd["bias_col"], wd["gain_col"], tm_in)
    cT, c_tok, e_tok = _gate_scan(gT)
    seq3 = lambda a: a.reshape(b, s, a.shape[-1])
    hmT, caugT, m_out = _mlstm_prompt(seq3(mq), seq3(mk), mvT, smoT, seq3(c_tok), cT, wd["gmh"].reshape(W_M, 1))
    yfT = _fox_prompt(fqT, fk_tok, e_tok, fvT16, tq, tk)
    y, yf_sample = _merge_ffn(
        x.reshape(b * s, d), hmT, yfT, sga, sgb, p.reshape(b * s, PLE_DIM),
        wd["wpa"], wd["wpb"], wd["wout"], wd["gffn"], wd["wgu"], wd["wdown"],
        wd["gple"], wd["wpg"], wd["wpp"], tm_ffn, fox)
    to_cache = lambda aT: aT.reshape(b, H_F, DH_F, s).transpose(0, 3, 1, 2)[None]
    logf = gT[:, 2 * H_M:N_GATES, :].transpose(0, 2, 1)[None]
    c_out = caugT[:, :, :DV_M, :].transpose(0, 1, 3, 2)[None]
    return (y.reshape(b, s, d), to_cache(fkT), to_cache(fvT), logf,
            c_out, caugT[None, :, :, DV_M, :], m_out[None, :, :H_M, 0]), yf_sample


L_PAD = 16
SEQ_BLOCK = 8


def _mlstm_sample_kernel(q_ref, k_ref, v_ref, smo_ref, grow_ref, gcol_ref, c_ref, n_ref, m_ref, gmh_ref,
                         hm_ref, co_ref, no_ref, mo_ref, *, n_valid):
    length = L_PAD
    row = lax.broadcasted_iota(jnp.int32, (length, length), 0)
    col = lax.broadcasted_iota(jnp.int32, (length, length), 1)
    valid = (col <= row) & (col < n_valid)
    chains = [(i, h) for i in range(SEQ_BLOCK) for h in range(H_M)]
    sl = lambda h: slice(h * DK_M, (h + 1) * DK_M)

    q = [q_ref[i, :, sl(h)] for i, h in chains]
    k = [k_ref[i, :, sl(h)] for i, h in chains]
    v = [v_ref[i, :, sl(h)] for i, h in chains]
    n_st = [n_ref[i, h:h + 1, :] for i, h in chains]
    s_mat = _each(_dot_nt, q, k)
    r2 = _each(lambda qq, ih: _dot(qq, c_ref[ih[0], ih[1]].astype(BF16)), q, chains)
    qn = _each(lambda qq, n: jnp.sum(qq.astype(F32) * n, axis=1, keepdims=True), q, n_st)

    lf_row = [grow_ref[i, H_M + h:H_M + h + 1, 0:length] for i, h in chains]
    lf_col = [gcol_ref[i, :, H_M + h:H_M + h + 1] for i, h in chains]
    b_col = _each(lambda r: jnp.sum(jnp.where(col <= row, r, 0.0), axis=1, keepdims=True), lf_row)
    b_row = _each(lambda c: jnp.sum(jnp.where(row <= col, c, 0.0), axis=0, keepdims=True), lf_col)
    w_intra, w_inter, floor, ws, decay, m_new = _mlstm_gates(
        [grow_ref[i, h:h + 1, 0:length] for i, h in chains], b_row,
        [gcol_ref[i, :, h:h + 1] for i, h in chains], b_col,
        [m_ref[i:i + 1, h:h + 1] for i, h in chains], valid, n_valid)
    sw = _each(lambda s, w: s * w, s_mat, w_intra)
    kw = _each(lambda kk, w: kk.astype(F32) * w, k, ws)

    r1 = _each(lambda s, vv: _dot(s.astype(BF16), vv), sw, v)
    upd = _each(lambda kk, vv: _dot_tn(kk.astype(BF16), vv), kw, v)
    sw_sum = _each(lambda s: jnp.sum(s, axis=1, keepdims=True), sw)
    kw_sum = _each(lambda kk: jnp.sum(kk, axis=0, keepdims=True), kw)

    num = _each(lambda w, a, c: w * a + c, w_inter, r2, r1)
    den = _each(lambda w, a, c: w * a + c, w_inter, qn, sw_sum)
    hh = _each(lambda n, d, f: n / jnp.maximum(jnp.abs(d), f), num, den, floor)
    scale = _each(_rms_scale, hh)
    for (i, h), x, sc, dc, u, n, ks, mn in zip(chains, hh, scale, decay, upd, n_st, kw_sum, m_new):
        hm_ref[i, :, sl(h)] = (x * sc * gmh_ref[:, sl(h)] * smo_ref[i, :, sl(h)].astype(F32)).astype(BF16)
        co_ref[i, h] = dc * c_ref[i, h] + u
        no_ref[i, h:h + 1, :] = dc * n + ks
        mo_ref[i:i + 1, h:h + 1] = mn


def _mlstm_sample(q, k, v, smo, grow, gcol, c_st, n_st, m_st, gmh, n_valid):
    db = q.shape[0]
    blk = lambda a: pl.BlockSpec((SEQ_BLOCK,) + a.shape[1:], lambda i: (i,) + (0,) * (a.ndim - 1))
    kern = functools.partial(_mlstm_sample_kernel, n_valid=n_valid)
    return pl.pallas_call(
        kern,
        grid=(db // SEQ_BLOCK,),
        in_specs=[blk(q), blk(k), blk(v), blk(smo), blk(grow), blk(gcol), blk(c_st), blk(n_st), blk(m_st),
                  pl.BlockSpec(gmh.shape, lambda i: (0, 0))],
        out_specs=(blk(q), blk(c_st), blk(n_st), blk(m_st)),
        out_shape=(jax.ShapeDtypeStruct(q.shape, BF16), jax.ShapeDtypeStruct(c_st.shape, F32),
                   jax.ShapeDtypeStruct(n_st.shape, F32), jax.ShapeDtypeStruct(m_st.shape, F32)),
        compiler_params=_params("parallel"),
        name="mlstm_sample",
    )(q, k, v, smo, grow, gcol, c_st, n_st, m_st, gmh)


def _run_with_hosted_fox(groups, step, fox, pt_ref, fq_ref, kn_ref, vn_ref, lfn_ref, ck_hbm, cv_hbm, cl_hbm,
                         yf_ref, kbuf, vbuf, lbuf, sem):
    per, total, n_pages, n_tok = fox
    first = step * per

    def page_copies(n, half):
        copies = []
        for j in range(n_pages):
            pid = pt_ref[n, j]
            copies.append(pltpu.make_async_copy(ck_hbm.at[pid], kbuf.at[half, j], sem.at[half]))
            copies.append(pltpu.make_async_copy(cv_hbm.at[pid], vbuf.at[half, j], sem.at[half]))
            copies.append(pltpu.make_async_copy(cl_hbm.at[pid], lbuf.at[half, j], sem.at[half]))
        return copies

    @pl.when(step == 0)
    def _():
        for c in page_copies(0, 0):
            c.start()

    bounds = [len(groups) * j // per for j in range(per + 1)]
    for j in range(per):
        n, half = first + j, j % 2
        mine = groups[bounds[j]:bounds[j + 1]]

        def prefetch():
            for c in page_copies(n + 1, 1 - half):
                c.start()

        if j + 1 < per:
            prefetch()
        else:
            pl.when(n + 1 < total)(prefetch)

        for c in page_copies(n, half):
            c.wait()
        scores = _fox_sample_scores(fq_ref[j], kn_ref[j], lfn_ref[j], lambda pg: kbuf[half, pg],
                                    lbuf[half].reshape(n_pages * H_F, LANES), n_pages, n_tok)
        for g in mine[:-1]:
            g()
        yf_ref[j] = _fox_sample_output(*scores, lambda pg: vbuf[half, pg], vn_ref[j], n_pages, n_tok)
        for g in mine[-1:]:
            g()


def _own_lanes():
    sub = lax.broadcasted_iota(jnp.int32, (H_F, W_F), 0)
    head_of_lane = lax.broadcasted_iota(jnp.int32, (H_F, W_F), 1) // DH_F
    return sub == head_of_lane


def _fox_sample_scores(fq, kn_new, lfn, k_page, lf_pages, n_pages, n_tok):
    rows = n_tok * H_F
    own = _own_lanes()
    tile = lambda a: jnp.concatenate([a] * n_tok, axis=0)
    qbd = jnp.concatenate(
        [jnp.where(own, jnp.broadcast_to(fq[t:t + 1, :], (H_F, W_F)), 0.0) for t in range(n_tok)], axis=0)
    qbd16 = qbd.astype(BF16)

    incl_all = _lane_cumsum(lf_pages)
    bias = [None] * n_pages
    carry = jnp.zeros((H_F, 1), F32)
    for pg in reversed(range(n_pages)):
        incl = incl_all[pg * H_F:(pg + 1) * H_F, :]
        tot = incl[:, LANES - 1:LANES]
        bias[pg] = (tot - incl + carry) * LOG2E
        carry = carry + tot

    logits = []
    for j in range(0, n_pages, 2):
        kcat = jnp.concatenate([k_page(j).astype(BF16), k_page(j + 1).astype(BF16)], axis=1)
        bcat = jnp.concatenate([tile(bias[j]), tile(bias[j + 1])], axis=1)
        logits.append(_dot(qbd16, kcat) + bcat)

    rowi = lax.broadcasted_iota(jnp.int32, (rows, 1), 0)
    qf = qbd16.astype(F32)
    new_logits = []
    f_new = jnp.zeros((H_F, 1), F32)
    for s in range(n_tok):
        f_new = f_new + lfn[:, s:s + 1]
        kn = kn_new[s:s + 1, :].astype(BF16).astype(F32)
        ln = jnp.sum(qf * kn, axis=1, keepdims=True) - tile(f_new) * LOG2E
        new_logits.append(jnp.where(rowi >= s * H_F, ln, NEG_INF))

    m = new_logits[0]
    for ln in new_logits[1:]:
        m = jnp.maximum(m, ln)
    for lg in logits:
        m = jnp.maximum(m, jnp.max(lg, axis=1, keepdims=True))
    return logits, new_logits, m


def _fox_sample_output(logits, new_logits, m, v_page, vn_new, n_pages, n_tok):
    rows = n_tok * H_F
    own = _own_lanes()
    denom = jnp.zeros((rows, 1), F32)
    acc = jnp.zeros((rows, W_F), F32)
    for idx, lg in enumerate(logits):
        p = jnp.exp2(lg - m)
        denom = denom + jnp.sum(p, axis=1, keepdims=True)
        j = 2 * idx
        vcat = jnp.concatenate([v_page(j).astype(BF16), v_page(j + 1).astype(BF16)], axis=1)
        acc = acc + _dot_nt(p.astype(BF16), vcat)
    for s, ln in enumerate(new_logits):
        p = jnp.exp2(ln - m)
        denom = denom + p
        acc = acc + p.astype(BF16).astype(F32) * vn_new[s:s + 1, :]
    out = acc / denom
    return jnp.concatenate(
        [jnp.sum(jnp.where(own, out[t * H_F:(t + 1) * H_F, :], 0.0), axis=0, keepdims=True)
         for t in range(n_tok)], axis=0)


def _sample_front(x, cache_k, cache_v, cache_logf, state_c, state_n, state_m, page_table, wd):
    db, n_tok, d = x.shape
    t = db * n_tok
    (mq, mk, mvT, smoT, sga, sgb, fqT, fkT, fvT, fk_tok, _, gT) = _in_proj(
        x.reshape(1, t, d), wd["g_mix"], wd["wn"], wd["wt"], wd["bias_col"], wd["gain_col"], t)
    mv = mvT[0].reshape(H_M, MV_ROWS, t)[:, :DV_M].reshape(W_M, t).T
    smo = smoT[0].T

    pad_tok = lambda a: jnp.pad(a.reshape(db, n_tok, -1), ((0, 0), (0, L_PAD - n_tok), (0, 0)))
    g16 = gT[0, :2 * 8].reshape(2 * 8, db, n_tok)
    grow = jnp.pad(g16.transpose(1, 0, 2), ((0, 0), (0, 0), (0, LANES - n_tok)))
    gcol = jnp.pad(g16.transpose(1, 2, 0), ((0, 0), (0, L_PAD - n_tok), (0, LANES - 2 * 8)))
    hm, c_new, n_new, m_new = _mlstm_sample(
        pad_tok(mq), pad_tok(mk), pad_tok(mv), pad_tok(smo), grow, gcol,
        state_c, state_n, state_m, wd["gmh"], n_tok)
    hmT = hm[:, :n_tok].reshape(t, W_M).T[None]

    pool, page = cache_k.shape[0], cache_k.shape[1]
    ck = cache_k.transpose(0, 2, 3, 1).reshape(pool, W_F, page)
    cv = cache_v.transpose(0, 2, 3, 1).reshape(pool, W_F, page)
    cl = cache_logf.transpose(0, 2, 1)
    tok3 = lambda aT: aT[0].T.astype(F32).reshape(db, n_tok, W_F)
    lfn = jnp.pad(gT[0, 2 * H_M:N_GATES].reshape(H_F, db, n_tok).transpose(1, 0, 2),
                  ((0, 0), (0, 0), (0, LANES - n_tok)))
    fox = (page_table, tok3(fqT), fk_tok.astype(F32).reshape(db, n_tok, W_F), tok3(fvT), lfn, ck, cv, cl)
    to_cache = lambda aT: aT[0].reshape(H_F, DH_F, db, n_tok).transpose(2, 3, 0, 1)[None]
    logf = gT[0, 2 * H_M:N_GATES].reshape(H_F, db, n_tok).transpose(1, 2, 0)[None]
    outs = (to_cache(fkT), to_cache(fvT), logf, c_new[None], n_new[None], m_new[None])
    return fox, (hmT, sga, sgb), outs


def _sample_back(x, p, yf, hmT, sga, sgb, wd):
    db, n_tok, d = x.shape
    t = db * n_tok
    yfT = yf.reshape(t, W_F).T.astype(BF16)[None]
    y = _merge_ffn(x.reshape(t, d), hmT, yfT, sga, sgb, p.reshape(t, PLE_DIM),
                   wd["wpa"], wd["wpb"], wd["wout"], wd["gffn"], wd["wgu"], wd["wdown"],
                   wd["gple"], wd["wpg"], wd["wpp"], t)
    return y.reshape(db, n_tok, d)


def kernel(x_prompt, x_sample, p_prompt, p_sample, cache_k, cache_v, cache_logf, state_C, state_n, state_m,
           page_table, g_mix, w_in, b_mi, b_mf, b_ff, g_mh, g_qn, g_kn, w_pa, w_pb, w_out, g_ffn, w_gu,
           w_down, g_ple, w_ple_gate, w_ple_proj):
    wd = _prep_weights(g_mix, w_in, b_mi, b_mf, b_ff, g_mh, g_qn, g_kn, w_pa, w_pb, w_out, g_ffn, w_gu,
                       w_down, g_ple, w_ple_gate, w_ple_proj)
    fox, mid, (ks, vs, lfs, cs, ns, ms) = _sample_front(
        x_sample, cache_k[0], cache_v[0], cache_logf[0], state_C[0], state_n[0], state_m[0], page_table, wd)
    (yp, kp, vp, lfp, cp, np_, mp), yf_sample = _prompt_path(x_prompt, p_prompt[0], wd, 512, 512, 512, 256, fox)
    ys = _sample_back(x_sample, p_sample[0], yf_sample, *mid, wd)
    return (yp, ys, kp, vp, lfp, cp, np_, mp, ks, vs, lfs, cs, ns, ms)
```

```python
import functools

import jax
import jax.numpy as jnp
from jax import lax
from jax.experimental import pallas as pl
from jax.experimental.pallas import tpu as pltpu

D_MODEL = 1024
H_M = 4
DK_M = 128
DV_M = 128
W_M = H_M * DV_M
H_F = 8
DH_F = 64
W_F = H_F * DH_F
D_FF = 2816
PLE_DIM = 256
CHUNK = 128
EPS = 1e-6
F_SCALE = DH_F ** -0.5
K_SCALE_M = DK_M ** -0.5
LOG2E = 1.4426950408889634
N_GATES = 2 * H_M + H_F
GATE_ROWS = 128
N_SPLIT = 3
V_ROWS = DH_F + 16
MV_ROWS = DV_M + 16
LANES = 128
MXU_TILE = 256
TOKEN_TILE = 2 * MXU_TILE
QUERY_TILE = 2 * MXU_TILE
KEY_TILE = MXU_TILE
VMEM_LIMIT = 56 * 1024 * 1024
VMEM_LIMIT_HOST = 62 * 1024 * 1024

BF16 = jnp.bfloat16
F32 = jnp.float32
NEG_INF = float("-inf")


def _dot(a, b):
    return jnp.dot(a, b, preferred_element_type=F32)


def _dot_nt(a, b):
    return lax.dot_general(a, b, (((1,), (1,)), ((), ())), preferred_element_type=F32)


def _dot_tn(a, b):
    return lax.dot_general(a, b, (((0,), (0,)), ((), ())), preferred_element_type=F32)


def _log_sigmoid(u):
    return jnp.minimum(u, 0.0) - jnp.log1p(jnp.exp(-jnp.abs(u)))


def _rms_scale(x):
    return lax.rsqrt(jnp.mean(x * x, axis=-1, keepdims=True) + EPS)


def _params(*sem, vmem=VMEM_LIMIT):
    return pltpu.CompilerParams(dimension_semantics=sem, vmem_limit_bytes=vmem)


def _in_proj_kernel(x_ref, gmix_ref, wn_ref, wt_ref, bias_ref, gain_ref,
                    mq_ref, mk_ref, mvT_ref, smoT_ref, sga_ref, sgb_ref,
                    fqT_ref, fkT_ref, fvT_ref, fk_ref, fvT16_ref, gT_ref):
    x = x_ref[0]
    xn = (x * _rms_scale(x) * gmix_ref[...]).astype(BF16)

    def zn(lo, hi):
        return _dot(xn, wn_ref[:, lo:hi])

    def zt(lo, hi):
        return _dot_nt(wt_ref[lo:hi, :], xn)

    def head_norm(z, gain):
        tm = z.shape[-1]
        r = z.reshape(H_F, DH_F, tm)
        r = r * lax.rsqrt(jnp.mean(r * r, axis=1, keepdims=True) + EPS)
        return r.reshape(W_F, tm) * gain

    def g_mq():
        mq_ref[...] = zn(0, W_M).astype(BF16)

    def g_mk():
        mk_ref[...] = (zn(W_M, 2 * W_M) * K_SCALE_M).astype(BF16)

    def g_mv():
        vT = zt(3 * W_F, 3 * W_F + W_M)
        tm = vT.shape[-1]
        v_aug = jnp.concatenate([vT.reshape(H_M, DV_M, tm), jnp.ones((H_M, MV_ROWS - DV_M, tm), F32)], axis=1)
        mvT_ref[0] = v_aug.reshape(H_M * MV_ROWS, tm).astype(BF16)

    def g_smo():
        smoT_ref[0] = jax.nn.sigmoid(zn(2 * W_M, 3 * W_M)).T.astype(BF16)

    def g_gate(ref, off, c):
        def run():
            ref[:, c:c + W_M] = jax.nn.sigmoid(zn(off + c, off + c + W_M)).astype(BF16)
        return run

    def g_fq():
        fqT_ref[0] = head_norm(zt(0, W_F), gain_ref[0:W_F, :]).astype(BF16)

    def g_fk():
        fk = head_norm(zt(W_F, 2 * W_F), gain_ref[W_F:2 * W_F, :])
        fkT_ref[0] = fk
        fk_ref[...] = fk.T.astype(BF16)

    def g_fv():
        fv = zt(2 * W_F, 3 * W_F)
        fvT_ref[0] = fv
        tm = fv.shape[-1]
        fv_aug = jnp.concatenate([fv.reshape(H_F, DH_F, tm), jnp.ones((H_F, V_ROWS - DH_F, tm), F32)], axis=1)
        fvT16_ref[0] = fv_aug.reshape(H_F * V_ROWS, tm).astype(BF16)

    def g_gates():
        u = zt(3 * W_F + W_M, 3 * W_F + W_M + GATE_ROWS) + bias_ref[...]
        row = lax.broadcasted_iota(jnp.int32, u.shape, 0)
        gT_ref[0] = jnp.where(row < H_M, u, jnp.where(row < N_GATES, _log_sigmoid(u), 0.0))

    g_mq()
    g_mk()
    g_mv()
    g_smo()
    for ref, off in ((sga_ref, 3 * W_M), (sgb_ref, 3 * W_M + D_MODEL)):
        for c in range(0, D_MODEL, W_M):
            g_gate(ref, off, c)()
    g_fq()
    g_fk()
    g_fv()
    g_gates()


def _in_proj(x, g_mix, wn, wt, bias_col, gain_col, tm):
    b, s, d = x.shape
    nt = s // tm
    t = b * s
    tok = lambda w: pl.BlockSpec((tm, w), lambda bi, i: (bi * nt + i, 0))
    hm = lambda r: pl.BlockSpec((1, r, tm), lambda bi, i: (bi, 0, i))
    full = lambda a: pl.BlockSpec(a.shape, lambda bi, i: (0,) * a.ndim)
    out_shape = (
        jax.ShapeDtypeStruct((t, W_M), BF16),
        jax.ShapeDtypeStruct((t, W_M), BF16),
        jax.ShapeDtypeStruct((b, H_M * MV_ROWS, s), BF16),
        jax.ShapeDtypeStruct((b, W_M, s), BF16),
        jax.ShapeDtypeStruct((t, D_MODEL), BF16),
        jax.ShapeDtypeStruct((t, D_MODEL), BF16),
        jax.ShapeDtypeStruct((b, W_F, s), BF16),
        jax.ShapeDtypeStruct((b, W_F, s), F32),
        jax.ShapeDtypeStruct((b, W_F, s), F32),
        jax.ShapeDtypeStruct((t, W_F), BF16),
        jax.ShapeDtypeStruct((b, H_F * V_ROWS, s), BF16),
        jax.ShapeDtypeStruct((b, GATE_ROWS, s), F32),
    )
    out_specs = (tok(W_M), tok(W_M), hm(H_M * MV_ROWS), hm(W_M), tok(D_MODEL), tok(D_MODEL),
                 hm(W_F), hm(W_F), hm(W_F), tok(W_F), hm(H_F * V_ROWS), hm(GATE_ROWS))
    return pl.pallas_call(
        _in_proj_kernel,
        grid=(b, nt),
        in_specs=[pl.BlockSpec((1, tm, d), lambda bi, i: (bi, i, 0)),
                  full(g_mix), full(wn), full(wt), full(bias_col), full(gain_col)],
        out_specs=out_specs,
        out_shape=out_shape,
        compiler_params=_params("parallel", "parallel"),
        name="in_proj",
    )(x, g_mix, wn, wt, bias_col, gain_col)


def _lane_cumsum(x):
    lane = lax.broadcasted_iota(jnp.int32, x.shape, x.ndim - 1)
    sh = 1
    while sh < x.shape[-1]:
        x = x + jnp.where(lane >= sh, pltpu.roll(x, sh, x.ndim - 1), 0.0)
        sh *= 2
    return x


def _split_bf16(x):
    pieces = []
    for _ in range(N_SPLIT):
        p = x.astype(BF16).astype(F32)
        pieces.append(p)
        x = x - p
    return pieces


def _gate_scan_kernel(gT_ref, cT_ref, c_ref, e_ref):
    s = gT_ref.shape[-1]
    rows = 2 * 8
    row = lax.broadcasted_iota(jnp.int32, (rows, CHUNK), 0)
    pad = jnp.zeros((GATE_ROWS - rows, CHUNK), F32)
    epad = jnp.zeros((GATE_ROWS - N_SPLIT * H_F, CHUNK), F32)
    sls = [slice(c * CHUNK, (c + 1) * CHUNK) for c in range(s // CHUNK)]
    g = [gT_ref[0, 0:rows, sl] for sl in sls]
    lane = lax.broadcasted_iota(jnp.int32, (rows, CHUNK), 1)
    cs = g
    sh = 1
    while sh < CHUNK:
        cs = _each(lambda x: x + jnp.where(lane >= sh, pltpu.roll(x, sh, 1), 0.0), cs)
        sh *= 2
    carry = jnp.zeros((rows, 1), F32)
    outs = []
    for gc, csc in zip(g, cs):
        outs.append(jnp.where(row < H_M, gc, jnp.where(row < 2 * H_M, csc, csc + carry)))
        carry = carry + csc[:, CHUNK - 1:CHUNK]
    fulls = _each(lambda o: jnp.concatenate([o, pad], axis=0), outs)
    fullsT = _each(lambda f: f.T, fulls)
    pieces = _each(lambda o: jnp.concatenate(_split_bf16(o[2 * H_M:N_GATES, :] * LOG2E) + [epad], axis=0), outs)
    piecesT = _each(lambda p: p.T.astype(BF16), pieces)
    for sl, f, fT, pT in zip(sls, fulls, fullsT, piecesT):
        cT_ref[0, :, sl] = f
        c_ref[sl, :] = fT
        e_ref[sl, :] = pT


def _gate_scan(gT):
    b, r, s = gT.shape
    return pl.pallas_call(
        _gate_scan_kernel,
        grid=(b,),
        in_specs=[pl.BlockSpec((1, r, s), lambda bi: (bi, 0, 0))],
        out_specs=(pl.BlockSpec((1, r, s), lambda bi: (bi, 0, 0)),
                   pl.BlockSpec((s, r), lambda bi: (bi, 0)),
                   pl.BlockSpec((s, r), lambda bi: (bi, 0))),
        out_shape=(jax.ShapeDtypeStruct((b, r, s), F32), jax.ShapeDtypeStruct((b * s, r), F32),
                   jax.ShapeDtypeStruct((b * s, r), BF16)),
        compiler_params=_params("parallel"),
        name="gate_scan",
    )(gT)


def _each(f, *lists):
    return [f(*a) for a in zip(*lists)]


def _mlstm_gates(ig_row, b_row, ig_col, b_col, m_st, valid, n_valid):
    length = b_col[0].shape[0]
    dlog = _each(lambda bc, br, ir: jnp.where(valid, bc - br + ir, NEG_INF), b_col, b_row, ig_row)
    a_col = _each(lambda bc, m: bc + m, b_col, m_st)
    mx = _each(lambda d: jnp.max(d, axis=1, keepdims=True), dlog)
    m_t = _each(jnp.maximum, a_col, mx)
    w_intra = _each(lambda d, m: jnp.exp(d - m), dlog, m_t)
    w_inter = _each(lambda a, m: jnp.exp(a - m), a_col, m_t)
    floor = _each(lambda m: jnp.exp(-m), m_t)
    b_end = _each(lambda br: br[:, n_valid - 1:n_valid], b_row)
    a_end = _each(lambda be, m: be + m, b_end, m_st)
    wlog = _each(lambda be, bc, ic: be - bc + ic, b_end, b_col, ig_col)
    if n_valid < length:
        keep = lax.broadcasted_iota(jnp.int32, (length, 1), 0) < n_valid
        wlog = _each(lambda w: jnp.where(keep, w, NEG_INF), wlog)
    wmax = _each(lambda w: jnp.max(w, axis=0, keepdims=True), wlog)
    m_new = _each(jnp.maximum, a_end, wmax)
    ws = _each(lambda w, m: jnp.exp(w - m), wlog, m_new)
    decay = _each(lambda a, m: jnp.exp(a - m), a_end, m_new)
    return w_intra, w_inter, floor, ws, decay, m_new


def _mlstm_prompt_kernel(mq_ref, mk_ref, mvT_ref, smoT_ref, c_ref, cT_ref, gmh_ref,
                         hmT_ref, caugT_ref, m_ref):
    nb = mq_ref.shape[0]

    @pl.when(pl.program_id(0) == 0)
    def _():
        caugT_ref[...] = jnp.zeros_like(caugT_ref)
        m_ref[...] = jnp.zeros_like(m_ref)

    srow = lax.broadcasted_iota(jnp.int32, (CHUNK, CHUNK), 0)
    tcol = lax.broadcasted_iota(jnp.int32, (CHUNK, CHUNK), 1)
    valid = srow <= tcol
    chains = [(b, h) for b in range(nb) for h in range(H_M)]
    sl = lambda h: slice(h * DK_M, (h + 1) * DK_M)
    vsl = lambda h: slice(h * MV_ROWS, (h + 1) * MV_ROWS)

    q = [mq_ref[b, :, sl(h)] for b, h in chains]
    k = [mk_ref[b, :, sl(h)] for b, h in chains]
    vT = [mvT_ref[b, vsl(h), :] for b, h in chains]
    sT = _each(_dot_nt, k, q)
    r2T = _each(lambda qq, bh: _dot_nt(caugT_ref[bh[0], bh[1]].astype(BF16), qq), q, chains)

    ig_row = [cT_ref[b, h:h + 1, :] for b, h in chains]
    b_row = [cT_ref[b, H_M + h:H_M + h + 1, :] for b, h in chains]
    c_col = [c_ref[b, :, h:h + 1] - c_ref[b, :, H_M + h:H_M + h + 1] for b, h in chains]
    m_st = [m_ref[b, h:h + 1, 0:1] for b, h in chains]
    dlog = _each(lambda br, cc: jnp.where(valid, br + cc, NEG_INF), b_row, c_col)
    a_row = _each(lambda br, m: br + m, b_row, m_st)
    m_t = _each(lambda a, d: jnp.maximum(a, jnp.max(d, axis=0, keepdims=True)), a_row, dlog)
    sw = _each(lambda s, d, m: (s * jnp.exp(d - m)).astype(BF16), sT, dlog, m_t)
    w_inter = _each(lambda a, m: jnp.exp(a - m), a_row, m_t)
    floor = _each(lambda m: jnp.exp(-m), m_t)

    b_end = _each(lambda br: br[:, CHUNK - 1:CHUNK], b_row)
    a_end = _each(lambda be, m: be + m, b_end, m_st)
    wlog = _each(lambda be, br, ir: be - br + ir, b_end, b_row, ig_row)
    m_new = _each(lambda a, w: jnp.maximum(a, jnp.max(w, axis=1, keepdims=True)), a_end, wlog)
    ws = _each(lambda w, m: jnp.exp(w - m), wlog, m_new)
    decay = _each(lambda a, m: jnp.exp(a - m), a_end, m_new)
    vw = _each(lambda v, w: (v.astype(F32) * w).astype(BF16), vT, ws)

    r1T = _each(_dot, vT, sw)
    upd = _each(_dot, vw, k)

    num = _each(lambda w, a, c: w * a[:DV_M] + c[:DV_M], w_inter, r2T, r1T)
    den = _each(lambda w, a, c: w * a[DV_M:DV_M + 1] + c[DV_M:DV_M + 1], w_inter, r2T, r1T)
    hh = _each(lambda n, d, f: n / jnp.maximum(jnp.abs(d), f), num, den, floor)
    scale = _each(lambda x: lax.rsqrt(jnp.mean(x * x, axis=0, keepdims=True) + EPS), hh)
    gain = [jnp.broadcast_to(gmh_ref[sl(h), :], (DV_M, CHUNK)) for h in range(H_M)]
    for (b, h), x, sc, dc, u, mn in zip(chains, hh, scale, decay, upd, m_new):
        hmT_ref[b, sl(h), :] = (x * sc * gain[h] * smoT_ref[b, sl(h), :].astype(F32)).astype(BF16)
        caugT_ref[b, h] = dc * caugT_ref[b, h] + u
        m_ref[b, h:h + 1, :] = jnp.broadcast_to(mn, (1, LANES))


def _mlstm_prompt(mq, mk, mvT, smoT, c_tok, cT, gmh_col):
    b, s, _ = mq.shape
    tok = lambda w: pl.BlockSpec((b, CHUNK, w), lambda c: (0, c, 0))
    hmaj = lambda r: pl.BlockSpec((b, r, CHUNK), lambda c: (0, 0, c))
    whole = lambda shape: pl.BlockSpec(shape, lambda c: (0,) * len(shape))
    state = (b, H_M, MV_ROWS, DK_M)
    return pl.pallas_call(
        _mlstm_prompt_kernel,
        grid=(s // CHUNK,),
        in_specs=[tok(W_M), tok(W_M), hmaj(H_M * MV_ROWS), hmaj(W_M), tok(GATE_ROWS), hmaj(GATE_ROWS),
                  whole(gmh_col.shape)],
        out_specs=(hmaj(W_M), whole(state), whole((b, 8, LANES))),
        out_shape=(jax.ShapeDtypeStruct((b, W_M, s), BF16),
                   jax.ShapeDtypeStruct(state, F32),
                   jax.ShapeDtypeStruct((b, 8, LANES), F32)),
        compiler_params=_params("arbitrary"),
        name="mlstm_prompt",
    )(mq, mk, mvT, smoT, c_tok, cT, gmh_col)


HEAD_GROUP = 2
LOOKAHEAD = 1
QT = 256


def _fox_prompt_kernel(fqT_ref, fk_ref, e_ref, vT_ref, o_ref, qa_sc, m_sc, acc_sc, *, tq, tk):
    qi = pl.program_id(1)
    ratio = tq // tk
    nqt = tq // QT
    rowq = lax.broadcasted_iota(jnp.int32, (2 * DH_F, QT), 0)
    srow = lax.broadcasted_iota(jnp.int32, (tk, QT), 0)
    tcol = lax.broadcasted_iota(jnp.int32, (tk, QT), 1)
    hsl = lambda h: slice(h * DH_F, (h + 1) * DH_F)
    psl = lambda h: slice((h // 2) * 2 * DH_F, (h // 2 + 1) * 2 * DH_F)
    qsl = lambda n: slice(n * QT, (n + 1) * QT)

    for h in range(H_F):
        lo = (h % 2) * DH_F
        sel = jnp.where((rowq % H_F == h) & (rowq < N_SPLIT * H_F), -1.0, 0.0).astype(BF16)
        for n in range(nqt):
            q2 = fqT_ref[0, psl(h), qsl(n)]
            qh = jnp.where((rowq >= lo) & (rowq < lo + DH_F), q2, 0)
            qa_sc[h, n] = jnp.concatenate([qh, sel], axis=0)
    m_sc[...] = jnp.full_like(m_sc, NEG_INF)
    acc_sc[...] = jnp.zeros_like(acc_sc)
    groups = [range(g0, g0 + HEAD_GROUP) for g0 in range(0, H_F, HEAD_GROUP)]

    def run(tiles):
        work = [(ti, heads) for ti in range(len(tiles)) for heads in groups]
        k0 = [pl.multiple_of(j * tk, tk) for j, _ in tiles]
        e = [e_ref[pl.ds(k, tk), :] for k in k0]
        halves = [range(nqt) if d is None else range(d // QT, nqt) for _, d in tiles]
        st = {}

        def scores(ti, heads):
            for h in heads:
                ka = jnp.concatenate([fk_ref[pl.ds(k0[ti], tk), psl(h)], e[ti]], axis=1)
                for n in halves[ti]:
                    st[ti, h, n] = _dot(ka, qa_sc[h, n])

        for w in work[:LOOKAHEAD]:
            scores(*w)
        for wi, (ti, heads) in enumerate(work):
            if wi + LOOKAHEAD < len(work):
                scores(*work[wi + LOOKAHEAD])
            units = [(h, n) for h in heads for n in halves[ti]]
            p, alpha = {}, {}
            for h, n in units:
                s_u = st.pop((ti, h, n))
                d = tiles[ti][1]
                if d is not None and d + tk > n * QT:
                    s_u = jnp.where(srow + d <= tcol + n * QT, s_u, NEG_INF)
                m_old = m_sc[h, n]
                m_new = jnp.maximum(m_old, jnp.max(s_u, axis=0, keepdims=True))
                alpha[h, n] = jnp.exp2(m_old - m_new)
                p[h, n] = jnp.exp2(s_u - m_new).astype(BF16)
                m_sc[h, n] = m_new
            for h, n in units:
                vT = vT_ref[0, h * V_ROWS:(h + 1) * V_ROWS, pl.ds(k0[ti], tk)]
                acc_sc[h, n] = alpha[h, n] * acc_sc[h, n] + _dot(vT, p[h, n])

    def body(jj, carry):
        run([(jj * ratio + r, None) for r in range(ratio)])
        return carry

    lax.fori_loop(0, qi, body, 0)
    run([(qi * ratio + r, r * tk) for r in range(ratio)])
    for h in range(H_F):
        for n in range(nqt):
            o_ref[0, hsl(h), qsl(n)] = (acc_sc[h, n, :DH_F, :] / acc_sc[h, n, DH_F:DH_F + 1, :]).astype(BF16)


def _fox_prompt(fqT, fk_tok, e_tok, fvT16, tq, tk):
    b, w, s = fqT.shape
    kern = functools.partial(_fox_prompt_kernel, tq=tq, tk=tk)
    return pl.pallas_call(
        kern,
        grid=(b, s // tq),
        in_specs=[pl.BlockSpec((1, w, tq), lambda bi, qi: (bi, 0, qi)),
                  pl.BlockSpec((s, w), lambda bi, qi: (bi, 0)),
                  pl.BlockSpec((s, GATE_ROWS), lambda bi, qi: (bi, 0)),
                  pl.BlockSpec((1, H_F * V_ROWS, s), lambda bi, qi: (bi, 0, 0))],
        out_specs=pl.BlockSpec((1, w, tq), lambda bi, qi: (bi, 0, qi)),
        out_shape=jax.ShapeDtypeStruct((b, w, s), BF16),
        scratch_shapes=[pltpu.VMEM((H_F, tq // QT, 2 * DH_F + GATE_ROWS, QT), BF16),
                        pltpu.VMEM((H_F, tq // QT, 1, QT), F32),
                        pltpu.VMEM((H_F, tq // QT, V_ROWS, QT), F32)],
        compiler_params=_params("parallel", "arbitrary"),
        name="fox_prompt",
    )(fqT, fk_tok, e_tok, fvT16)


FF_TILE = MXU_TILE
DOWN_TILE = 2 * MXU_TILE


def _merge_ffn_kernel(*refs, fox):
    refs = list(refs)
    pt_ref = refs.pop(0) if fox else None
    (x_ref, hmT_ref, yfT_ref, sga_ref, sgb_ref, p_ref, wpa_ref, wpb_ref, wout_ref, gffn_ref, wgu_ref,
     wdown_ref, gple_ref, wpg_ref, wpp_ref) = refs[:15]
    del refs[:15]
    fox_in = [refs.pop(0) for _ in range(7)] if fox else None
    y_ref = refs.pop(0)
    yf_ref = refs.pop(0) if fox else None
    hdn_sc = refs.pop(0)
    val = {}

    def g_merge():
        a = _dot_tn(hmT_ref[0], wpa_ref[...])
        bb = _dot_tn(yfT_ref[0], wpb_ref[...])
        u = sga_ref[...].astype(F32) * a + sgb_ref[...].astype(F32) * bb
        x1 = x_ref[...] + _dot(u.astype(BF16), wout_ref[...])
        val["x1"] = x1
        val["xn"] = (x1 * _rms_scale(x1) * gffn_ref[...]).astype(BF16)

    def g_ffn(j):
        def run():
            g = _dot(val["xn"], wgu_ref[:, j:j + FF_TILE])
            up = _dot(val["xn"], wgu_ref[:, D_FF + j:D_FF + j + FF_TILE])
            hdn_sc[:, j:j + FF_TILE] = (jax.nn.silu(g) * up).astype(BF16)
        return run

    def g_down(c):
        def run():
            val["x2", c] = val["x1"][:, c:c + DOWN_TILE] + _dot(hdn_sc[...], wdown_ref[:, c:c + DOWN_TILE])
        return run

    def g_out():
        x2 = jnp.concatenate([val["x2", c] for c in range(0, D_MODEL, DOWN_TILE)], axis=1)
        xg = (x2 * _rms_scale(x2) * gple_ref[...]).astype(BF16)
        gate = jax.nn.sigmoid(_dot(xg, wpg_ref[...]))
        y_ref[...] = x2 + gate * _dot(p_ref[...].astype(BF16), wpp_ref[...])

    kt = lambda k: -(-k // MXU_TILE)
    groups = [(g_merge, 2 * kt(W_M) * kt(D_MODEL) + kt(D_MODEL) ** 2)]
    groups += [(g_ffn(j), 2 * kt(D_MODEL)) for j in range(0, D_FF, FF_TILE)]
    groups += [(g_down(c), kt(D_FF) * kt(DOWN_TILE)) for c in range(0, D_MODEL, DOWN_TILE)]
    groups += [(g_out, kt(D_MODEL) ** 2 + kt(PLE_DIM) * kt(D_MODEL))]
    if fox:
        _run_with_hosted_fox(groups, pl.program_id(0), fox, pt_ref, *fox_in, yf_ref, *refs)
    else:
        for g, _ in groups:
            g()


def _merge_ffn(x, hmT, yfT, sga, sgb, p, wpa, wpb, wout, gffn, wgu, wdown, gple, wpg, wpp, tm, fox=None):
    t, d = x.shape
    b, _, s = yfT.shape
    nt = s // tm
    steps = t // tm
    tok = lambda w: pl.BlockSpec((tm, w), lambda i, *_: (i, 0))
    full = lambda a: pl.BlockSpec(a.shape, lambda i, *_: (0,) * a.ndim)
    hmaj = lambda r: pl.BlockSpec((1, r, tm), lambda i, *_: (i // nt, 0, i % nt))
    in_specs = [tok(d), hmaj(W_M), hmaj(W_F), tok(d), tok(d), tok(PLE_DIM),
                full(wpa), full(wpb), full(wout), full(gffn), full(wgu), full(wdown),
                full(gple), full(wpg), full(wpp)]
    operands = (x, hmT, yfT, sga, sgb, p, wpa, wpb, wout, gffn, wgu, wdown, gple, wpg, wpp)
    hdn = pltpu.VMEM((tm, D_FF), BF16)
    if fox is None:
        return pl.pallas_call(
            functools.partial(_merge_ffn_kernel, fox=None),
            grid=(steps,),
            in_specs=in_specs,
            out_specs=tok(d),
            out_shape=jax.ShapeDtypeStruct((t, d), F32),
            scratch_shapes=[hdn],
            compiler_params=_params("parallel"),
            name="merge_ffn",
        )(*operands)

    page_table, fq, kn, vn, lfn, ck, cv, cl = fox
    db, n_tok, _ = fq.shape
    n_pages, page = page_table.shape[1], ck.shape[-1]
    per = db // steps
    assert per * steps == db and per % 2 == 0
    seqs = lambda a: pl.BlockSpec((per,) + a.shape[1:], lambda i, *_: (i,) + (0,) * (a.ndim - 1))
    hbm = pl.BlockSpec(memory_space=pl.ANY)
    return pl.pallas_call(
        functools.partial(_merge_ffn_kernel, fox=(per, db, n_pages, n_tok)),
        grid_spec=pltpu.PrefetchScalarGridSpec(
            num_scalar_prefetch=1, grid=(steps,),
            in_specs=in_specs + [seqs(fq), seqs(kn), seqs(vn), seqs(lfn), hbm, hbm, hbm],
            out_specs=(tok(d), seqs(fq)),
            scratch_shapes=[hdn,
                            pltpu.VMEM((2, n_pages, W_F, page), F32),
                            pltpu.VMEM((2, n_pages, W_F, page), F32),
                            pltpu.VMEM((2, n_pages, H_F, page), F32),
                            pltpu.SemaphoreType.DMA((2,))]),
        out_shape=(jax.ShapeDtypeStruct((t, d), F32), jax.ShapeDtypeStruct(fq.shape, F32)),
        compiler_params=_params("arbitrary", vmem=VMEM_LIMIT_HOST),
        name="merge_ffn_fox_sample",
    )(page_table, *operands, fq, kn, vn, lfn, ck, cv, cl)


_SPLIT = (W_M, W_M, W_M, W_M, H_M, H_M, W_F, W_F, W_F, H_F, D_MODEL, D_MODEL)


def _prep_weights(g_mix, w_in, b_mi, b_mf, b_ff, g_mh, g_qn, g_kn, w_pa, w_pb, w_out, g_ffn, w_gu,
                  w_down, g_ple, w_ple_gate, w_ple_proj):
    offs = [0]
    for sz in _SPLIT:
        offs.append(offs[-1] + sz)
    o_mq, _, o_mv, o_mo, o_mi, o_mf, o_fq, _, _, o_ff, o_ga, _, o_end = offs
    w = w_in[0]
    wT = w.T
    wn = jnp.concatenate([w[:, o_mq:o_mv], w[:, o_mo:o_mi], w[:, o_ga:o_end]], axis=1).astype(BF16)
    wt = jnp.concatenate([wT[o_fq:o_ff], wT[o_mv:o_mo], wT[o_mi:o_fq], wT[o_ff:o_ga],
                          jnp.zeros((GATE_ROWS - N_GATES, D_MODEL), F32)], axis=0).astype(BF16)
    bias_col = jnp.concatenate([b_mi[0], b_mf[0], b_ff[0], jnp.zeros((GATE_ROWS - N_GATES,), F32)])[:, None]
    gain_col = jnp.concatenate([jnp.tile(g_qn[0], H_F) * (F_SCALE * LOG2E), jnp.tile(g_kn[0], H_F)])[:, None]
    return dict(
        g_mix=g_mix, wn=wn, wt=wt, bias_col=bias_col, gain_col=gain_col,
        gmh=g_mh[0].reshape(1, W_M),
        wpa=w_pa[0].astype(BF16), wpb=w_pb[0].astype(BF16), wout=w_out[0].astype(BF16),
        gffn=g_ffn, wgu=w_gu[0].astype(BF16), wdown=w_down[0].astype(BF16),
        gple=g_ple, wpg=w_ple_gate[0].astype(BF16), wpp=w_ple_proj[0].astype(BF16))


def _prompt_path(x, p, wd, tm_in, tm_ffn, tq, tk, fox):
    b, s, d = x.shape
    (mq, mk, mvT, smoT, sga, sgb, fqT, fkT, fvT, fk_tok, fvT16, gT) = _in_proj(
        x, wd["g_mix"], wd["wn"], wd["wt"], wd["bias_col"], wd["gain_col"], tm_in)
    cT, c_tok, e_tok = _gate_scan(gT)
    seq3 = lambda a: a.reshape(b, s, a.shape[-1])
    hmT, caugT, m_out = _mlstm_prompt(seq3(mq), seq3(mk), mvT, smoT, seq3(c_tok), cT, wd["gmh"].reshape(W_M, 1))
    yfT = _fox_prompt(fqT, fk_tok, e_tok, fvT16, tq, tk)
    y, yf_sample = _merge_ffn(
        x.reshape(b * s, d), hmT, yfT, sga, sgb, p.reshape(b * s, PLE_DIM),
        wd["wpa"], wd["wpb"], wd["wout"], wd["gffn"], wd["wgu"], wd["wdown"],
        wd["gple"], wd["wpg"], wd["wpp"], tm_ffn, fox)
    to_cache = lambda aT: aT.reshape(b, H_F, DH_F, s).transpose(0, 3, 1, 2)[None]
    logf = gT[:, 2 * H_M:N_GATES, :].transpose(0, 2, 1)[None]
    c_out = caugT[:, :, :DV_M, :].transpose(0, 1, 3, 2)[None]
    return (y.reshape(b, s, d), to_cache(fkT), to_cache(fvT), logf,
            c_out, caugT[None, :, :, DV_M, :], m_out[None, :, :H_M, 0]), yf_sample


L_PAD = 16
SEQ_BLOCK = 8


def _mlstm_sample_kernel(q_ref, k_ref, v_ref, smo_ref, grow_ref, gcol_ref, c_ref, n_ref, m_ref, gmh_ref,
                         hm_ref, co_ref, no_ref, mo_ref, *, n_valid):
    length = L_PAD
    row = lax.broadcasted_iota(jnp.int32, (length, length), 0)
    col = lax.broadcasted_iota(jnp.int32, (length, length), 1)
    valid = (col <= row) & (col < n_valid)
    chains = [(i, h) for i in range(SEQ_BLOCK) for h in range(H_M)]
    sl = lambda h: slice(h * DK_M, (h + 1) * DK_M)

    q = [q_ref[i, :, sl(h)] for i, h in chains]
    k = [k_ref[i, :, sl(h)] for i, h in chains]
    v = [v_ref[i, :, sl(h)] for i, h in chains]
    n_st = [n_ref[i, h:h + 1, :] for i, h in chains]
    s_mat = _each(_dot_nt, q, k)
    r2 = _each(lambda qq, ih: _dot(qq, c_ref[ih[0], ih[1]].astype(BF16)), q, chains)
    qn = _each(lambda qq, n: jnp.sum(qq.astype(F32) * n, axis=1, keepdims=True), q, n_st)

    lf_row = [grow_ref[i, H_M + h:H_M + h + 1, 0:length] for i, h in chains]
    lf_col = [gcol_ref[i, :, H_M + h:H_M + h + 1] for i, h in chains]
    b_col = _each(lambda r: jnp.sum(jnp.where(col <= row, r, 0.0), axis=1, keepdims=True), lf_row)
    b_row = _each(lambda c: jnp.sum(jnp.where(row <= col, c, 0.0), axis=0, keepdims=True), lf_col)
    w_intra, w_inter, floor, ws, decay, m_new = _mlstm_gates(
        [grow_ref[i, h:h + 1, 0:length] for i, h in chains], b_row,
        [gcol_ref[i, :, h:h + 1] for i, h in chains], b_col,
        [m_ref[i:i + 1, h:h + 1] for i, h in chains], valid, n_valid)
    sw = _each(lambda s, w: s * w, s_mat, w_intra)
    kw = _each(lambda kk, w: kk.astype(F32) * w, k, ws)

    r1 = _each(lambda s, vv: _dot(s.astype(BF16), vv), sw, v)
    upd = _each(lambda kk, vv: _dot_tn(kk.astype(BF16), vv), kw, v)
    sw_sum = _each(lambda s: jnp.sum(s, axis=1, keepdims=True), sw)
    kw_sum = _each(lambda kk: jnp.sum(kk, axis=0, keepdims=True), kw)

    num = _each(lambda w, a, c: w * a + c, w_inter, r2, r1)
    den = _each(lambda w, a, c: w * a + c, w_inter, qn, sw_sum)
    hh = _each(lambda n, d, f: n / jnp.maximum(jnp.abs(d), f), num, den, floor)
    scale = _each(_rms_scale, hh)
    for (i, h), x, sc, dc, u, n, ks, mn in zip(chains, hh, scale, decay, upd, n_st, kw_sum, m_new):
        hm_ref[i, :, sl(h)] = (x * sc * gmh_ref[:, sl(h)] * smo_ref[i, :, sl(h)].astype(F32)).astype(BF16)
        co_ref[i, h] = dc * c_ref[i, h] + u
        no_ref[i, h:h + 1, :] = dc * n + ks
        mo_ref[i:i + 1, h:h + 1] = mn


def _mlstm_sample(q, k, v, smo, grow, gcol, c_st, n_st, m_st, gmh, n_valid):
    db = q.shape[0]
    blk = lambda a: pl.BlockSpec((SEQ_BLOCK,) + a.shape[1:], lambda i: (i,) + (0,) * (a.ndim - 1))
    kern = functools.partial(_mlstm_sample_kernel, n_valid=n_valid)
    return pl.pallas_call(
        kern,
        grid=(db // SEQ_BLOCK,),
        in_specs=[blk(q), blk(k), blk(v), blk(smo), blk(grow), blk(gcol), blk(c_st), blk(n_st), blk(m_st),
                  pl.BlockSpec(gmh.shape, lambda i: (0, 0))],
        out_specs=(blk(q), blk(c_st), blk(n_st), blk(m_st)),
        out_shape=(jax.ShapeDtypeStruct(q.shape, BF16), jax.ShapeDtypeStruct(c_st.shape, F32),
                   jax.ShapeDtypeStruct(n_st.shape, F32), jax.ShapeDtypeStruct(m_st.shape, F32)),
        compiler_params=_params("parallel"),
        name="mlstm_sample",
    )(q, k, v, smo, grow, gcol, c_st, n_st, m_st, gmh)


def _run_with_hosted_fox(groups, step, fox, pt_ref, fq_ref, kn_ref, vn_ref, lfn_ref, ck_hbm, cv_hbm, cl_hbm,
                         yf_ref, kbuf, vbuf, lbuf, sem):
    per, total, n_pages, n_tok = fox
    first = step * per

    def page_copies(n, half):
        copies = []
        for j in range(n_pages):
            pid = pt_ref[n, j]
            copies.append(pltpu.make_async_copy(ck_hbm.at[pid], kbuf.at[half, j], sem.at[half]))
            copies.append(pltpu.make_async_copy(cv_hbm.at[pid], vbuf.at[half, j], sem.at[half]))
            copies.append(pltpu.make_async_copy(cl_hbm.at[pid], lbuf.at[half, j], sem.at[half]))
        return copies

    @pl.when(step == 0)
    def _():
        for c in page_copies(0, 0):
            c.start()

    total_w = sum(w for _, w in groups)
    runs, acc_w = [[] for _ in range(per)], 0
    for g, w in groups:
        runs[min(per - 1, per * (2 * acc_w + w) // (2 * total_w))].append(g)
        acc_w += w
    assert all(runs)
    for j in range(per):
        n, half = first + j, j % 2
        mine = runs[j]

        def prefetch():
            for c in page_copies(n + 1, 1 - half):
                c.start()

        if j + 1 < per:
            prefetch()
        else:
            pl.when(n + 1 < total)(prefetch)

        for c in page_copies(n, half):
            c.wait()
        scores = _fox_sample_scores(fq_ref[j], kn_ref[j], lfn_ref[j], lambda pg: kbuf[half, pg],
                                    lbuf[half].reshape(n_pages * H_F, LANES), n_pages, n_tok)
        for g in mine[:-1]:
            g()
        yf_ref[j] = _fox_sample_output(*scores, lambda pg: vbuf[half, pg], vn_ref[j], n_pages, n_tok)
        for g in mine[-1:]:
            g()


def _own_lanes():
    sub = lax.broadcasted_iota(jnp.int32, (H_F, W_F), 0)
    head_of_lane = lax.broadcasted_iota(jnp.int32, (H_F, W_F), 1) // DH_F
    return sub == head_of_lane


def _fox_sample_scores(fq, kn_new, lfn, k_page, lf_pages, n_pages, n_tok):
    rows = n_tok * H_F
    own = _own_lanes()
    tile = lambda a: jnp.concatenate([a] * n_tok, axis=0)
    qbd = jnp.concatenate(
        [jnp.where(own, jnp.broadcast_to(fq[t:t + 1, :], (H_F, W_F)), 0.0) for t in range(n_tok)], axis=0)
    qbd16 = qbd.astype(BF16)

    incl_all = _lane_cumsum(lf_pages)
    bias = [None] * n_pages
    carry = jnp.zeros((H_F, 1), F32)
    for pg in reversed(range(n_pages)):
        incl = incl_all[pg * H_F:(pg + 1) * H_F, :]
        tot = incl[:, LANES - 1:LANES]
        bias[pg] = (tot - incl + carry) * LOG2E
        carry = carry + tot

    logits = []
    for j in range(0, n_pages, 2):
        kcat = jnp.concatenate([k_page(j).astype(BF16), k_page(j + 1).astype(BF16)], axis=1)
        bcat = jnp.concatenate([tile(bias[j]), tile(bias[j + 1])], axis=1)
        logits.append(_dot(qbd16, kcat) + bcat)

    rowi = lax.broadcasted_iota(jnp.int32, (rows, 1), 0)
    qf = qbd16.astype(F32)
    new_logits = []
    f_new = jnp.zeros((H_F, 1), F32)
    for s in range(n_tok):
        f_new = f_new + lfn[:, s:s + 1]
        kn = kn_new[s:s + 1, :].astype(BF16).astype(F32)
        ln = jnp.sum(qf * kn, axis=1, keepdims=True) - tile(f_new) * LOG2E
        new_logits.append(jnp.where(rowi >= s * H_F, ln, NEG_INF))

    m = new_logits[0]
    for ln in new_logits[1:]:
        m = jnp.maximum(m, ln)
    for lg in logits:
        m = jnp.maximum(m, jnp.max(lg, axis=1, keepdims=True))
    return logits, new_logits, m


def _fox_sample_output(logits, new_logits, m, v_page, vn_new, n_pages, n_tok):
    rows = n_tok * H_F
    own = _own_lanes()
    denom = jnp.zeros((rows, 1), F32)
    acc = jnp.zeros((rows, W_F), F32)
    for idx, lg in enumerate(logits):
        p = jnp.exp2(lg - m)
        denom = denom + jnp.sum(p, axis=1, keepdims=True)
        j = 2 * idx
        vcat = jnp.concatenate([v_page(j).astype(BF16), v_page(j + 1).astype(BF16)], axis=1)
        acc = acc + _dot_nt(p.astype(BF16), vcat)
    for s, ln in enumerate(new_logits):
        p = jnp.exp2(ln - m)
        denom = denom + p
        acc = acc + p.astype(BF16).astype(F32) * vn_new[s:s + 1, :]
    out = acc / denom
    return jnp.concatenate(
        [jnp.sum(jnp.where(own, out[t * H_F:(t + 1) * H_F, :], 0.0), axis=0, keepdims=True)
         for t in range(n_tok)], axis=0)


def _sample_front(x, cache_k, cache_v, cache_logf, state_c, state_n, state_m, page_table, wd):
    db, n_tok, d = x.shape
    t = db * n_tok
    (mq, mk, mvT, smoT, sga, sgb, fqT, fkT, fvT, fk_tok, _, gT) = _in_proj(
        x.reshape(1, t, d), wd["g_mix"], wd["wn"], wd["wt"], wd["bias_col"], wd["gain_col"], t)
    mv = mvT[0].reshape(H_M, MV_ROWS, t)[:, :DV_M].reshape(W_M, t).T
    smo = smoT[0].T

    pad_tok = lambda a: jnp.pad(a.reshape(db, n_tok, -1), ((0, 0), (0, L_PAD - n_tok), (0, 0)))
    g16 = gT[0, :2 * 8].reshape(2 * 8, db, n_tok)
    grow = jnp.pad(g16.transpose(1, 0, 2), ((0, 0), (0, 0), (0, LANES - n_tok)))
    gcol = jnp.pad(g16.transpose(1, 2, 0), ((0, 0), (0, L_PAD - n_tok), (0, LANES - 2 * 8)))
    hm, c_new, n_new, m_new = _mlstm_sample(
        pad_tok(mq), pad_tok(mk), pad_tok(mv), pad_tok(smo), grow, gcol,
        state_c, state_n, state_m, wd["gmh"], n_tok)
    hmT = hm[:, :n_tok].reshape(t, W_M).T[None]

    pool, page = cache_k.shape[0], cache_k.shape[1]
    ck = cache_k.transpose(0, 2, 3, 1).reshape(pool, W_F, page)
    cv = cache_v.transpose(0, 2, 3, 1).reshape(pool, W_F, page)
    cl = cache_logf.transpose(0, 2, 1)
    tok3 = lambda aT: aT[0].T.astype(F32).reshape(db, n_tok, W_F)
    lfn = jnp.pad(gT[0, 2 * H_M:N_GATES].reshape(H_F, db, n_tok).transpose(1, 0, 2),
                  ((0, 0), (0, 0), (0, LANES - n_tok)))
    fox = (page_table, tok3(fqT), fk_tok.astype(F32).reshape(db, n_tok, W_F), tok3(fvT), lfn, ck, cv, cl)
    to_cache = lambda aT: aT[0].reshape(H_F, DH_F, db, n_tok).transpose(2, 3, 0, 1)[None]
    logf = gT[0, 2 * H_M:N_GATES].reshape(H_F, db, n_tok).transpose(1, 2, 0)[None]
    outs = (to_cache(fkT), to_cache(fvT), logf, c_new[None], n_new[None], m_new[None])
    return fox, (hmT, sga, sgb), outs


def _sample_back(x, p, yf, hmT, sga, sgb, wd):
    db, n_tok, d = x.shape
    t = db * n_tok
    yfT = yf.reshape(t, W_F).T.astype(BF16)[None]
    y = _merge_ffn(x.reshape(t, d), hmT, yfT, sga, sgb, p.reshape(t, PLE_DIM),
                   wd["wpa"], wd["wpb"], wd["wout"], wd["gffn"], wd["wgu"], wd["wdown"],
                   wd["gple"], wd["wpg"], wd["wpp"], t)
    return y.reshape(db, n_tok, d)


def kernel(x_prompt, x_sample, p_prompt, p_sample, cache_k, cache_v, cache_logf, state_C, state_n, state_m,
           page_table, g_mix, w_in, b_mi, b_mf, b_ff, g_mh, g_qn, g_kn, w_pa, w_pb, w_out, g_ffn, w_gu,
           w_down, g_ple, w_ple_gate, w_ple_proj):
    wd = _prep_weights(g_mix, w_in, b_mi, b_mf, b_ff, g_mh, g_qn, g_kn, w_pa, w_pb, w_out, g_ffn, w_gu,
                       w_down, g_ple, w_ple_gate, w_ple_proj)
    fox, mid, (ks, vs, lfs, cs, ns, ms) = _sample_front(
        x_sample, cache_k[0], cache_v[0], cache_logf[0], state_C[0], state_n[0], state_m[0], page_table, wd)
    (yp, kp, vp, lfp, cp, np_, mp), yf_sample = _prompt_path(
        x_prompt, p_prompt[0], wd, TOKEN_TILE, TOKEN_TILE, QUERY_TILE, KEY_TILE, fox)
    ys = _sample_back(x_sample, p_sample[0], yf_sample, *mid, wd)
    return (yp, ys, kp, vp, lfp, cp, np_, mp, ks, vs, lfs, cs, ns, ms)
```

```python
import functools

import jax
import jax.numpy as jnp
from jax import lax
from jax.experimental import pallas as pl
from jax.experimental.pallas import tpu as pltpu

D_MODEL = 1024
H_M = 4
DK_M = 128
DV_M = 128
W_M = H_M * DV_M
H_F = 8
DH_F = 64
W_F = H_F * DH_F
D_FF = 2816
PLE_DIM = 256
CHUNK = 128
EPS = 1e-6
F_SCALE = DH_F ** -0.5
K_SCALE_M = DK_M ** -0.5
LOG2E = 1.4426950408889634
N_GATES = 2 * H_M + H_F
GATE_ROWS = 128
N_SPLIT = 3
V_ROWS = DH_F + 16
MV_ROWS = DV_M + 16
LANES = 128
MXU_TILE = 256
TOKEN_TILE = 2 * MXU_TILE
PROJ_TILE = 4 * MXU_TILE
QUERY_TILE = 2 * MXU_TILE
KEY_TILE = MXU_TILE
VMEM_LIMIT = 56 * 1024 * 1024
VMEM_LIMIT_HOST = 62 * 1024 * 1024

BF16 = jnp.bfloat16
F32 = jnp.float32
NEG_INF = float("-inf")


def _dot(a, b):
    return jnp.dot(a, b, preferred_element_type=F32)


def _dot_nt(a, b):
    return lax.dot_general(a, b, (((1,), (1,)), ((), ())), preferred_element_type=F32)


def _dot_tn(a, b):
    return lax.dot_general(a, b, (((0,), (0,)), ((), ())), preferred_element_type=F32)


def _log_sigmoid(u):
    return jnp.minimum(u, 0.0) - jnp.log1p(jnp.exp(-jnp.abs(u)))


def _rms_scale(x):
    return lax.rsqrt(jnp.mean(x * x, axis=-1, keepdims=True) + EPS)


def _params(*sem, vmem=VMEM_LIMIT):
    return pltpu.CompilerParams(dimension_semantics=sem, vmem_limit_bytes=vmem)


def _in_proj_kernel(x_ref, gmix_ref, wn_ref, wt_ref, bias_ref, gain_ref,
                    mq_ref, mk_ref, mvT_ref, smoT_ref, sga_ref, sgb_ref,
                    fqT_ref, fkT_ref, fvT_ref, fk_ref, fvT16_ref, gT_ref):
    x = x_ref[0]
    xn = (x * _rms_scale(x) * gmix_ref[...]).astype(BF16)

    def zn(lo, hi):
        return _dot_nt(xn, wn_ref[lo:hi, :])

    def zt(lo, hi):
        return _dot_nt(wt_ref[lo:hi, :], xn)

    def head_norm(z, gain):
        tm = z.shape[-1]
        r = z.reshape(H_F, DH_F, tm)
        r = r * lax.rsqrt(jnp.mean(r * r, axis=1, keepdims=True) + EPS)
        return r.reshape(W_F, tm) * gain

    def g_mq():
        mq_ref[...] = zn(0, W_M).astype(BF16)

    def g_mk():
        mk_ref[...] = (zn(W_M, 2 * W_M) * K_SCALE_M).astype(BF16)

    def g_mv():
        vT = zt(3 * W_F, 3 * W_F + W_M)
        tm = vT.shape[-1]
        v_aug = jnp.concatenate([vT.reshape(H_M, DV_M, tm), jnp.ones((H_M, MV_ROWS - DV_M, tm), F32)], axis=1)
        mvT_ref[0] = v_aug.reshape(H_M * MV_ROWS, tm).astype(BF16)

    def g_smo():
        smoT_ref[0] = jax.nn.sigmoid(zn(2 * W_M, 3 * W_M)).T.astype(BF16)

    def g_gate(ref, off, c):
        def run():
            ref[:, c:c + W_M] = jax.nn.sigmoid(zn(off + c, off + c + W_M)).astype(BF16)
        return run

    def g_fq():
        fqT_ref[0] = head_norm(zt(0, W_F), gain_ref[0:W_F, :]).astype(BF16)

    def g_fk():
        fk = head_norm(zt(W_F, 2 * W_F), gain_ref[W_F:2 * W_F, :])
        fkT_ref[0] = fk
        fk_ref[...] = fk.T.astype(BF16)

    def g_fv():
        fv = zt(2 * W_F, 3 * W_F)
        fvT_ref[0] = fv
        tm = fv.shape[-1]
        fv_aug = jnp.concatenate([fv.reshape(H_F, DH_F, tm), jnp.ones((H_F, V_ROWS - DH_F, tm), F32)], axis=1)
        fvT16_ref[0] = fv_aug.reshape(H_F * V_ROWS, tm).astype(BF16)

    def g_gates():
        u = zt(3 * W_F + W_M, 3 * W_F + W_M + GATE_ROWS) + bias_ref[...]
        row = lax.broadcasted_iota(jnp.int32, u.shape, 0)
        gT_ref[0] = jnp.where(row < H_M, u, jnp.where(row < N_GATES, _log_sigmoid(u), 0.0))

    g_mq()
    g_mk()
    g_mv()
    g_smo()
    for ref, off in ((sga_ref, 3 * W_M), (sgb_ref, 3 * W_M + D_MODEL)):
        for c in range(0, D_MODEL, W_M):
            g_gate(ref, off, c)()
    g_fq()
    g_fk()
    g_fv()
    g_gates()


def _in_proj(x, g_mix, wn, wt, bias_col, gain_col, tm):
    b, s, d = x.shape
    nt = s // tm
    t = b * s
    tok = lambda w: pl.BlockSpec((tm, w), lambda bi, i: (bi * nt + i, 0))
    hm = lambda r: pl.BlockSpec((1, r, tm), lambda bi, i: (bi, 0, i))
    full = lambda a: pl.BlockSpec(a.shape, lambda bi, i: (0,) * a.ndim)
    out_shape = (
        jax.ShapeDtypeStruct((t, W_M), BF16),
        jax.ShapeDtypeStruct((t, W_M), BF16),
        jax.ShapeDtypeStruct((b, H_M * MV_ROWS, s), BF16),
        jax.ShapeDtypeStruct((b, W_M, s), BF16),
        jax.ShapeDtypeStruct((t, D_MODEL), BF16),
        jax.ShapeDtypeStruct((t, D_MODEL), BF16),
        jax.ShapeDtypeStruct((b, W_F, s), BF16),
        jax.ShapeDtypeStruct((b, W_F, s), F32),
        jax.ShapeDtypeStruct((b, W_F, s), F32),
        jax.ShapeDtypeStruct((t, W_F), BF16),
        jax.ShapeDtypeStruct((b, H_F * V_ROWS, s), BF16),
        jax.ShapeDtypeStruct((b, GATE_ROWS, s), F32),
    )
    out_specs = (tok(W_M), tok(W_M), hm(H_M * MV_ROWS), hm(W_M), tok(D_MODEL), tok(D_MODEL),
                 hm(W_F), hm(W_F), hm(W_F), tok(W_F), hm(H_F * V_ROWS), hm(GATE_ROWS))
    return pl.pallas_call(
        _in_proj_kernel,
        grid=(b, nt),
        in_specs=[pl.BlockSpec((1, tm, d), lambda bi, i: (bi, i, 0)),
                  full(g_mix), full(wn), full(wt), full(bias_col), full(gain_col)],
        out_specs=out_specs,
        out_shape=out_shape,
        compiler_params=_params("parallel", "parallel"),
        name="in_proj",
    )(x, g_mix, wn, wt, bias_col, gain_col)


def _lane_cumsum(x):
    lane = lax.broadcasted_iota(jnp.int32, x.shape, x.ndim - 1)
    sh = 1
    while sh < x.shape[-1]:
        x = x + jnp.where(lane >= sh, pltpu.roll(x, sh, x.ndim - 1), 0.0)
        sh *= 2
    return x


def _split_bf16(x):
    pieces = []
    for _ in range(N_SPLIT):
        p = x.astype(BF16).astype(F32)
        pieces.append(p)
        x = x - p
    return pieces


def _gate_scan_kernel(gT_ref, cT_ref, c_ref, e_ref):
    s = gT_ref.shape[-1]
    rows = 2 * 8
    row = lax.broadcasted_iota(jnp.int32, (rows, CHUNK), 0)
    pad = jnp.zeros((GATE_ROWS - rows, CHUNK), F32)
    epad = jnp.zeros((GATE_ROWS - N_SPLIT * H_F, CHUNK), F32)
    sls = [slice(c * CHUNK, (c + 1) * CHUNK) for c in range(s // CHUNK)]
    g = [gT_ref[0, 0:rows, sl] for sl in sls]
    lane = lax.broadcasted_iota(jnp.int32, (rows, CHUNK), 1)
    cs = g
    sh = 1
    while sh < CHUNK:
        cs = _each(lambda x: x + jnp.where(lane >= sh, pltpu.roll(x, sh, 1), 0.0), cs)
        sh *= 2
    carry = jnp.zeros((rows, 1), F32)
    outs = []
    for gc, csc in zip(g, cs):
        outs.append(jnp.where(row < H_M, gc, jnp.where(row < 2 * H_M, csc, csc + carry)))
        carry = carry + csc[:, CHUNK - 1:CHUNK]
    fulls = _each(lambda o: jnp.concatenate([o, pad], axis=0), outs)
    fullsT = _each(lambda f: f.T, fulls)
    pieces = _each(lambda o: jnp.concatenate(_split_bf16(o[2 * H_M:N_GATES, :] * LOG2E) + [epad], axis=0), outs)
    piecesT = _each(lambda p: p.T.astype(BF16), pieces)
    for sl, f, fT, pT in zip(sls, fulls, fullsT, piecesT):
        cT_ref[0, :, sl] = f
        c_ref[sl, :] = fT
        e_ref[sl, :] = pT


def _gate_scan(gT):
    b, r, s = gT.shape
    return pl.pallas_call(
        _gate_scan_kernel,
        grid=(b,),
        in_specs=[pl.BlockSpec((1, r, s), lambda bi: (bi, 0, 0))],
        out_specs=(pl.BlockSpec((1, r, s), lambda bi: (bi, 0, 0)),
                   pl.BlockSpec((s, r), lambda bi: (bi, 0)),
                   pl.BlockSpec((s, r), lambda bi: (bi, 0))),
        out_shape=(jax.ShapeDtypeStruct((b, r, s), F32), jax.ShapeDtypeStruct((b * s, r), F32),
                   jax.ShapeDtypeStruct((b * s, r), BF16)),
        compiler_params=_params("parallel"),
        name="gate_scan",
    )(gT)


def _each(f, *lists):
    return [f(*a) for a in zip(*lists)]


def _mlstm_gates(ig_row, b_row, ig_col, b_col, m_st, valid, n_valid):
    length = b_col[0].shape[0]
    dlog = _each(lambda bc, br, ir: jnp.where(valid, bc - br + ir, NEG_INF), b_col, b_row, ig_row)
    a_col = _each(lambda bc, m: bc + m, b_col, m_st)
    mx = _each(lambda d: jnp.max(d, axis=1, keepdims=True), dlog)
    m_t = _each(jnp.maximum, a_col, mx)
    w_intra = _each(lambda d, m: jnp.exp(d - m), dlog, m_t)
    w_inter = _each(lambda a, m: jnp.exp(a - m), a_col, m_t)
    floor = _each(lambda m: jnp.exp(-m), m_t)
    b_end = _each(lambda br: br[:, n_valid - 1:n_valid], b_row)
    a_end = _each(lambda be, m: be + m, b_end, m_st)
    wlog = _each(lambda be, bc, ic: be - bc + ic, b_end, b_col, ig_col)
    if n_valid < length:
        keep = lax.broadcasted_iota(jnp.int32, (length, 1), 0) < n_valid
        wlog = _each(lambda w: jnp.where(keep, w, NEG_INF), wlog)
    wmax = _each(lambda w: jnp.max(w, axis=0, keepdims=True), wlog)
    m_new = _each(jnp.maximum, a_end, wmax)
    ws = _each(lambda w, m: jnp.exp(w - m), wlog, m_new)
    decay = _each(lambda a, m: jnp.exp(a - m), a_end, m_new)
    return w_intra, w_inter, floor, ws, decay, m_new


def _mlstm_prompt_kernel(mq_ref, mk_ref, mvT_ref, smoT_ref, c_ref, cT_ref, gmh_ref,
                         hmT_ref, caugT_ref, m_ref):
    nb = mq_ref.shape[0]

    @pl.when(pl.program_id(0) == 0)
    def _():
        caugT_ref[...] = jnp.zeros_like(caugT_ref)
        m_ref[...] = jnp.zeros_like(m_ref)

    srow = lax.broadcasted_iota(jnp.int32, (CHUNK, CHUNK), 0)
    tcol = lax.broadcasted_iota(jnp.int32, (CHUNK, CHUNK), 1)
    valid = srow <= tcol
    chains = [(b, h) for b in range(nb) for h in range(H_M)]
    sl = lambda h: slice(h * DK_M, (h + 1) * DK_M)
    vsl = lambda h: slice(h * MV_ROWS, (h + 1) * MV_ROWS)

    q = [mq_ref[b, :, sl(h)] for b, h in chains]
    k = [mk_ref[b, :, sl(h)] for b, h in chains]
    vT = [mvT_ref[b, vsl(h), :] for b, h in chains]
    sT = _each(_dot_nt, k, q)
    r2T = _each(lambda qq, bh: _dot_nt(caugT_ref[bh[0], bh[1]].astype(BF16), qq), q, chains)

    ig_row = [cT_ref[b, h:h + 1, :] for b, h in chains]
    b_row = [cT_ref[b, H_M + h:H_M + h + 1, :] for b, h in chains]
    c_col = [c_ref[b, :, h:h + 1] - c_ref[b, :, H_M + h:H_M + h + 1] for b, h in chains]
    m_st = [m_ref[b, h:h + 1, 0:1] for b, h in chains]
    dlog = _each(lambda br, cc: jnp.where(valid, br + cc, NEG_INF), b_row, c_col)
    a_row = _each(lambda br, m: br + m, b_row, m_st)
    m_t = _each(lambda a, d: jnp.maximum(a, jnp.max(d, axis=0, keepdims=True)), a_row, dlog)
    sw = _each(lambda s, d, m: (s * jnp.exp(d - m)).astype(BF16), sT, dlog, m_t)
    w_inter = _each(lambda a, m: jnp.exp(a - m), a_row, m_t)
    floor = _each(lambda m: jnp.exp(-m), m_t)

    b_end = _each(lambda br: br[:, CHUNK - 1:CHUNK], b_row)
    a_end = _each(lambda be, m: be + m, b_end, m_st)
    wlog = _each(lambda be, br, ir: be - br + ir, b_end, b_row, ig_row)
    m_new = _each(lambda a, w: jnp.maximum(a, jnp.max(w, axis=1, keepdims=True)), a_end, wlog)
    ws = _each(lambda w, m: jnp.exp(w - m), wlog, m_new)
    decay = _each(lambda a, m: jnp.exp(a - m), a_end, m_new)
    vw = _each(lambda v, w: (v.astype(F32) * w).astype(BF16), vT, ws)

    r1T = _each(_dot, vT, sw)
    upd = _each(_dot, vw, k)

    num = _each(lambda w, a, c: w * a[:DV_M] + c[:DV_M], w_inter, r2T, r1T)
    den = _each(lambda w, a, c: w * a[DV_M:DV_M + 1] + c[DV_M:DV_M + 1], w_inter, r2T, r1T)
    hh = _each(lambda n, d, f: n / jnp.maximum(jnp.abs(d), f), num, den, floor)
    scale = _each(lambda x: lax.rsqrt(jnp.mean(x * x, axis=0, keepdims=True) + EPS), hh)
    gain = [jnp.broadcast_to(gmh_ref[sl(h), :], (DV_M, CHUNK)) for h in range(H_M)]
    for (b, h), x, sc, dc, u, mn in zip(chains, hh, scale, decay, upd, m_new):
        hmT_ref[b, sl(h), :] = (x * sc * gain[h] * smoT_ref[b, sl(h), :].astype(F32)).astype(BF16)
        caugT_ref[b, h] = dc * caugT_ref[b, h] + u
        m_ref[b, h:h + 1, :] = jnp.broadcast_to(mn, (1, LANES))


def _mlstm_prompt(mq, mk, mvT, smoT, c_tok, cT, gmh_col):
    b, s, _ = mq.shape
    tok = lambda w: pl.BlockSpec((b, CHUNK, w), lambda c: (0, c, 0))
    hmaj = lambda r: pl.BlockSpec((b, r, CHUNK), lambda c: (0, 0, c))
    whole = lambda shape: pl.BlockSpec(shape, lambda c: (0,) * len(shape))
    state = (b, H_M, MV_ROWS, DK_M)
    return pl.pallas_call(
        _mlstm_prompt_kernel,
        grid=(s // CHUNK,),
        in_specs=[tok(W_M), tok(W_M), hmaj(H_M * MV_ROWS), hmaj(W_M), tok(GATE_ROWS), hmaj(GATE_ROWS),
                  whole(gmh_col.shape)],
        out_specs=(hmaj(W_M), whole(state), whole((b, 8, LANES))),
        out_shape=(jax.ShapeDtypeStruct((b, W_M, s), BF16),
                   jax.ShapeDtypeStruct(state, F32),
                   jax.ShapeDtypeStruct((b, 8, LANES), F32)),
        compiler_params=_params("arbitrary"),
        name="mlstm_prompt",
    )(mq, mk, mvT, smoT, c_tok, cT, gmh_col)


HEAD_GROUP = 2
LOOKAHEAD = 1
QT = 256


def _fox_prompt_kernel(fqT_ref, fk_ref, e_ref, vT_ref, o_ref, qa_sc, m_sc, acc_sc, *, tq, tk):
    qi = pl.program_id(1)
    ratio = tq // tk
    nqt = tq // QT
    rowq = lax.broadcasted_iota(jnp.int32, (2 * DH_F, QT), 0)
    srow = lax.broadcasted_iota(jnp.int32, (tk, QT), 0)
    tcol = lax.broadcasted_iota(jnp.int32, (tk, QT), 1)
    hsl = lambda h: slice(h * DH_F, (h + 1) * DH_F)
    psl = lambda h: slice((h // 2) * 2 * DH_F, (h // 2 + 1) * 2 * DH_F)
    qsl = lambda n: slice(n * QT, (n + 1) * QT)

    for h in range(H_F):
        lo = (h % 2) * DH_F
        sel = jnp.where((rowq % H_F == h) & (rowq < N_SPLIT * H_F), -1.0, 0.0).astype(BF16)
        for n in range(nqt):
            q2 = fqT_ref[0, psl(h), qsl(n)]
            qh = jnp.where((rowq >= lo) & (rowq < lo + DH_F), q2, 0)
            qa_sc[h, n] = jnp.concatenate([qh, sel], axis=0)
    m_sc[...] = jnp.full_like(m_sc, NEG_INF)
    acc_sc[...] = jnp.zeros_like(acc_sc)
    groups = [range(g0, g0 + HEAD_GROUP) for g0 in range(0, H_F, HEAD_GROUP)]

    def run(tiles):
        work = [(ti, heads) for ti in range(len(tiles)) for heads in groups]
        k0 = [pl.multiple_of(j * tk, tk) for j, _ in tiles]
        e = [e_ref[pl.ds(k, tk), :] for k in k0]
        halves = [range(nqt) if d is None else range(d // QT, nqt) for _, d in tiles]
        st = {}

        def scores(ti, heads):
            for h in heads:
                ka = jnp.concatenate([fk_ref[pl.ds(k0[ti], tk), psl(h)], e[ti]], axis=1)
                for n in halves[ti]:
                    st[ti, h, n] = _dot(ka, qa_sc[h, n])

        for w in work[:LOOKAHEAD]:
            scores(*w)
        for wi, (ti, heads) in enumerate(work):
            if wi + LOOKAHEAD < len(work):
                scores(*work[wi + LOOKAHEAD])
            units = [(h, n) for h in heads for n in halves[ti]]
            p, alpha = {}, {}
            for h, n in units:
                s_u = st.pop((ti, h, n))
                d = tiles[ti][1]
                if d is not None and d + tk > n * QT:
                    s_u = jnp.where(srow + d <= tcol + n * QT, s_u, NEG_INF)
                m_old = m_sc[h, n]
                m_new = jnp.maximum(m_old, jnp.max(s_u, axis=0, keepdims=True))
                alpha[h, n] = jnp.exp2(m_old - m_new)
                p[h, n] = jnp.exp2(s_u - m_new).astype(BF16)
                m_sc[h, n] = m_new
            for h, n in units:
                vT = vT_ref[0, h * V_ROWS:(h + 1) * V_ROWS, pl.ds(k0[ti], tk)]
                acc_sc[h, n] = alpha[h, n] * acc_sc[h, n] + _dot(vT, p[h, n])

    def body(jj, carry):
        run([(jj * 2 * ratio + r, None) for r in range(2 * ratio)])
        return carry

    lax.fori_loop(0, qi // 2, body, 0)

    @pl.when(qi % 2 == 1)
    def _():
        run([((qi - 1) * ratio + r, None) for r in range(ratio)])

    run([(qi * ratio + r, r * tk) for r in range(ratio)])
    for h in range(H_F):
        for n in range(nqt):
            o_ref[0, hsl(h), qsl(n)] = (acc_sc[h, n, :DH_F, :] / acc_sc[h, n, DH_F:DH_F + 1, :]).astype(BF16)


def _fox_prompt(fqT, fk_tok, e_tok, fvT16, tq, tk):
    b, w, s = fqT.shape
    kern = functools.partial(_fox_prompt_kernel, tq=tq, tk=tk)
    return pl.pallas_call(
        kern,
        grid=(b, s // tq),
        in_specs=[pl.BlockSpec((1, w, tq), lambda bi, qi: (bi, 0, qi)),
                  pl.BlockSpec((s, w), lambda bi, qi: (bi, 0)),
                  pl.BlockSpec((s, GATE_ROWS), lambda bi, qi: (bi, 0)),
                  pl.BlockSpec((1, H_F * V_ROWS, s), lambda bi, qi: (bi, 0, 0))],
        out_specs=pl.BlockSpec((1, w, tq), lambda bi, qi: (bi, 0, qi)),
        out_shape=jax.ShapeDtypeStruct((b, w, s), BF16),
        scratch_shapes=[pltpu.VMEM((H_F, tq // QT, 2 * DH_F + GATE_ROWS, QT), BF16),
                        pltpu.VMEM((H_F, tq // QT, 1, QT), F32),
                        pltpu.VMEM((H_F, tq // QT, V_ROWS, QT), F32)],
        compiler_params=_params("parallel", "arbitrary"),
        name="fox_prompt",
    )(fqT, fk_tok, e_tok, fvT16)


FF_TILE = MXU_TILE
DOWN_TILE = 2 * MXU_TILE


def _merge_ffn_kernel(*refs, fox):
    refs = list(refs)
    pt_ref = refs.pop(0) if fox else None
    (x_ref, hmT_ref, yfT_ref, sga_ref, sgb_ref, p_ref, wpa_ref, wpb_ref, wout_ref, gffn_ref, wgu_ref,
     wdown_ref, gple_ref, wpg_ref, wpp_ref) = refs[:15]
    del refs[:15]
    fox_in = [refs.pop(0) for _ in range(7)] if fox else None
    y_ref = refs.pop(0)
    yf_ref = refs.pop(0) if fox else None
    hdn_sc = refs.pop(0)
    val = {}

    def g_merge():
        a = _dot_tn(hmT_ref[0], wpa_ref[...])
        bb = _dot_tn(yfT_ref[0], wpb_ref[...])
        u = sga_ref[...].astype(F32) * a + sgb_ref[...].astype(F32) * bb
        x1 = x_ref[...] + _dot(u.astype(BF16), wout_ref[...])
        val["x1"] = x1
        val["xn"] = (x1 * _rms_scale(x1) * gffn_ref[...]).astype(BF16)

    def g_ffn(j):
        def run():
            g = _dot(val["xn"], wgu_ref[:, j:j + FF_TILE])
            up = _dot(val["xn"], wgu_ref[:, D_FF + j:D_FF + j + FF_TILE])
            hdn_sc[:, j:j + FF_TILE] = (jax.nn.silu(g) * up).astype(BF16)
        return run

    def g_down(c):
        def run():
            val["x2", c] = val["x1"][:, c:c + DOWN_TILE] + _dot(hdn_sc[...], wdown_ref[:, c:c + DOWN_TILE])
        return run

    def g_out():
        x2 = jnp.concatenate([val["x2", c] for c in range(0, D_MODEL, DOWN_TILE)], axis=1)
        xg = (x2 * _rms_scale(x2) * gple_ref[...]).astype(BF16)
        gate = jax.nn.sigmoid(_dot(xg, wpg_ref[...]))
        y_ref[...] = x2 + gate * _dot(p_ref[...].astype(BF16), wpp_ref[...])

    kt = lambda k: -(-k // MXU_TILE)
    groups = [(g_merge, 2 * kt(W_M) * kt(D_MODEL) + kt(D_MODEL) ** 2)]
    groups += [(g_ffn(j), 2 * kt(D_MODEL)) for j in range(0, D_FF, FF_TILE)]
    groups += [(g_down(c), kt(D_FF) * kt(DOWN_TILE)) for c in range(0, D_MODEL, DOWN_TILE)]
    groups += [(g_out, kt(D_MODEL) ** 2 + kt(PLE_DIM) * kt(D_MODEL))]
    if fox:
        _run_with_hosted_fox(groups, pl.program_id(0), fox, pt_ref, *fox_in, yf_ref, *refs)
    else:
        for g, _ in groups:
            g()


def _merge_ffn(x, hmT, yfT, sga, sgb, p, wpa, wpb, wout, gffn, wgu, wdown, gple, wpg, wpp, tm, fox=None):
    t, d = x.shape
    b, _, s = yfT.shape
    nt = s // tm
    steps = t // tm
    tok = lambda w: pl.BlockSpec((tm, w), lambda i, *_: (i, 0))
    full = lambda a: pl.BlockSpec(a.shape, lambda i, *_: (0,) * a.ndim)
    hmaj = lambda r: pl.BlockSpec((1, r, tm), lambda i, *_: (i // nt, 0, i % nt))
    in_specs = [tok(d), hmaj(W_M), hmaj(W_F), tok(d), tok(d), tok(PLE_DIM),
                full(wpa), full(wpb), full(wout), full(gffn), full(wgu), full(wdown),
                full(gple), full(wpg), full(wpp)]
    operands = (x, hmT, yfT, sga, sgb, p, wpa, wpb, wout, gffn, wgu, wdown, gple, wpg, wpp)
    hdn = pltpu.VMEM((tm, D_FF), BF16)
    if fox is None:
        return pl.pallas_call(
            functools.partial(_merge_ffn_kernel, fox=None),
            grid=(steps,),
            in_specs=in_specs,
            out_specs=tok(d),
            out_shape=jax.ShapeDtypeStruct((t, d), F32),
            scratch_shapes=[hdn],
            compiler_params=_params("parallel"),
            name="merge_ffn",
        )(*operands)

    page_table, fq, kn, vn, lfn, ck, cv, cl = fox
    db, n_tok, _ = fq.shape
    n_pages, page = page_table.shape[1], ck.shape[-1]
    per = db // steps
    assert per * steps == db and per % 2 == 0
    seqs = lambda a: pl.BlockSpec((per,) + a.shape[1:], lambda i, *_: (i,) + (0,) * (a.ndim - 1))
    hbm = pl.BlockSpec(memory_space=pl.ANY)
    return pl.pallas_call(
        functools.partial(_merge_ffn_kernel, fox=(per, db, n_pages, n_tok)),
        grid_spec=pltpu.PrefetchScalarGridSpec(
            num_scalar_prefetch=1, grid=(steps,),
            in_specs=in_specs + [seqs(fq), seqs(kn), seqs(vn), seqs(lfn), hbm, hbm, hbm],
            out_specs=(tok(d), seqs(fq)),
            scratch_shapes=[hdn,
                            pltpu.VMEM((2, n_pages, W_F, page), F32),
                            pltpu.VMEM((2, n_pages, W_F, page), F32),
                            pltpu.VMEM((2, n_pages, H_F, page), F32),
                            pltpu.SemaphoreType.DMA((2,))]),
        out_shape=(jax.ShapeDtypeStruct((t, d), F32), jax.ShapeDtypeStruct(fq.shape, F32)),
        compiler_params=_params("arbitrary", vmem=VMEM_LIMIT_HOST),
        name="merge_ffn_fox_sample",
    )(page_table, *operands, fq, kn, vn, lfn, ck, cv, cl)


_SPLIT = (W_M, W_M, W_M, W_M, H_M, H_M, W_F, W_F, W_F, H_F, D_MODEL, D_MODEL)


def _prep_weights(g_mix, w_in, b_mi, b_mf, b_ff, g_mh, g_qn, g_kn, w_pa, w_pb, w_out, g_ffn, w_gu,
                  w_down, g_ple, w_ple_gate, w_ple_proj):
    offs = [0]
    for sz in _SPLIT:
        offs.append(offs[-1] + sz)
    o_mq, _, o_mv, o_mo, o_mi, o_mf, o_fq, _, _, o_ff, o_ga, _, o_end = offs
    w = w_in[0]
    wT = w.T
    wn = jnp.concatenate([wT[o_mq:o_mv], wT[o_mo:o_mi], wT[o_ga:o_end]], axis=0).astype(BF16)
    wt = jnp.concatenate([wT[o_fq:o_ff], wT[o_mv:o_mo], wT[o_mi:o_fq], wT[o_ff:o_ga],
                          jnp.zeros((GATE_ROWS - N_GATES, D_MODEL), F32)], axis=0).astype(BF16)
    bias_col = jnp.concatenate([b_mi[0], b_mf[0], b_ff[0], jnp.zeros((GATE_ROWS - N_GATES,), F32)])[:, None]
    gain_col = jnp.concatenate([jnp.tile(g_qn[0], H_F) * (F_SCALE * LOG2E), jnp.tile(g_kn[0], H_F)])[:, None]
    return dict(
        g_mix=g_mix, wn=wn, wt=wt, bias_col=bias_col, gain_col=gain_col,
        gmh=g_mh[0].reshape(1, W_M),
        wpa=w_pa[0].astype(BF16), wpb=w_pb[0].astype(BF16), wout=w_out[0].astype(BF16),
        gffn=g_ffn, wgu=w_gu[0].astype(BF16), wdown=w_down[0].astype(BF16),
        gple=g_ple, wpg=w_ple_gate[0].astype(BF16), wpp=w_ple_proj[0].astype(BF16))


def _prompt_path(x, p, wd, tm_in, tm_ffn, tq, tk, fox):
    b, s, d = x.shape
    (mq, mk, mvT, smoT, sga, sgb, fqT, fkT, fvT, fk_tok, fvT16, gT) = _in_proj(
        x, wd["g_mix"], wd["wn"], wd["wt"], wd["bias_col"], wd["gain_col"], tm_in)
    cT, c_tok, e_tok = _gate_scan(gT)
    seq3 = lambda a: a.reshape(b, s, a.shape[-1])
    hmT, caugT, m_out = _mlstm_prompt(seq3(mq), seq3(mk), mvT, smoT, seq3(c_tok), cT, wd["gmh"].reshape(W_M, 1))
    yfT = _fox_prompt(fqT, fk_tok, e_tok, fvT16, tq, tk)
    y, yf_sample = _merge_ffn(
        x.reshape(b * s, d), hmT, yfT, sga, sgb, p.reshape(b * s, PLE_DIM),
        wd["wpa"], wd["wpb"], wd["wout"], wd["gffn"], wd["wgu"], wd["wdown"],
        wd["gple"], wd["wpg"], wd["wpp"], tm_ffn, fox)
    to_cache = lambda aT: aT.reshape(b, H_F, DH_F, s).transpose(0, 3, 1, 2)[None]
    logf = gT[:, 2 * H_M:N_GATES, :].transpose(0, 2, 1)[None]
    c_out = caugT[:, :, :DV_M, :].transpose(0, 1, 3, 2)[None]
    return (y.reshape(b, s, d), to_cache(fkT), to_cache(fvT), logf,
            c_out, caugT[None, :, :, DV_M, :], m_out[None, :, :H_M, 0]), yf_sample


L_PAD = 16
SEQ_BLOCK = 8


def _mlstm_sample_kernel(q_ref, k_ref, v_ref, smo_ref, grow_ref, gcol_ref, c_ref, n_ref, m_ref, gmh_ref,
                         hm_ref, co_ref, no_ref, mo_ref, *, n_valid):
    length = L_PAD
    row = lax.broadcasted_iota(jnp.int32, (length, length), 0)
    col = lax.broadcasted_iota(jnp.int32, (length, length), 1)
    valid = (col <= row) & (col < n_valid)
    chains = [(i, h) for i in range(SEQ_BLOCK) for h in range(H_M)]
    sl = lambda h: slice(h * DK_M, (h + 1) * DK_M)

    q = [q_ref[i, :, sl(h)] for i, h in chains]
    k = [k_ref[i, :, sl(h)] for i, h in chains]
    v = [v_ref[i, :, sl(h)] for i, h in chains]
    n_st = [n_ref[i, h:h + 1, :] for i, h in chains]
    s_mat = _each(_dot_nt, q, k)
    r2 = _each(lambda qq, ih: _dot(qq, c_ref[ih[0], ih[1]].astype(BF16)), q, chains)
    qn = _each(lambda qq, n: jnp.sum(qq.astype(F32) * n, axis=1, keepdims=True), q, n_st)

    lf_row = [grow_ref[i, H_M + h:H_M + h + 1, 0:length] for i, h in chains]
    lf_col = [gcol_ref[i, :, H_M + h:H_M + h + 1] for i, h in chains]
    b_col = _each(lambda r: jnp.sum(jnp.where(col <= row, r, 0.0), axis=1, keepdims=True), lf_row)
    b_row = _each(lambda c: jnp.sum(jnp.where(row <= col, c, 0.0), axis=0, keepdims=True), lf_col)
    w_intra, w_inter, floor, ws, decay, m_new = _mlstm_gates(
        [grow_ref[i, h:h + 1, 0:length] for i, h in chains], b_row,
        [gcol_ref[i, :, h:h + 1] for i, h in chains], b_col,
        [m_ref[i:i + 1, h:h + 1] for i, h in chains], valid, n_valid)
    sw = _each(lambda s, w: s * w, s_mat, w_intra)
    kw = _each(lambda kk, w: kk.astype(F32) * w, k, ws)

    r1 = _each(lambda s, vv: _dot(s.astype(BF16), vv), sw, v)
    upd = _each(lambda kk, vv: _dot_tn(kk.astype(BF16), vv), kw, v)
    sw_sum = _each(lambda s: jnp.sum(s, axis=1, keepdims=True), sw)
    kw_sum = _each(lambda kk: jnp.sum(kk, axis=0, keepdims=True), kw)

    num = _each(lambda w, a, c: w * a + c, w_inter, r2, r1)
    den = _each(lambda w, a, c: w * a + c, w_inter, qn, sw_sum)
    hh = _each(lambda n, d, f: n / jnp.maximum(jnp.abs(d), f), num, den, floor)
    scale = _each(_rms_scale, hh)
    for (i, h), x, sc, dc, u, n, ks, mn in zip(chains, hh, scale, decay, upd, n_st, kw_sum, m_new):
        hm_ref[i, :, sl(h)] = (x * sc * gmh_ref[:, sl(h)] * smo_ref[i, :, sl(h)].astype(F32)).astype(BF16)
        co_ref[i, h] = dc * c_ref[i, h] + u
        no_ref[i, h:h + 1, :] = dc * n + ks
        mo_ref[i:i + 1, h:h + 1] = mn


def _mlstm_sample(q, k, v, smo, grow, gcol, c_st, n_st, m_st, gmh, n_valid):
    db = q.shape[0]
    blk = lambda a: pl.BlockSpec((SEQ_BLOCK,) + a.shape[1:], lambda i: (i,) + (0,) * (a.ndim - 1))
    kern = functools.partial(_mlstm_sample_kernel, n_valid=n_valid)
    return pl.pallas_call(
        kern,
        grid=(db // SEQ_BLOCK,),
        in_specs=[blk(q), blk(k), blk(v), blk(smo), blk(grow), blk(gcol), blk(c_st), blk(n_st), blk(m_st),
                  pl.BlockSpec(gmh.shape, lambda i: (0, 0))],
        out_specs=(blk(q), blk(c_st), blk(n_st), blk(m_st)),
        out_shape=(jax.ShapeDtypeStruct(q.shape, BF16), jax.ShapeDtypeStruct(c_st.shape, F32),
                   jax.ShapeDtypeStruct(n_st.shape, F32), jax.ShapeDtypeStruct(m_st.shape, F32)),
        compiler_params=_params("parallel"),
        name="mlstm_sample",
    )(q, k, v, smo, grow, gcol, c_st, n_st, m_st, gmh)


def _run_with_hosted_fox(groups, step, fox, pt_ref, fq_ref, kn_ref, vn_ref, lfn_ref, ck_hbm, cv_hbm, cl_hbm,
                         yf_ref, kbuf, vbuf, lbuf, sem):
    per, total, n_pages, n_tok = fox
    first = step * per

    def page_copies(n, half):
        copies = []
        for j in range(n_pages):
            pid = pt_ref[n, j]
            copies.append(pltpu.make_async_copy(ck_hbm.at[pid], kbuf.at[half, j], sem.at[half]))
            copies.append(pltpu.make_async_copy(cv_hbm.at[pid], vbuf.at[half, j], sem.at[half]))
            copies.append(pltpu.make_async_copy(cl_hbm.at[pid], lbuf.at[half, j], sem.at[half]))
        return copies

    @pl.when(step == 0)
    def _():
        for c in page_copies(0, 0):
            c.start()

    total_w = sum(w for _, w in groups)
    runs, acc_w = [[] for _ in range(per)], 0
    for g, w in groups:
        runs[min(per - 1, per * (2 * acc_w + w) // (2 * total_w))].append(g)
        acc_w += w
    assert all(runs)
    for j in range(per):
        n, half = first + j, j % 2
        mine = runs[j]

        def prefetch():
            for c in page_copies(n + 1, 1 - half):
                c.start()

        if j + 1 < per:
            prefetch()
        else:
            pl.when(n + 1 < total)(prefetch)

        for c in page_copies(n, half):
            c.wait()
        scores = _fox_sample_scores(fq_ref[j], kn_ref[j], lfn_ref[j], lambda pg: kbuf[half, pg],
                                    lbuf[half].reshape(n_pages * H_F, LANES), n_pages, n_tok)
        for g in mine[:-1]:
            g()
        yf_ref[j] = _fox_sample_output(*scores, lambda pg: vbuf[half, pg], vn_ref[j], n_pages, n_tok)
        for g in mine[-1:]:
            g()


def _own_lanes():
    sub = lax.broadcasted_iota(jnp.int32, (H_F, W_F), 0)
    head_of_lane = lax.broadcasted_iota(jnp.int32, (H_F, W_F), 1) // DH_F
    return sub == head_of_lane


def _fox_sample_scores(fq, kn_new, lfn, k_page, lf_pages, n_pages, n_tok):
    rows = n_tok * H_F
    own = _own_lanes()
    tile = lambda a: jnp.concatenate([a] * n_tok, axis=0)
    qbd = jnp.concatenate(
        [jnp.where(own, jnp.broadcast_to(fq[t:t + 1, :], (H_F, W_F)), 0.0) for t in range(n_tok)], axis=0)
    qbd16 = qbd.astype(BF16)

    incl_all = _lane_cumsum(lf_pages)
    bias = [None] * n_pages
    carry = jnp.zeros((H_F, 1), F32)
    for pg in reversed(range(n_pages)):
        incl = incl_all[pg * H_F:(pg + 1) * H_F, :]
        tot = incl[:, LANES - 1:LANES]
        bias[pg] = (tot - incl + carry) * LOG2E
        carry = carry + tot

    logits = []
    for j in range(0, n_pages, 2):
        kcat = jnp.concatenate([k_page(j).astype(BF16), k_page(j + 1).astype(BF16)], axis=1)
        bcat = jnp.concatenate([tile(bias[j]), tile(bias[j + 1])], axis=1)
        logits.append(_dot(qbd16, kcat) + bcat)

    rowi = lax.broadcasted_iota(jnp.int32, (rows, 1), 0)
    qf = qbd16.astype(F32)
    new_logits = []
    f_new = jnp.zeros((H_F, 1), F32)
    for s in range(n_tok):
        f_new = f_new + lfn[:, s:s + 1]
        kn = kn_new[s:s + 1, :].astype(BF16).astype(F32)
        ln = jnp.sum(qf * kn, axis=1, keepdims=True) - tile(f_new) * LOG2E
        new_logits.append(jnp.where(rowi >= s * H_F, ln, NEG_INF))

    m = new_logits[0]
    for ln in new_logits[1:]:
        m = jnp.maximum(m, ln)
    for lg in logits:
        m = jnp.maximum(m, jnp.max(lg, axis=1, keepdims=True))
    return logits, new_logits, m


def _fox_sample_output(logits, new_logits, m, v_page, vn_new, n_pages, n_tok):
    rows = n_tok * H_F
    own = _own_lanes()
    denom = jnp.zeros((rows, 1), F32)
    acc = jnp.zeros((rows, W_F), F32)
    for idx, lg in enumerate(logits):
        p = jnp.exp2(lg - m)
        denom = denom + jnp.sum(p, axis=1, keepdims=True)
        j = 2 * idx
        vcat = jnp.concatenate([v_page(j).astype(BF16), v_page(j + 1).astype(BF16)], axis=1)
        acc = acc + _dot_nt(p.astype(BF16), vcat)
    for s, ln in enumerate(new_logits):
        p = jnp.exp2(ln - m)
        denom = denom + p
        acc = acc + p.astype(BF16).astype(F32) * vn_new[s:s + 1, :]
    out = acc / denom
    return jnp.concatenate(
        [jnp.sum(jnp.where(own, out[t * H_F:(t + 1) * H_F, :], 0.0), axis=0, keepdims=True)
         for t in range(n_tok)], axis=0)


def _sample_front(x, cache_k, cache_v, cache_logf, state_c, state_n, state_m, page_table, wd):
    db, n_tok, d = x.shape
    t = db * n_tok
    (mq, mk, mvT, smoT, sga, sgb, fqT, fkT, fvT, fk_tok, _, gT) = _in_proj(
        x.reshape(1, t, d), wd["g_mix"], wd["wn"], wd["wt"], wd["bias_col"], wd["gain_col"], t)
    mv = mvT[0].reshape(H_M, MV_ROWS, t)[:, :DV_M].reshape(W_M, t).T
    smo = smoT[0].T

    pad_tok = lambda a: jnp.pad(a.reshape(db, n_tok, -1), ((0, 0), (0, L_PAD - n_tok), (0, 0)))
    g16 = gT[0, :2 * 8].reshape(2 * 8, db, n_tok)
    grow = jnp.pad(g16.transpose(1, 0, 2), ((0, 0), (0, 0), (0, LANES - n_tok)))
    gcol = jnp.pad(g16.transpose(1, 2, 0), ((0, 0), (0, L_PAD - n_tok), (0, LANES - 2 * 8)))
    hm, c_new, n_new, m_new = _mlstm_sample(
        pad_tok(mq), pad_tok(mk), pad_tok(mv), pad_tok(smo), grow, gcol,
        state_c, state_n, state_m, wd["gmh"], n_tok)
    hmT = hm[:, :n_tok].reshape(t, W_M).T[None]

    pool, page = cache_k.shape[0], cache_k.shape[1]
    ck = cache_k.transpose(0, 2, 3, 1).reshape(pool, W_F, page)
    cv = cache_v.transpose(0, 2, 3, 1).reshape(pool, W_F, page)
    cl = cache_logf.transpose(0, 2, 1)
    tok3 = lambda aT: aT[0].T.astype(F32).reshape(db, n_tok, W_F)
    lfn = jnp.pad(gT[0, 2 * H_M:N_GATES].reshape(H_F, db, n_tok).transpose(1, 0, 2),
                  ((0, 0), (0, 0), (0, LANES - n_tok)))
    fox = (page_table, tok3(fqT), fk_tok.astype(F32).reshape(db, n_tok, W_F), tok3(fvT), lfn, ck, cv, cl)
    to_cache = lambda aT: aT[0].reshape(H_F, DH_F, db, n_tok).transpose(2, 3, 0, 1)[None]
    logf = gT[0, 2 * H_M:N_GATES].reshape(H_F, db, n_tok).transpose(1, 2, 0)[None]
    outs = (to_cache(fkT), to_cache(fvT), logf, c_new[None], n_new[None], m_new[None])
    return fox, (hmT, sga, sgb), outs


def _sample_back(x, p, yf, hmT, sga, sgb, wd):
    db, n_tok, d = x.shape
    t = db * n_tok
    yfT = yf.reshape(t, W_F).T.astype(BF16)[None]
    y = _merge_ffn(x.reshape(t, d), hmT, yfT, sga, sgb, p.reshape(t, PLE_DIM),
                   wd["wpa"], wd["wpb"], wd["wout"], wd["gffn"], wd["wgu"], wd["wdown"],
                   wd["gple"], wd["wpg"], wd["wpp"], t)
    return y.reshape(db, n_tok, d)


def kernel(x_prompt, x_sample, p_prompt, p_sample, cache_k, cache_v, cache_logf, state_C, state_n, state_m,
           page_table, g_mix, w_in, b_mi, b_mf, b_ff, g_mh, g_qn, g_kn, w_pa, w_pb, w_out, g_ffn, w_gu,
           w_down, g_ple, w_ple_gate, w_ple_proj):
    wd = _prep_weights(g_mix, w_in, b_mi, b_mf, b_ff, g_mh, g_qn, g_kn, w_pa, w_pb, w_out, g_ffn, w_gu,
                       w_down, g_ple, w_ple_gate, w_ple_proj)
    fox, mid, (ks, vs, lfs, cs, ns, ms) = _sample_front(
        x_sample, cache_k[0], cache_v[0], cache_logf[0], state_C[0], state_n[0], state_m[0], page_table, wd)
    (yp, kp, vp, lfp, cp, np_, mp), yf_sample = _prompt_path(
        x_prompt, p_prompt[0], wd, PROJ_TILE, TOKEN_TILE, QUERY_TILE, KEY_TILE, fox)
    ys = _sample_back(x_sample, p_sample[0], yf_sample, *mid, wd)
    return (yp, ys, kp, vp, lfp, cp, np_, mp, ks, vs, lfs, cs, ns, ms)
```

```python
import functools

import jax
import jax.numpy as jnp
from jax import lax
from jax.experimental import pallas as pl
from jax.experimental.pallas import tpu as pltpu

D_MODEL = 1024
H_M = 4
DK_M = 128
DV_M = 128
W_M = H_M * DV_M
H_F = 8
DH_F = 64
W_F = H_F * DH_F
D_FF = 2816
PLE_DIM = 256
CHUNK = 128
EPS = 1e-6
F_SCALE = DH_F ** -0.5
K_SCALE_M = DK_M ** -0.5
LOG2E = 1.4426950408889634
N_GATES = 2 * H_M + H_F
GATE_ROWS = 128
N_SPLIT = 3
V_ROWS = DH_F + 16
MV_ROWS = DV_M + 16
LANES = 128
MXU_TILE = 256
TOKEN_TILE = 2 * MXU_TILE
PROJ_TILE = 4 * MXU_TILE
QUERY_TILE = 2 * MXU_TILE
KEY_TILE = MXU_TILE
VMEM_LIMIT = 56 * 1024 * 1024
VMEM_LIMIT_HOST = 62 * 1024 * 1024

BF16 = jnp.bfloat16
F32 = jnp.float32
NEG_INF = float("-inf")


def _dot(a, b):
    return jnp.dot(a, b, preferred_element_type=F32)


def _dot_nt(a, b):
    return lax.dot_general(a, b, (((1,), (1,)), ((), ())), preferred_element_type=F32)


def _dot_tn(a, b):
    return lax.dot_general(a, b, (((0,), (0,)), ((), ())), preferred_element_type=F32)


def _log_sigmoid(u):
    return jnp.minimum(u, 0.0) - jnp.log1p(jnp.exp(-jnp.abs(u)))


def _rms_scale(x):
    return lax.rsqrt(jnp.mean(x * x, axis=-1, keepdims=True) + EPS)


def _params(*sem, vmem=VMEM_LIMIT):
    return pltpu.CompilerParams(dimension_semantics=sem, vmem_limit_bytes=vmem)


def _in_proj_kernel(x_ref, gmix_ref, wn_ref, wt_ref, bias_ref, gain_ref,
                    mq_ref, mk_ref, mvT_ref, smoT_ref, sga_ref, sgb_ref,
                    fqT_ref, fkT_ref, fvT_ref, fk_ref, fvT16_ref, gT_ref):
    x = x_ref[0]
    xn = (x * _rms_scale(x) * gmix_ref[...]).astype(BF16)

    def zn(lo, hi):
        return _dot_nt(xn, wn_ref[lo:hi, :])

    def zt(lo, hi):
        return _dot_nt(wt_ref[lo:hi, :], xn)

    def head_norm(z, gain):
        tm = z.shape[-1]
        r = z.reshape(H_F, DH_F, tm)
        r = r * lax.rsqrt(jnp.mean(r * r, axis=1, keepdims=True) + EPS)
        return r.reshape(W_F, tm) * gain

    def g_mq():
        mq_ref[...] = zn(0, W_M).astype(BF16)

    def g_mk():
        mk_ref[...] = (zn(W_M, 2 * W_M) * K_SCALE_M).astype(BF16)

    def g_mv():
        vT = zt(3 * W_F, 3 * W_F + W_M)
        tm = vT.shape[-1]
        v_aug = jnp.concatenate([vT.reshape(H_M, DV_M, tm), jnp.ones((H_M, MV_ROWS - DV_M, tm), F32)], axis=1)
        mvT_ref[0] = v_aug.reshape(H_M * MV_ROWS, tm).astype(BF16)

    def g_smo():
        smoT_ref[0] = jax.nn.sigmoid(zn(2 * W_M, 3 * W_M)).T.astype(BF16)

    def g_gate(ref, off, c):
        def run():
            ref[:, c:c + W_M] = jax.nn.sigmoid(zn(off + c, off + c + W_M)).astype(BF16)
        return run

    def g_fq():
        fqT_ref[0] = head_norm(zt(0, W_F), gain_ref[0:W_F, :]).astype(BF16)

    def g_fk():
        fk = head_norm(zt(W_F, 2 * W_F), gain_ref[W_F:2 * W_F, :])
        fkT_ref[0] = fk
        fk_ref[...] = fk.T.astype(BF16)

    def g_fv():
        fv = zt(2 * W_F, 3 * W_F)
        fvT_ref[0] = fv
        tm = fv.shape[-1]
        fv_aug = jnp.concatenate([fv.reshape(H_F, DH_F, tm), jnp.ones((H_F, V_ROWS - DH_F, tm), F32)], axis=1)
        fvT16_ref[0] = fv_aug.reshape(H_F * V_ROWS, tm).astype(BF16)

    def g_gates():
        u = zt(3 * W_F + W_M, 3 * W_F + W_M + GATE_ROWS) + bias_ref[...]
        row = lax.broadcasted_iota(jnp.int32, u.shape, 0)
        gT_ref[0] = jnp.where(row < H_M, u, jnp.where(row < N_GATES, _log_sigmoid(u), 0.0))

    g_mq()
    g_mk()
    g_mv()
    g_smo()
    for ref, off in ((sga_ref, 3 * W_M), (sgb_ref, 3 * W_M + D_MODEL)):
        for c in range(0, D_MODEL, W_M):
            g_gate(ref, off, c)()
    g_fq()
    g_fk()
    g_fv()
    g_gates()


def _in_proj(x, g_mix, wn, wt, bias_col, gain_col, tm):
    b, s, d = x.shape
    nt = s // tm
    t = b * s
    tok = lambda w: pl.BlockSpec((tm, w), lambda bi, i: (bi * nt + i, 0))
    hm = lambda r: pl.BlockSpec((1, r, tm), lambda bi, i: (bi, 0, i))
    full = lambda a: pl.BlockSpec(a.shape, lambda bi, i: (0,) * a.ndim)
    out_shape = (
        jax.ShapeDtypeStruct((t, W_M), BF16),
        jax.ShapeDtypeStruct((t, W_M), BF16),
        jax.ShapeDtypeStruct((b, H_M * MV_ROWS, s), BF16),
        jax.ShapeDtypeStruct((b, W_M, s), BF16),
        jax.ShapeDtypeStruct((t, D_MODEL), BF16),
        jax.ShapeDtypeStruct((t, D_MODEL), BF16),
        jax.ShapeDtypeStruct((b, W_F, s), BF16),
        jax.ShapeDtypeStruct((b, W_F, s), F32),
        jax.ShapeDtypeStruct((b, W_F, s), F32),
        jax.ShapeDtypeStruct((t, W_F), BF16),
        jax.ShapeDtypeStruct((b, H_F * V_ROWS, s), BF16),
        jax.ShapeDtypeStruct((b, GATE_ROWS, s), F32),
    )
    out_specs = (tok(W_M), tok(W_M), hm(H_M * MV_ROWS), hm(W_M), tok(D_MODEL), tok(D_MODEL),
                 hm(W_F), hm(W_F), hm(W_F), tok(W_F), hm(H_F * V_ROWS), hm(GATE_ROWS))
    return pl.pallas_call(
        _in_proj_kernel,
        grid=(b, nt),
        in_specs=[pl.BlockSpec((1, tm, d), lambda bi, i: (bi, i, 0)),
                  full(g_mix), full(wn), full(wt), full(bias_col), full(gain_col)],
        out_specs=out_specs,
        out_shape=out_shape,
        compiler_params=_params("parallel", "parallel"),
        name="in_proj",
    )(x, g_mix, wn, wt, bias_col, gain_col)


def _lane_cumsum(x):
    lane = lax.broadcasted_iota(jnp.int32, x.shape, x.ndim - 1)
    sh = 1
    while sh < x.shape[-1]:
        x = x + jnp.where(lane >= sh, pltpu.roll(x, sh, x.ndim - 1), 0.0)
        sh *= 2
    return x


def _split_bf16(x):
    pieces = []
    for _ in range(N_SPLIT):
        p = x.astype(BF16).astype(F32)
        pieces.append(p)
        x = x - p
    return pieces


def _gate_scan_kernel(gT_ref, cT_ref, c_ref, e_ref):
    s = gT_ref.shape[-1]
    rows = 2 * 8
    row = lax.broadcasted_iota(jnp.int32, (rows, CHUNK), 0)
    pad = jnp.zeros((GATE_ROWS - rows, CHUNK), F32)
    epad = jnp.zeros((GATE_ROWS - N_SPLIT * H_F, CHUNK), F32)
    sls = [slice(c * CHUNK, (c + 1) * CHUNK) for c in range(s // CHUNK)]
    g = [gT_ref[0, 0:rows, sl] for sl in sls]
    lane = lax.broadcasted_iota(jnp.int32, (rows, CHUNK), 1)
    cs = g
    sh = 1
    while sh < CHUNK:
        cs = _each(lambda x: x + jnp.where(lane >= sh, pltpu.roll(x, sh, 1), 0.0), cs)
        sh *= 2
    carry = jnp.zeros((rows, 1), F32)
    outs = []
    for gc, csc in zip(g, cs):
        outs.append(jnp.where(row < H_M, gc, jnp.where(row < 2 * H_M, csc, csc + carry)))
        carry = carry + csc[:, CHUNK - 1:CHUNK]
    fulls = _each(lambda o: jnp.concatenate([o, pad], axis=0), outs)
    fullsT = _each(lambda f: f.T, fulls)
    pieces = _each(lambda o: jnp.concatenate(_split_bf16(o[2 * H_M:N_GATES, :] * LOG2E) + [epad], axis=0), outs)
    piecesT = _each(lambda p: p.T.astype(BF16), pieces)
    for sl, f, fT, pT in zip(sls, fulls, fullsT, piecesT):
        cT_ref[0, :, sl] = f
        c_ref[sl, :] = fT
        e_ref[sl, :] = pT


def _gate_scan(gT):
    b, r, s = gT.shape
    return pl.pallas_call(
        _gate_scan_kernel,
        grid=(b,),
        in_specs=[pl.BlockSpec((1, r, s), lambda bi: (bi, 0, 0))],
        out_specs=(pl.BlockSpec((1, r, s), lambda bi: (bi, 0, 0)),
                   pl.BlockSpec((s, r), lambda bi: (bi, 0)),
                   pl.BlockSpec((s, r), lambda bi: (bi, 0))),
        out_shape=(jax.ShapeDtypeStruct((b, r, s), F32), jax.ShapeDtypeStruct((b * s, r), F32),
                   jax.ShapeDtypeStruct((b * s, r), BF16)),
        compiler_params=_params("parallel"),
        name="gate_scan",
    )(gT)


def _each(f, *lists):
    return [f(*a) for a in zip(*lists)]


def _mlstm_gates(ig_row, b_row, ig_col, b_col, m_st, valid, n_valid):
    length = b_col[0].shape[0]
    dlog = _each(lambda bc, br, ir: jnp.where(valid, bc - br + ir, NEG_INF), b_col, b_row, ig_row)
    a_col = _each(lambda bc, m: bc + m, b_col, m_st)
    mx = _each(lambda d: jnp.max(d, axis=1, keepdims=True), dlog)
    m_t = _each(jnp.maximum, a_col, mx)
    w_intra = _each(lambda d, m: jnp.exp(d - m), dlog, m_t)
    w_inter = _each(lambda a, m: jnp.exp(a - m), a_col, m_t)
    floor = _each(lambda m: jnp.exp(-m), m_t)
    b_end = _each(lambda br: br[:, n_valid - 1:n_valid], b_row)
    a_end = _each(lambda be, m: be + m, b_end, m_st)
    wlog = _each(lambda be, bc, ic: be - bc + ic, b_end, b_col, ig_col)
    if n_valid < length:
        keep = lax.broadcasted_iota(jnp.int32, (length, 1), 0) < n_valid
        wlog = _each(lambda w: jnp.where(keep, w, NEG_INF), wlog)
    wmax = _each(lambda w: jnp.max(w, axis=0, keepdims=True), wlog)
    m_new = _each(jnp.maximum, a_end, wmax)
    ws = _each(lambda w, m: jnp.exp(w - m), wlog, m_new)
    decay = _each(lambda a, m: jnp.exp(a - m), a_end, m_new)
    return w_intra, w_inter, floor, ws, decay, m_new


def _mlstm_prompt_kernel(mq_ref, mk_ref, mvT_ref, smoT_ref, c_ref, cT_ref, gmh_ref,
                         hmT_ref, caugT_ref, m_ref):
    nb = mq_ref.shape[0]

    @pl.when(pl.program_id(0) == 0)
    def _():
        caugT_ref[...] = jnp.zeros_like(caugT_ref)
        m_ref[...] = jnp.zeros_like(m_ref)

    srow = lax.broadcasted_iota(jnp.int32, (CHUNK, CHUNK), 0)
    tcol = lax.broadcasted_iota(jnp.int32, (CHUNK, CHUNK), 1)
    valid = srow <= tcol
    chains = [(b, h) for b in range(nb) for h in range(H_M)]
    sl = lambda h: slice(h * DK_M, (h + 1) * DK_M)
    vsl = lambda h: slice(h * MV_ROWS, (h + 1) * MV_ROWS)

    q = [mq_ref[b, :, sl(h)] for b, h in chains]
    k = [mk_ref[b, :, sl(h)] for b, h in chains]
    vT = [mvT_ref[b, vsl(h), :] for b, h in chains]
    sT = _each(_dot_nt, k, q)
    r2T = _each(lambda qq, bh: _dot_nt(caugT_ref[bh[0], bh[1]].astype(BF16), qq), q, chains)

    ig_row = [cT_ref[b, h:h + 1, :] for b, h in chains]
    b_row = [cT_ref[b, H_M + h:H_M + h + 1, :] for b, h in chains]
    c_col = [c_ref[b, :, h:h + 1] - c_ref[b, :, H_M + h:H_M + h + 1] for b, h in chains]
    m_st = [m_ref[b, h:h + 1, 0:1] for b, h in chains]
    dlog = _each(lambda br, cc: jnp.where(valid, br + cc, NEG_INF), b_row, c_col)
    a_row = _each(lambda br, m: br + m, b_row, m_st)
    m_t = _each(lambda a, d: jnp.maximum(a, jnp.max(d, axis=0, keepdims=True)), a_row, dlog)
    sw = _each(lambda s, d, m: (s * jnp.exp(d - m)).astype(BF16), sT, dlog, m_t)
    w_inter = _each(lambda a, m: jnp.exp(a - m), a_row, m_t)
    floor = _each(lambda m: jnp.exp(-m), m_t)

    b_end = _each(lambda br: br[:, CHUNK - 1:CHUNK], b_row)
    a_end = _each(lambda be, m: be + m, b_end, m_st)
    wlog = _each(lambda be, br, ir: be - br + ir, b_end, b_row, ig_row)
    m_new = _each(lambda a, w: jnp.maximum(a, jnp.max(w, axis=1, keepdims=True)), a_end, wlog)
    ws = _each(lambda w, m: jnp.exp(w - m), wlog, m_new)
    decay = _each(lambda a, m: jnp.exp(a - m), a_end, m_new)
    vw = _each(lambda v, w: (v.astype(F32) * w).astype(BF16), vT, ws)

    r1T = _each(_dot, vT, sw)
    upd = _each(_dot, vw, k)

    num = _each(lambda w, a, c: w * a[:DV_M] + c[:DV_M], w_inter, r2T, r1T)
    den = _each(lambda w, a, c: w * a[DV_M:DV_M + 1] + c[DV_M:DV_M + 1], w_inter, r2T, r1T)
    hh = _each(lambda n, d, f: n / jnp.maximum(jnp.abs(d), f), num, den, floor)
    scale = _each(lambda x: lax.rsqrt(jnp.mean(x * x, axis=0, keepdims=True) + EPS), hh)
    gain = [jnp.broadcast_to(gmh_ref[sl(h), :], (DV_M, CHUNK)) for h in range(H_M)]
    for (b, h), x, sc, dc, u, mn in zip(chains, hh, scale, decay, upd, m_new):
        hmT_ref[b, sl(h), :] = (x * sc * gain[h] * smoT_ref[b, sl(h), :].astype(F32)).astype(BF16)
        caugT_ref[b, h] = dc * caugT_ref[b, h] + u
        m_ref[b, h:h + 1, :] = jnp.broadcast_to(mn, (1, LANES))


def _mlstm_prompt(mq, mk, mvT, smoT, c_tok, cT, gmh_col):
    b, s, _ = mq.shape
    tok = lambda w: pl.BlockSpec((b, CHUNK, w), lambda c: (0, c, 0))
    hmaj = lambda r: pl.BlockSpec((b, r, CHUNK), lambda c: (0, 0, c))
    whole = lambda shape: pl.BlockSpec(shape, lambda c: (0,) * len(shape))
    state = (b, H_M, MV_ROWS, DK_M)
    return pl.pallas_call(
        _mlstm_prompt_kernel,
        grid=(s // CHUNK,),
        in_specs=[tok(W_M), tok(W_M), hmaj(H_M * MV_ROWS), hmaj(W_M), tok(GATE_ROWS), hmaj(GATE_ROWS),
                  whole(gmh_col.shape)],
        out_specs=(hmaj(W_M), whole(state), whole((b, 8, LANES))),
        out_shape=(jax.ShapeDtypeStruct((b, W_M, s), BF16),
                   jax.ShapeDtypeStruct(state, F32),
                   jax.ShapeDtypeStruct((b, 8, LANES), F32)),
        compiler_params=_params("arbitrary"),
        name="mlstm_prompt",
    )(mq, mk, mvT, smoT, c_tok, cT, gmh_col)


LOOKAHEAD = 5
QT = 256


def _fox_prompt_kernel(fqT_ref, fk_ref, e_ref, vT_ref, o_ref, qa_sc, m_sc, acc_sc, *, tq, tk):
    qi = pl.program_id(1)
    ratio = tq // tk
    nqt = tq // QT
    rowq = lax.broadcasted_iota(jnp.int32, (2 * DH_F, QT), 0)
    srow = lax.broadcasted_iota(jnp.int32, (tk, QT), 0)
    tcol = lax.broadcasted_iota(jnp.int32, (tk, QT), 1)
    hsl = lambda h: slice(h * DH_F, (h + 1) * DH_F)
    psl = lambda h: slice((h // 2) * 2 * DH_F, (h // 2 + 1) * 2 * DH_F)
    qsl = lambda n: slice(n * QT, (n + 1) * QT)

    for h in range(H_F):
        lo = (h % 2) * DH_F
        sel = jnp.where((rowq % H_F == h) & (rowq < N_SPLIT * H_F), -1.0, 0.0).astype(BF16)
        for n in range(nqt):
            q2 = fqT_ref[0, psl(h), qsl(n)]
            qh = jnp.where((rowq >= lo) & (rowq < lo + DH_F), q2, 0)
            qa_sc[h, n] = jnp.concatenate([qh, sel], axis=0)
    m_sc[...] = jnp.full_like(m_sc, NEG_INF)
    acc_sc[...] = jnp.zeros_like(acc_sc)

    def run(tiles):
        k0 = [pl.multiple_of(j * tk, tk) for j, _ in tiles]
        e = [e_ref[pl.ds(k, tk), :] for k in k0]
        halves = [range(nqt) if d is None else range(d // QT, nqt) for _, d in tiles]
        work = [(ti, h, n) for ti in range(len(tiles)) for h in range(H_F) for n in halves[ti]]
        st = {}

        def scores(ti, h, n):
            ka = jnp.concatenate([fk_ref[pl.ds(k0[ti], tk), psl(h)], e[ti]], axis=1)
            st[ti, h, n] = _dot(ka, qa_sc[h, n])

        for w in work[:LOOKAHEAD]:
            scores(*w)
        for wi, (ti, h, n) in enumerate(work):
            if wi + LOOKAHEAD < len(work):
                scores(*work[wi + LOOKAHEAD])
            s_u = st.pop((ti, h, n))
            d = tiles[ti][1]
            if d is not None and d + tk > n * QT:
                s_u = jnp.where(srow + d <= tcol + n * QT, s_u, NEG_INF)
            m_old = m_sc[h, n]
            m_new = jnp.maximum(m_old, jnp.max(s_u, axis=0, keepdims=True))
            alpha = jnp.exp2(m_old - m_new)
            p = jnp.exp2(s_u - m_new).astype(BF16)
            m_sc[h, n] = m_new
            vT = vT_ref[0, h * V_ROWS:(h + 1) * V_ROWS, pl.ds(k0[ti], tk)]
            acc_sc[h, n] = alpha * acc_sc[h, n] + _dot(vT, p)

    def body(jj, carry):
        run([(jj * 2 * ratio + r, None) for r in range(2 * ratio)])
        return carry

    lax.fori_loop(0, qi // 2, body, 0)

    @pl.when(qi % 2 == 1)
    def _():
        run([((qi - 1) * ratio + r, None) for r in range(ratio)])

    run([(qi * ratio + r, r * tk) for r in range(ratio)])
    for h in range(H_F):
        for n in range(nqt):
            o_ref[0, hsl(h), qsl(n)] = (acc_sc[h, n, :DH_F, :] / acc_sc[h, n, DH_F:DH_F + 1, :]).astype(BF16)


def _fox_prompt(fqT, fk_tok, e_tok, fvT16, tq, tk):
    b, w, s = fqT.shape
    kern = functools.partial(_fox_prompt_kernel, tq=tq, tk=tk)
    return pl.pallas_call(
        kern,
        grid=(b, s // tq),
        in_specs=[pl.BlockSpec((1, w, tq), lambda bi, qi: (bi, 0, qi)),
                  pl.BlockSpec((s, w), lambda bi, qi: (bi, 0)),
                  pl.BlockSpec((s, GATE_ROWS), lambda bi, qi: (bi, 0)),
                  pl.BlockSpec((1, H_F * V_ROWS, s), lambda bi, qi: (bi, 0, 0))],
        out_specs=pl.BlockSpec((1, w, tq), lambda bi, qi: (bi, 0, qi)),
        out_shape=jax.ShapeDtypeStruct((b, w, s), BF16),
        scratch_shapes=[pltpu.VMEM((H_F, tq // QT, 2 * DH_F + GATE_ROWS, QT), BF16),
                        pltpu.VMEM((H_F, tq // QT, 1, QT), F32),
                        pltpu.VMEM((H_F, tq // QT, V_ROWS, QT), F32)],
        compiler_params=_params("parallel", "arbitrary"),
        name="fox_prompt",
    )(fqT, fk_tok, e_tok, fvT16)


FF_TILE = MXU_TILE
DOWN_TILE = 2 * MXU_TILE


def _merge_ffn_kernel(*refs, fox):
    refs = list(refs)
    pt_ref = refs.pop(0) if fox else None
    (x_ref, hmT_ref, yfT_ref, sga_ref, sgb_ref, p_ref, wpa_ref, wpb_ref, wout_ref, gffn_ref, wgu_ref,
     wdown_ref, gple_ref, wpg_ref, wpp_ref) = refs[:15]
    del refs[:15]
    fox_in = [refs.pop(0) for _ in range(7)] if fox else None
    y_ref = refs.pop(0)
    yf_ref = refs.pop(0) if fox else None
    hdn_sc = refs.pop(0)
    val = {}

    def g_merge():
        a = _dot_tn(hmT_ref[0], wpa_ref[...])
        bb = _dot_tn(yfT_ref[0], wpb_ref[...])
        u = sga_ref[...].astype(F32) * a + sgb_ref[...].astype(F32) * bb
        x1 = x_ref[...] + _dot(u.astype(BF16), wout_ref[...])
        val["x1"] = x1
        val["xn"] = (x1 * _rms_scale(x1) * gffn_ref[...]).astype(BF16)

    def g_ffn(j):
        def run():
            g = _dot(val["xn"], wgu_ref[:, j:j + FF_TILE])
            up = _dot(val["xn"], wgu_ref[:, D_FF + j:D_FF + j + FF_TILE])
            hdn_sc[:, j:j + FF_TILE] = (jax.nn.silu(g) * up).astype(BF16)
        return run

    def g_down(c):
        def run():
            val["x2", c] = val["x1"][:, c:c + DOWN_TILE] + _dot(hdn_sc[...], wdown_ref[:, c:c + DOWN_TILE])
        return run

    def g_out():
        x2 = jnp.concatenate([val["x2", c] for c in range(0, D_MODEL, DOWN_TILE)], axis=1)
        xg = (x2 * _rms_scale(x2) * gple_ref[...]).astype(BF16)
        gate = jax.nn.sigmoid(_dot(xg, wpg_ref[...]))
        y_ref[...] = x2 + gate * _dot(p_ref[...].astype(BF16), wpp_ref[...])

    kt = lambda k: -(-k // MXU_TILE)
    groups = [(g_merge, 2 * kt(W_M) * kt(D_MODEL) + kt(D_MODEL) ** 2)]
    groups += [(g_ffn(j), 2 * kt(D_MODEL)) for j in range(0, D_FF, FF_TILE)]
    groups += [(g_down(c), kt(D_FF) * kt(DOWN_TILE)) for c in range(0, D_MODEL, DOWN_TILE)]
    groups += [(g_out, kt(D_MODEL) ** 2 + kt(PLE_DIM) * kt(D_MODEL))]
    if fox:
        _run_with_hosted_fox(groups, pl.program_id(0), fox, pt_ref, *fox_in, yf_ref, *refs)
    else:
        for g, _ in groups:
            g()


def _merge_ffn(x, hmT, yfT, sga, sgb, p, wpa, wpb, wout, gffn, wgu, wdown, gple, wpg, wpp, tm, fox=None):
    t, d = x.shape
    b, _, s = yfT.shape
    nt = s // tm
    steps = t // tm
    tok = lambda w: pl.BlockSpec((tm, w), lambda i, *_: (i, 0))
    full = lambda a: pl.BlockSpec(a.shape, lambda i, *_: (0,) * a.ndim)
    hmaj = lambda r: pl.BlockSpec((1, r, tm), lambda i, *_: (i // nt, 0, i % nt))
    in_specs = [tok(d), hmaj(W_M), hmaj(W_F), tok(d), tok(d), tok(PLE_DIM),
                full(wpa), full(wpb), full(wout), full(gffn), full(wgu), full(wdown),
                full(gple), full(wpg), full(wpp)]
    operands = (x, hmT, yfT, sga, sgb, p, wpa, wpb, wout, gffn, wgu, wdown, gple, wpg, wpp)
    hdn = pltpu.VMEM((tm, D_FF), BF16)
    if fox is None:
        return pl.pallas_call(
            functools.partial(_merge_ffn_kernel, fox=None),
            grid=(steps,),
            in_specs=in_specs,
            out_specs=tok(d),
            out_shape=jax.ShapeDtypeStruct((t, d), F32),
            scratch_shapes=[hdn],
            compiler_params=_params("parallel"),
            name="merge_ffn",
        )(*operands)

    page_table, fq, kn, vn, lfn, ck, cv, cl = fox
    db, n_tok, _ = fq.shape
    n_pages, page = page_table.shape[1], ck.shape[-1]
    per = db // steps
    assert per * steps == db and per % 2 == 0
    seqs = lambda a: pl.BlockSpec((per,) + a.shape[1:], lambda i, *_: (i,) + (0,) * (a.ndim - 1))
    hbm = pl.BlockSpec(memory_space=pl.ANY)
    return pl.pallas_call(
        functools.partial(_merge_ffn_kernel, fox=(per, db, n_pages, n_tok)),
        grid_spec=pltpu.PrefetchScalarGridSpec(
            num_scalar_prefetch=1, grid=(steps,),
            in_specs=in_specs + [seqs(fq), seqs(kn), seqs(vn), seqs(lfn), hbm, hbm, hbm],
            out_specs=(tok(d), seqs(fq)),
            scratch_shapes=[hdn,
                            pltpu.VMEM((2, n_pages, W_F, page), F32),
                            pltpu.VMEM((2, n_pages, W_F, page), F32),
                            pltpu.VMEM((2, n_pages, H_F, page), F32),
                            pltpu.SemaphoreType.DMA((2,))]),
        out_shape=(jax.ShapeDtypeStruct((t, d), F32), jax.ShapeDtypeStruct(fq.shape, F32)),
        compiler_params=_params("arbitrary", vmem=VMEM_LIMIT_HOST),
        name="merge_ffn_fox_sample",
    )(page_table, *operands, fq, kn, vn, lfn, ck, cv, cl)


_SPLIT = (W_M, W_M, W_M, W_M, H_M, H_M, W_F, W_F, W_F, H_F, D_MODEL, D_MODEL)


def _prep_weights(g_mix, w_in, b_mi, b_mf, b_ff, g_mh, g_qn, g_kn, w_pa, w_pb, w_out, g_ffn, w_gu,
                  w_down, g_ple, w_ple_gate, w_ple_proj):
    offs = [0]
    for sz in _SPLIT:
        offs.append(offs[-1] + sz)
    o_mq, _, o_mv, o_mo, o_mi, o_mf, o_fq, _, _, o_ff, o_ga, _, o_end = offs
    w = w_in[0]
    wT = w.T
    wn = jnp.concatenate([wT[o_mq:o_mv], wT[o_mo:o_mi], wT[o_ga:o_end]], axis=0).astype(BF16)
    wt = jnp.concatenate([wT[o_fq:o_ff], wT[o_mv:o_mo], wT[o_mi:o_fq], wT[o_ff:o_ga],
                          jnp.zeros((GATE_ROWS - N_GATES, D_MODEL), F32)], axis=0).astype(BF16)
    bias_col = jnp.concatenate([b_mi[0], b_mf[0], b_ff[0], jnp.zeros((GATE_ROWS - N_GATES,), F32)])[:, None]
    gain_col = jnp.concatenate([jnp.tile(g_qn[0], H_F) * (F_SCALE * LOG2E), jnp.tile(g_kn[0], H_F)])[:, None]
    return dict(
        g_mix=g_mix, wn=wn, wt=wt, bias_col=bias_col, gain_col=gain_col,
        gmh=g_mh[0].reshape(1, W_M),
        wpa=w_pa[0].astype(BF16), wpb=w_pb[0].astype(BF16), wout=w_out[0].astype(BF16),
        gffn=g_ffn, wgu=w_gu[0].astype(BF16), wdown=w_down[0].astype(BF16),
        gple=g_ple, wpg=w_ple_gate[0].astype(BF16), wpp=w_ple_proj[0].astype(BF16))


def _prompt_path(x, p, wd, tm_in, tm_ffn, tq, tk, fox):
    b, s, d = x.shape
    (mq, mk, mvT, smoT, sga, sgb, fqT, fkT, fvT, fk_tok, fvT16, gT) = _in_proj(
        x, wd["g_mix"], wd["wn"], wd["wt"], wd["bias_col"], wd["gain_col"], tm_in)
    cT, c_tok, e_tok = _gate_scan(gT)
    seq3 = lambda a: a.reshape(b, s, a.shape[-1])
    hmT, caugT, m_out = _mlstm_prompt(seq3(mq), seq3(mk), mvT, smoT, seq3(c_tok), cT, wd["gmh"].reshape(W_M, 1))
    yfT = _fox_prompt(fqT, fk_tok, e_tok, fvT16, tq, tk)
    y, yf_sample = _merge_ffn(
        x.reshape(b * s, d), hmT, yfT, sga, sgb, p.reshape(b * s, PLE_DIM),
        wd["wpa"], wd["wpb"], wd["wout"], wd["gffn"], wd["wgu"], wd["wdown"],
        wd["gple"], wd["wpg"], wd["wpp"], tm_ffn, fox)
    to_cache = lambda aT: aT.reshape(b, H_F, DH_F, s).transpose(0, 3, 1, 2)[None]
    logf = gT[:, 2 * H_M:N_GATES, :].transpose(0, 2, 1)[None]
    c_out = caugT[:, :, :DV_M, :].transpose(0, 1, 3, 2)[None]
    return (y.reshape(b, s, d), to_cache(fkT), to_cache(fvT), logf,
            c_out, caugT[None, :, :, DV_M, :], m_out[None, :, :H_M, 0]), yf_sample


L_PAD = 16
SEQ_BLOCK = 8


def _mlstm_sample_kernel(q_ref, k_ref, v_ref, smo_ref, grow_ref, gcol_ref, c_ref, n_ref, m_ref, gmh_ref,
                         hm_ref, co_ref, no_ref, mo_ref, *, n_valid):
    length = L_PAD
    row = lax.broadcasted_iota(jnp.int32, (length, length), 0)
    col = lax.broadcasted_iota(jnp.int32, (length, length), 1)
    valid = (col <= row) & (col < n_valid)
    chains = [(i, h) for i in range(SEQ_BLOCK) for h in range(H_M)]
    sl = lambda h: slice(h * DK_M, (h + 1) * DK_M)

    q = [q_ref[i, :, sl(h)] for i, h in chains]
    k = [k_ref[i, :, sl(h)] for i, h in chains]
    v = [v_ref[i, :, sl(h)] for i, h in chains]
    n_st = [n_ref[i, h:h + 1, :] for i, h in chains]
    s_mat = _each(_dot_nt, q, k)
    r2 = _each(lambda qq, ih: _dot(qq, c_ref[ih[0], ih[1]].astype(BF16)), q, chains)
    qn = _each(lambda qq, n: jnp.sum(qq.astype(F32) * n, axis=1, keepdims=True), q, n_st)

    lf_row = [grow_ref[i, H_M + h:H_M + h + 1, 0:length] for i, h in chains]
    lf_col = [gcol_ref[i, :, H_M + h:H_M + h + 1] for i, h in chains]
    b_col = _each(lambda r: jnp.sum(jnp.where(col <= row, r, 0.0), axis=1, keepdims=True), lf_row)
    b_row = _each(lambda c: jnp.sum(jnp.where(row <= col, c, 0.0), axis=0, keepdims=True), lf_col)
    w_intra, w_inter, floor, ws, decay, m_new = _mlstm_gates(
        [grow_ref[i, h:h + 1, 0:length] for i, h in chains], b_row,
        [gcol_ref[i, :, h:h + 1] for i, h in chains], b_col,
        [m_ref[i:i + 1, h:h + 1] for i, h in chains], valid, n_valid)
    sw = _each(lambda s, w: s * w, s_mat, w_intra)
    kw = _each(lambda kk, w: kk.astype(F32) * w, k, ws)

    r1 = _each(lambda s, vv: _dot(s.astype(BF16), vv), sw, v)
    upd = _each(lambda kk, vv: _dot_tn(kk.astype(BF16), vv), kw, v)
    sw_sum = _each(lambda s: jnp.sum(s, axis=1, keepdims=True), sw)
    kw_sum = _each(lambda kk: jnp.sum(kk, axis=0, keepdims=True), kw)

    num = _each(lambda w, a, c: w * a + c, w_inter, r2, r1)
    den = _each(lambda w, a, c: w * a + c, w_inter, qn, sw_sum)
    hh = _each(lambda n, d, f: n / jnp.maximum(jnp.abs(d), f), num, den, floor)
    scale = _each(_rms_scale, hh)
    for (i, h), x, sc, dc, u, n, ks, mn in zip(chains, hh, scale, decay, upd, n_st, kw_sum, m_new):
        hm_ref[i, :, sl(h)] = (x * sc * gmh_ref[:, sl(h)] * smo_ref[i, :, sl(h)].astype(F32)).astype(BF16)
        co_ref[i, h] = dc * c_ref[i, h] + u
        no_ref[i, h:h + 1, :] = dc * n + ks
        mo_ref[i:i + 1, h:h + 1] = mn


def _mlstm_sample(q, k, v, smo, grow, gcol, c_st, n_st, m_st, gmh, n_valid):
    db = q.shape[0]
    blk = lambda a: pl.BlockSpec((SEQ_BLOCK,) + a.shape[1:], lambda i: (i,) + (0,) * (a.ndim - 1))
    kern = functools.partial(_mlstm_sample_kernel, n_valid=n_valid)
    return pl.pallas_call(
        kern,
        grid=(db // SEQ_BLOCK,),
        in_specs=[blk(q), blk(k), blk(v), blk(smo), blk(grow), blk(gcol), blk(c_st), blk(n_st), blk(m_st),
                  pl.BlockSpec(gmh.shape, lambda i: (0, 0))],
        out_specs=(blk(q), blk(c_st), blk(n_st), blk(m_st)),
        out_shape=(jax.ShapeDtypeStruct(q.shape, BF16), jax.ShapeDtypeStruct(c_st.shape, F32),
                   jax.ShapeDtypeStruct(n_st.shape, F32), jax.ShapeDtypeStruct(m_st.shape, F32)),
        compiler_params=_params("parallel"),
        name="mlstm_sample",
    )(q, k, v, smo, grow, gcol, c_st, n_st, m_st, gmh)


def _run_with_hosted_fox(groups, step, fox, pt_ref, fq_ref, kn_ref, vn_ref, lfn_ref, ck_hbm, cv_hbm, cl_hbm,
                         yf_ref, kbuf, vbuf, lbuf, sem):
    per, total, n_pages, n_tok = fox
    first = step * per

    def page_copies(n, half):
        copies = []
        for j in range(n_pages):
            pid = pt_ref[n, j]
            copies.append(pltpu.make_async_copy(ck_hbm.at[pid], kbuf.at[half, j], sem.at[half]))
            copies.append(pltpu.make_async_copy(cv_hbm.at[pid], vbuf.at[half, j], sem.at[half]))
            copies.append(pltpu.make_async_copy(cl_hbm.at[pid], lbuf.at[half, j], sem.at[half]))
        return copies

    @pl.when(step == 0)
    def _():
        for c in page_copies(0, 0):
            c.start()

    total_w = sum(w for _, w in groups)
    runs, acc_w = [[] for _ in range(per)], 0
    for g, w in groups:
        runs[min(per - 1, per * (2 * acc_w + w) // (2 * total_w))].append(g)
        acc_w += w
    assert all(runs)
    for j in range(per):
        n, half = first + j, j % 2
        mine = runs[j]

        def prefetch():
            for c in page_copies(n + 1, 1 - half):
                c.start()

        if j + 1 < per:
            prefetch()
        else:
            pl.when(n + 1 < total)(prefetch)

        for c in page_copies(n, half):
            c.wait()
        scores = _fox_sample_scores(fq_ref[j], kn_ref[j], lfn_ref[j], lambda pg: kbuf[half, pg],
                                    lbuf[half].reshape(n_pages * H_F, LANES), n_pages, n_tok)
        for g in mine[:1]:
            g()
        yf_ref[j] = _fox_sample_output(*scores, lambda pg: vbuf[half, pg], vn_ref[j], n_pages, n_tok)
        for g in mine[1:]:
            g()


def _own_lanes():
    sub = lax.broadcasted_iota(jnp.int32, (H_F, W_F), 0)
    head_of_lane = lax.broadcasted_iota(jnp.int32, (H_F, W_F), 1) // DH_F
    return sub == head_of_lane


def _fox_sample_scores(fq, kn_new, lfn, k_page, lf_pages, n_pages, n_tok):
    rows = n_tok * H_F
    own = _own_lanes()
    tile = lambda a: jnp.concatenate([a] * n_tok, axis=0)
    qbd = jnp.concatenate(
        [jnp.where(own, jnp.broadcast_to(fq[t:t + 1, :], (H_F, W_F)), 0.0) for t in range(n_tok)], axis=0)
    qbd16 = qbd.astype(BF16)

    incl_all = _lane_cumsum(lf_pages)
    bias = [None] * n_pages
    carry = jnp.zeros((H_F, 1), F32)
    for pg in reversed(range(n_pages)):
        incl = incl_all[pg * H_F:(pg + 1) * H_F, :]
        tot = incl[:, LANES - 1:LANES]
        bias[pg] = (tot - incl + carry) * LOG2E
        carry = carry + tot

    logits = []
    for j in range(0, n_pages, 2):
        kcat = jnp.concatenate([k_page(j).astype(BF16), k_page(j + 1).astype(BF16)], axis=1)
        bcat = jnp.concatenate([tile(bias[j]), tile(bias[j + 1])], axis=1)
        logits.append(_dot(qbd16, kcat) + bcat)

    rowi = lax.broadcasted_iota(jnp.int32, (rows, 1), 0)
    qf = qbd16.astype(F32)
    new_logits = []
    f_new = jnp.zeros((H_F, 1), F32)
    for s in range(n_tok):
        f_new = f_new + lfn[:, s:s + 1]
        kn = kn_new[s:s + 1, :].astype(BF16).astype(F32)
        ln = jnp.sum(qf * kn, axis=1, keepdims=True) - tile(f_new) * LOG2E
        new_logits.append(jnp.where(rowi >= s * H_F, ln, NEG_INF))

    m = new_logits[0]
    for ln in new_logits[1:]:
        m = jnp.maximum(m, ln)
    for lg in logits:
        m = jnp.maximum(m, jnp.max(lg, axis=1, keepdims=True))
    return logits, new_logits, m


def _fox_sample_output(logits, new_logits, m, v_page, vn_new, n_pages, n_tok):
    rows = n_tok * H_F
    own = _own_lanes()
    denom = jnp.zeros((rows, 1), F32)
    acc = jnp.zeros((rows, W_F), F32)
    for idx, lg in enumerate(logits):
        p = jnp.exp2(lg - m)
        denom = denom + jnp.sum(p, axis=1, keepdims=True)
        j = 2 * idx
        vcat = jnp.concatenate([v_page(j).astype(BF16), v_page(j + 1).astype(BF16)], axis=1)
        acc = acc + _dot_nt(p.astype(BF16), vcat)
    for s, ln in enumerate(new_logits):
        p = jnp.exp2(ln - m)
        denom = denom + p
        acc = acc + p.astype(BF16).astype(F32) * vn_new[s:s + 1, :]
    out = acc / denom
    return jnp.concatenate(
        [jnp.sum(jnp.where(own, out[t * H_F:(t + 1) * H_F, :], 0.0), axis=0, keepdims=True)
         for t in range(n_tok)], axis=0)


def _sample_front(x, cache_k, cache_v, cache_logf, state_c, state_n, state_m, page_table, wd):
    db, n_tok, d = x.shape
    t = db * n_tok
    (mq, mk, mvT, smoT, sga, sgb, fqT, fkT, fvT, fk_tok, _, gT) = _in_proj(
        x.reshape(1, t, d), wd["g_mix"], wd["wn"], wd["wt"], wd["bias_col"], wd["gain_col"], t)
    mv = mvT[0].reshape(H_M, MV_ROWS, t)[:, :DV_M].reshape(W_M, t).T
    smo = smoT[0].T

    pad_tok = lambda a: jnp.pad(a.reshape(db, n_tok, -1), ((0, 0), (0, L_PAD - n_tok), (0, 0)))
    g16 = gT[0, :2 * 8].reshape(2 * 8, db, n_tok)
    grow = jnp.pad(g16.transpose(1, 0, 2), ((0, 0), (0, 0), (0, LANES - n_tok)))
    gcol = jnp.pad(g16.transpose(1, 2, 0), ((0, 0), (0, L_PAD - n_tok), (0, LANES - 2 * 8)))
    hm, c_new, n_new, m_new = _mlstm_sample(
        pad_tok(mq), pad_tok(mk), pad_tok(mv), pad_tok(smo), grow, gcol,
        state_c, state_n, state_m, wd["gmh"], n_tok)
    hmT = hm[:, :n_tok].reshape(t, W_M).T[None]

    pool, page = cache_k.shape[0], cache_k.shape[1]
    ck = cache_k.transpose(0, 2, 3, 1).reshape(pool, W_F, page)
    cv = cache_v.transpose(0, 2, 3, 1).reshape(pool, W_F, page)
    cl = cache_logf.transpose(0, 2, 1)
    tok3 = lambda aT: aT[0].T.astype(F32).reshape(db, n_tok, W_F)
    lfn = jnp.pad(gT[0, 2 * H_M:N_GATES].reshape(H_F, db, n_tok).transpose(1, 0, 2),
                  ((0, 0), (0, 0), (0, LANES - n_tok)))
    fox = (page_table, tok3(fqT), fk_tok.astype(F32).reshape(db, n_tok, W_F), tok3(fvT), lfn, ck, cv, cl)
    to_cache = lambda aT: aT[0].reshape(H_F, DH_F, db, n_tok).transpose(2, 3, 0, 1)[None]
    logf = gT[0, 2 * H_M:N_GATES].reshape(H_F, db, n_tok).transpose(1, 2, 0)[None]
    outs = (to_cache(fkT), to_cache(fvT), logf, c_new[None], n_new[None], m_new[None])
    return fox, (hmT, sga, sgb), outs


def _sample_back(x, p, yf, hmT, sga, sgb, wd):
    db, n_tok, d = x.shape
    t = db * n_tok
    yfT = yf.reshape(t, W_F).T.astype(BF16)[None]
    y = _merge_ffn(x.reshape(t, d), hmT, yfT, sga, sgb, p.reshape(t, PLE_DIM),
                   wd["wpa"], wd["wpb"], wd["wout"], wd["gffn"], wd["wgu"], wd["wdown"],
                   wd["gple"], wd["wpg"], wd["wpp"], t)
    return y.reshape(db, n_tok, d)


def kernel(x_prompt, x_sample, p_prompt, p_sample, cache_k, cache_v, cache_logf, state_C, state_n, state_m,
           page_table, g_mix, w_in, b_mi, b_mf, b_ff, g_mh, g_qn, g_kn, w_pa, w_pb, w_out, g_ffn, w_gu,
           w_down, g_ple, w_ple_gate, w_ple_proj):
    wd = _prep_weights(g_mix, w_in, b_mi, b_mf, b_ff, g_mh, g_qn, g_kn, w_pa, w_pb, w_out, g_ffn, w_gu,
                       w_down, g_ple, w_ple_gate, w_ple_proj)
    fox, mid, (ks, vs, lfs, cs, ns, ms) = _sample_front(
        x_sample, cache_k[0], cache_v[0], cache_logf[0], state_C[0], state_n[0], state_m[0], page_table, wd)
    (yp, kp, vp, lfp, cp, np_, mp), yf_sample = _prompt_path(
        x_prompt, p_prompt[0], wd, PROJ_TILE, TOKEN_TILE, QUERY_TILE, KEY_TILE, fox)
    ys = _sample_back(x_sample, p_sample[0], yf_sample, *mid, wd)
    return (yp, ys, kp, vp, lfp, cp, np_, mp, ks, vs, lfs, cs, ns, ms)
```
